```python
import math
import jax, jax.numpy as jnp
from jax import lax
import numpy as np

D_MODEL = 1024
BATCH = 16
SEQ = 4096
DEPTH = 1

N_META = 16
RET_HEADS = 4
RET_WIDTH = D_MODEL // 2
RET_DV = RET_WIDTH // RET_HEADS
RET_DK = RET_DV // 2
RET_QK_WIDTH = RET_HEADS * RET_DK
CHUNK = 128
ROPE_BASE = 10000.0
POOL_WINDOWS = (2, 4, 8, 16)
POOL_GROUPS = len(POOL_WINDOWS)
POOL_WIDTH = D_MODEL // 2
POOL_CH = POOL_WIDTH // POOL_GROUPS
MIX_WIDTH = RET_WIDTH + POOL_WIDTH
IN_COLS = 2 * RET_QK_WIDTH + 2 * RET_WIDTH + POOL_WIDTH
N_GROUPS = 4
EXPERTS_PER_GROUP = 8
N_EXPERTS = N_GROUPS * EXPERTS_PER_GROUP
D_EXPERT = D_MODEL // 2
TOP_K = 2
MOE_BLOCK = 128
LN_EPS = 1e-5
GN_EPS = 1e-6
ALPHA = (2 * DEPTH) ** 0.25
BETA = (8 * DEPTH) ** -0.25

kernel_name = 'hybrid_retention_pool_hmoe'


def layer_norm(x, g, b):
    xf = x.astype(jnp.float32)
    mu = jnp.mean(xf, axis=-1, keepdims=True)
    var = jnp.mean(jnp.square(xf - mu), axis=-1, keepdims=True)
    return ((xf - mu) * lax.rsqrt(var + LN_EPS) * g.astype(jnp.float32) + b.astype(jnp.float32)).astype(x.dtype)


def rotary(x, pos):
    half = x.shape[-1] // 2
    inv = ROPE_BASE ** (-jnp.arange(half, dtype=jnp.float32) / half)
    ang = pos.astype(jnp.float32)[:, None] * inv[None, :]
    cos = jnp.cos(ang)[None, :, None, :]
    sin = jnp.sin(ang)[None, :, None, :]
    x1, x2 = x[..., :half], x[..., half:]
    return jnp.concatenate([x1 * cos - x2 * sin, x1 * sin + x2 * cos], axis=-1).astype(x.dtype)


def retention_chunkwise(q, k, v):
    B, L, H, _ = q.shape
    pad = CHUNK - N_META
    padw = ((0, 0), (pad, 0), (0, 0), (0, 0))
    q, k, v = jnp.pad(q, padw), jnp.pad(k, padw), jnp.pad(v, padw)
    Lp = L + pad
    nc = Lp // CHUNK
    qc = q.reshape(B, nc, CHUNK, H, RET_DK)
    kc = k.reshape(B, nc, CHUNK, H, RET_DK)
    vc = v.reshape(B, nc, CHUNK, H, RET_DV)
    log_g = jnp.log1p(-jnp.power(2.0, -5.0 - jnp.arange(H, dtype=jnp.float32)))
    i = jnp.arange(CHUNK, dtype=jnp.float32)
    rel = i[:, None] - i[None, :]
    dmask = jnp.where(rel[None] >= 0, jnp.exp(jnp.maximum(rel, 0.0)[None] * log_g[:, None, None]), 0.0)
    scores = jnp.einsum('bnihd,bnjhd->bnhij', qc, kc) * dmask
    inner = jnp.einsum('bnhij,bnjhv->bnihv', scores, vc)
    zeta = jnp.exp((CHUNK - 1 - i)[None, :] * log_g[:, None])
    kv = jnp.einsum('bnjhd,hj,bnjhv->nbhdv', kc, zeta, vc)
    chunk_decay = jnp.exp(CHUNK * log_g)[:, None, None]

    def step(state, kv_n):
        return chunk_decay * state + kv_n, state

    _, prev = lax.scan(step, jnp.zeros((B, H, RET_DK, RET_DV), kv.dtype), kv)
    xi = jnp.exp((i + 1)[None, :] * log_g[:, None])
    cross = jnp.einsum('bnihd,nbhdv,hi->bnihv', qc, prev, xi)
    out = (inner + cross).reshape(B, Lp, H, RET_DV)
    return out[:, pad:]


def head_group_norm(y):
    yf = y.astype(jnp.float32)
    mu = jnp.mean(yf, axis=-1, keepdims=True)
    var = jnp.mean(jnp.square(yf - mu), axis=-1, keepdims=True)
    return (yf - mu) * lax.rsqrt(var + GN_EPS)


def causal_multiscale_pool(u):
    L = u.shape[1]
    uf = u.astype(jnp.float32)
    cs = jnp.cumsum(uf, axis=1)
    t = jnp.arange(L)
    outs = []
    for gi, w in enumerate(POOL_WINDOWS):
        sl = slice(gi * POOL_CH, (gi + 1) * POOL_CH)
        c = cs[..., sl]
        shifted = jnp.pad(c, ((0, 0), (w, 0), (0, 0)))[:, :L]
        count = jnp.minimum(t + 1, w).astype(jnp.float32)[None, :, None]
        outs.append((c - shifted) / count - uf[..., sl])
    return jnp.concatenate(outs, axis=-1)


def token_mixer(h, w_in, pool_w, pool_b, pool_scale, w_out):
    B, L, _ = h.shape
    proj = h @ w_in
    q, k, v, gate, u = jnp.split(
        proj, [RET_QK_WIDTH, 2 * RET_QK_WIDTH, 2 * RET_QK_WIDTH + RET_WIDTH, 2 * RET_QK_WIDTH + 2 * RET_WIDTH], axis=-1)
    pos = jnp.arange(L)
    q = rotary(q.reshape(B, L, RET_HEADS, RET_DK), pos)
    k = rotary(k.reshape(B, L, RET_HEADS, RET_DK), pos) * (RET_DK ** -0.5)
    ret = retention_chunkwise(q, k, v.reshape(B, L, RET_HEADS, RET_DV))
    ret = head_group_norm(ret).reshape(B, L, RET_WIDTH).astype(h.dtype)
    ret_out = jax.nn.silu(gate) * ret
    pooled = causal_multiscale_pool(u).astype(h.dtype).reshape(B, L, POOL_GROUPS, POOL_CH)
    pooled = jnp.einsum('blgc,gcd->blgd', pooled, pool_w) + pool_b[None, None]
    pool_out = pooled.reshape(B, L, POOL_WIDTH) * pool_scale
    return jnp.concatenate([ret_out, pool_out], axis=-1) @ w_out


def hierarchical_moe(h2d, rg_w, rg_b, re_w, re_b, w_gate, w_up, w_down):
    N = h2d.shape[0]
    g_logits = (h2d @ rg_w + rg_b).astype(jnp.float32)
    g_p, g_idx = lax.top_k(jax.nn.softmax(g_logits, axis=-1), 1)
    e_logits = (h2d @ re_w + re_b).astype(jnp.float32).reshape(N, N_GROUPS, EXPERTS_PER_GROUP)
    sel = e_logits[jnp.arange(N), g_idx[:, 0]]
    e_p, e_idx = lax.top_k(jax.nn.softmax(sel, axis=-1), TOP_K)
    e_p = e_p / jnp.sum(e_p, axis=-1, keepdims=True)
    pair_w = (g_p * e_p).reshape(-1).astype(h2d.dtype)
    pair_e = (g_idx * EXPERTS_PER_GROUP + e_idx).reshape(-1).astype(jnp.int32)
    pair_tok = jnp.repeat(jnp.arange(N, dtype=jnp.int32), TOP_K)
    P = N * TOP_K
    order = jnp.argsort(pair_e)
    se = pair_e[order]
    stok = pair_tok[order]
    sw = pair_w[order]
    counts = jnp.bincount(pair_e, length=N_EXPERTS)
    starts = jnp.cumsum(counts) - counts
    padded = ((counts + MOE_BLOCK - 1) // MOE_BLOCK) * MOE_BLOCK
    pends = jnp.cumsum(padded)
    pstarts = pends - padded
    dest = pstarts[se] + (jnp.arange(P, dtype=jnp.int32) - starts[se])
    n_blocks = P // MOE_BLOCK + N_EXPERTS + 1
    buf = jnp.zeros((n_blocks * MOE_BLOCK, h2d.shape[1]), h2d.dtype).at[dest].set(h2d[stok])
    block_start = jnp.arange(n_blocks, dtype=jnp.int32) * MOE_BLOCK
    block_e = jnp.minimum(jnp.searchsorted(pends, block_start, side='right'), N_EXPERTS - 1)

    def expert_block(args):
        xb, e = args
        return (jax.nn.silu(xb @ w_gate[e]) * (xb @ w_up[e])) @ w_down[e]

    y_buf = lax.map(expert_block, (buf.reshape(n_blocks, MOE_BLOCK, -1), block_e)).reshape(n_blocks * MOE_BLOCK, -1)
    y_pairs = y_buf[dest] * sw[:, None]
    return jax.ops.segment_sum(y_pairs, stok, num_segments=N)


def setup_inputs(seed: int = 0) -> dict:
    key = jax.random.key(seed)
    ks = jax.random.split(key, 22)

    def nrm(k, shape, scale):
        return jax.random.normal(k, shape, jnp.float32) * scale

    return {
        'x': nrm(ks[0], (BATCH, SEQ, D_MODEL), 1.0),
        'meta_tokens': nrm(ks[1], (N_META, D_MODEL), 1.0),
        'ln_emb_g': 1.0 + nrm(ks[2], (D_MODEL,), 0.02),
        'ln_emb_b': nrm(ks[3], (D_MODEL,), 0.02),
        'w_in': nrm(ks[4], (DEPTH, D_MODEL, IN_COLS), D_MODEL ** -0.5),
        'pool_w': nrm(ks[5], (DEPTH, POOL_GROUPS, POOL_CH, POOL_CH), POOL_CH ** -0.5),
        'pool_b': nrm(ks[6], (DEPTH, POOL_GROUPS, POOL_CH), 0.01),
        'pool_scale': 1.0 + nrm(ks[7], (DEPTH, POOL_WIDTH), 0.02),
        'w_out': nrm(ks[8], (DEPTH, MIX_WIDTH, D_MODEL), BETA * MIX_WIDTH ** -0.5),
        'ln1_g': 1.0 + nrm(ks[9], (DEPTH, D_MODEL), 0.02),
        'ln1_b': nrm(ks[10], (DEPTH, D_MODEL), 0.02),
        'router_group_w': nrm(ks[11], (DEPTH, D_MODEL, N_GROUPS), D_MODEL ** -0.5),
        'router_group_b': nrm(ks[12], (DEPTH, N_GROUPS), 0.01),
        'router_expert_w': nrm(ks[13], (DEPTH, D_MODEL, N_EXPERTS), D_MODEL ** -0.5),
        'router_expert_b': nrm(ks[14], (DEPTH, N_EXPERTS), 0.01),
        'expert_w_gate': nrm(ks[15], (DEPTH, N_EXPERTS, D_MODEL, D_EXPERT), D_MODEL ** -0.5),
        'expert_w_up': nrm(ks[16], (DEPTH, N_EXPERTS, D_MODEL, D_EXPERT), D_MODEL ** -0.5),
        'expert_w_down': nrm(ks[17], (DEPTH, N_EXPERTS, D_EXPERT, D_MODEL), BETA * D_EXPERT ** -0.5),
        'ln2_g': 1.0 + nrm(ks[18], (DEPTH, D_MODEL), 0.02),
        'ln2_b': nrm(ks[19], (DEPTH, D_MODEL), 0.02),
    }


def reference(x, meta_tokens, ln_emb_g, ln_emb_b, w_in, pool_w, pool_b, pool_scale, w_out,
              ln1_g, ln1_b, router_group_w, router_group_b, router_expert_w, router_expert_b,
              expert_w_gate, expert_w_up, expert_w_down, ln2_g, ln2_b):
    B = x.shape[0]
    meta = jnp.broadcast_to(meta_tokens[None].astype(x.dtype), (B, N_META, D_MODEL))
    h = jnp.concatenate([meta, x], axis=1)
    h = layer_norm(h, ln_emb_g, ln_emb_b)
    for l in range(DEPTH):
        mix = token_mixer(h, w_in[l], pool_w[l], pool_b[l], pool_scale[l], w_out[l])
        h = layer_norm(ALPHA * h + mix, ln1_g[l], ln1_b[l])
        y = hierarchical_moe(h.reshape(-1, D_MODEL), router_group_w[l], router_group_b[l],
                             router_expert_w[l], router_expert_b[l],
                             expert_w_gate[l], expert_w_up[l], expert_w_down[l]).reshape(h.shape)
        h = layer_norm(ALPHA * h + y, ln2_g[l], ln2_b[l])
    return h[:, N_META:]
```

```python
import functools

import jax
import jax.numpy as jnp
from jax import lax
from jax.experimental import pallas as pl
from jax.experimental.pallas import tpu as pltpu

D_MODEL = 1024
DEPTH = 1
N_META = 16
RET_HEADS = 4
RET_WIDTH = D_MODEL // 2
RET_DV = RET_WIDTH // RET_HEADS
RET_DK = RET_DV // 2
RET_QK_WIDTH = RET_HEADS * RET_DK
CHUNK = 128
ROPE_BASE = 10000.0
POOL_WINDOWS = (2, 4, 8, 16)
POOL_GROUPS = len(POOL_WINDOWS)
POOL_WIDTH = D_MODEL // 2
POOL_CH = POOL_WIDTH // POOL_GROUPS
IN_COLS = 2 * RET_QK_WIDTH + 2 * RET_WIDTH + POOL_WIDTH
N_GROUPS = 4
EXPERTS_PER_GROUP = 8
N_EXPERTS = N_GROUPS * EXPERTS_PER_GROUP
D_EXPERT = D_MODEL // 2
TOP_K = 2
LN_EPS = 1e-5
GN_EPS = 1e-6
ALPHA = (2 * DEPTH) ** 0.25

Q0, K0, V0, G0, U0 = 0, RET_QK_WIDTH, 2 * RET_QK_WIDTH, 2 * RET_QK_WIDTH + RET_WIDTH, 2 * RET_QK_WIDTH + 2 * RET_WIDTH

MIX_ROWS = 512
EXPERT_ROWS = 256
DISPATCH_ROWS = 256
ROUTER_ROWS = 40
VMEM_LIMIT = 56 * 1024 * 1024

F32 = jnp.float32
BF16 = jnp.bfloat16


def _layer_norm(x, g, b, eps):
    mu = jnp.mean(x, axis=-1, keepdims=True)
    xc = x - mu
    var = jnp.mean(xc * xc, axis=-1, keepdims=True)
    return xc * lax.rsqrt(var + eps) * g + b


def _rotary(z, cos, sin_signed, first_half):
    partner = jnp.where(first_half, pltpu.roll(z, RET_QK_WIDTH - RET_DK // 2, 1), pltpu.roll(z, RET_DK // 2, 1))
    return z * cos + partner * sin_signed


def _first_half_mask(rows):
    lane = lax.broadcasted_iota(jnp.int32, (rows, RET_QK_WIDTH), 1)
    return (lane % RET_DK) < (RET_DK // 2)


def _meta_kernel(meta_ref, g_ref, b_ref, win_ref, cos_ref, sin_ref, zeta_ref, bd_ref, s0_ref, tail_ref):
    h = _layer_norm(meta_ref[...], g_ref[...], b_ref[...], LN_EPS)
    proj = jnp.dot(h.astype(BF16), win_ref[...], preferred_element_type=F32)
    k = _rotary(proj[:, K0:V0], cos_ref[...], sin_ref[...], _first_half_mask(N_META)) * (RET_DK ** -0.5)
    kz = (k * zeta_ref[...]).astype(BF16)
    v = proj[:, V0:G0].astype(BF16)
    kv = lax.dot_general(kz, v, (((0,), (0,)), ((), ())), preferred_element_type=F32)
    s0_ref[...] = kv * bd_ref[...]
    tail_ref[...] = proj[:, U0:]


def _mixer_kernel(x_ref, lng_ref, lnb_ref, win_ref, wout_ref, poolw_ref, poolb_ref, pools_ref, ln1g_ref, ln1b_ref,
                  cos_ref, sin_ref, dmask_ref, xi_ref, zeta_ref, decay_ref, bd_ref, s0_ref, tail0_ref,
                  wr_ref, br_ref,
                  h1_ref, pe_ref, rank_ref, pw_ref, cnt_ref,
                  state_ref, tail_ref, uext_ref, proj_ref, h0_ref, h0b_ref, mixin_ref, tri_ref, carry_ref):
    rows = x_ref.shape[0]
    n_chunks = rows // CHUNK
    first_step = jnp.logical_and(pl.program_id(0) == 0, pl.program_id(1) == 0)

    @pl.when(first_step)
    def _():
        r = lax.broadcasted_iota(jnp.int32, (rows, rows), 0)
        c = lax.broadcasted_iota(jnp.int32, (rows, rows), 1)
        tri_ref[...] = jnp.where(r < c, 1.0, 0.0).astype(BF16)
        carry_ref[...] = jnp.zeros_like(carry_ref)

    @pl.when(pl.program_id(1) == 0)
    def _():
        state_ref[...] = s0_ref[...]
        tail_ref[...] = tail0_ref[...]

    def ln_body(c, _):
        sl = pl.ds(pl.multiple_of(c * CHUNK, CHUNK), CHUNK)
        h0 = _layer_norm(x_ref[sl, :], lng_ref[...], lnb_ref[...], LN_EPS)
        h0_ref[sl, :] = h0
        h0b_ref[sl, :] = h0.astype(BF16)
        return 0
    lax.fori_loop(0, n_chunks, ln_body, 0)

    proj_ref[...] = jnp.dot(h0b_ref[...], win_ref[...], preferred_element_type=F32)

    first_half = _first_half_mask(CHUNK)
    head_of_lane = lax.broadcasted_iota(jnp.int32, (CHUNK, RET_QK_WIDTH), 1) // RET_DK

    def ret_body(c, _):
        sl = pl.ds(pl.multiple_of(c * CHUNK, CHUNK), CHUNK)
        cos = cos_ref[sl, :]
        sin = sin_ref[sl, :]
        q = _rotary(proj_ref[sl, Q0:K0], cos, sin, first_half)
        k = _rotary(proj_ref[sl, K0:V0], cos, sin, first_half) * (RET_DK ** -0.5)
        qb = q.astype(BF16)
        kb = k.astype(BF16)
        vb = proj_ref[sl, V0:G0].astype(BF16)
        q_heads = jnp.concatenate(
            [jnp.where(head_of_lane == h, qb, jnp.zeros_like(qb)) for h in range(RET_HEADS)], axis=0)
        scores = lax.dot_general(q_heads, kb, (((1,), (1,)), ((), ())), preferred_element_type=F32)
        p = (scores * dmask_ref[...]).astype(BF16)
        inner = jnp.concatenate(
            [jnp.dot(p[h * CHUNK:(h + 1) * CHUNK, :], vb[:, h * RET_DV:(h + 1) * RET_DV],
                     preferred_element_type=F32) for h in range(RET_HEADS)], axis=1)
        state = state_ref[...]
        cross = jnp.dot((q * xi_ref[...]).astype(BF16), state.astype(BF16), preferred_element_type=F32)
        kz = (k * zeta_ref[...]).astype(BF16)
        kv = lax.dot_general(kz, vb, (((0,), (0,)), ((), ())), preferred_element_type=F32)
        state_ref[...] = state * decay_ref[...] + kv * bd_ref[...]
        ret = inner + cross
        gate = proj_ref[sl, G0:U0]
        outs = []
        for h in range(RET_HEADS):
            o = ret[:, h * RET_DV:(h + 1) * RET_DV]
            mu = jnp.mean(o, axis=-1, keepdims=True)
            oc = o - mu
            var = jnp.mean(oc * oc, axis=-1, keepdims=True)
            outs.append(oc * lax.rsqrt(var + GN_EPS))
        gn = jnp.concatenate(outs, axis=1)
        mixin_ref[sl, 0:RET_WIDTH] = (gate * jax.nn.sigmoid(gate) * gn).astype(BF16)
        return 0
    lax.fori_loop(0, n_chunks, ret_body, 0)

    uext_ref[0:N_META, :] = tail_ref[...]
    uext_ref[N_META:, :] = proj_ref[:, U0:]
    tail_ref[...] = uext_ref[rows:, :]
    for g, w in enumerate(POOL_WINDOWS):
        lanes = slice(g * POOL_CH, (g + 1) * POOL_CH)
        e = uext_ref[:, lanes]
        acc = e
        shift = 1
        while shift < w:
            acc = acc + pltpu.roll(acc, shift, 0)
            shift *= 2
        pooled = acc[N_META:, :] * (1.0 / w) - e[N_META:, :]
        mixed = jnp.dot(pooled.astype(BF16), poolw_ref[g], preferred_element_type=F32) + poolb_ref[:, lanes]
        mixin_ref[:, RET_WIDTH + g * POOL_CH:RET_WIDTH + (g + 1) * POOL_CH] = (mixed * pools_ref[:, lanes]).astype(BF16)

    proj_ref[:, 0:D_MODEL] = jnp.dot(mixin_ref[...], wout_ref[...], preferred_element_type=F32)

    def ln1_body(c, _):
        sl = pl.ds(pl.multiple_of(c * CHUNK, CHUNK), CHUNK)
        h1 = _layer_norm(ALPHA * h0_ref[sl, :] + proj_ref[sl, 0:D_MODEL], ln1g_ref[...], ln1b_ref[...], LN_EPS)
        h1_ref[sl, :] = h1
        h0b_ref[sl, :] = h1.astype(BF16)
        return 0
    lax.fori_loop(0, n_chunks, ln1_body, 0)

    logits = lax.dot_general(wr_ref[...], h0b_ref[...], (((1,), (1,)), ((), ())), preferred_element_type=F32)
    logits = logits + br_ref[...]
    gl = logits[0:N_GROUPS, :]
    gmax = jnp.max(gl, axis=0, keepdims=True)
    g_p = 1.0 / jnp.sum(jnp.exp(gl - gmax), axis=0, keepdims=True)
    grow = lax.broadcasted_iota(jnp.int32, gl.shape, 0)
    g_idx = jnp.min(jnp.where(gl == gmax, grow, N_GROUPS), axis=0, keepdims=True)
    sel = logits[8:8 + EXPERTS_PER_GROUP, :]
    for g in range(1, N_GROUPS):
        sel = jnp.where(g_idx == g, logits[8 + g * EXPERTS_PER_GROUP:8 + (g + 1) * EXPERTS_PER_GROUP, :], sel)
    erow = lax.broadcasted_iota(jnp.int32, sel.shape, 0)
    m1 = jnp.max(sel, axis=0, keepdims=True)
    i1 = jnp.min(jnp.where(sel == m1, erow, EXPERTS_PER_GROUP), axis=0, keepdims=True)
    sel2 = jnp.where(erow == i1, -jnp.inf, sel)
    m2 = jnp.max(sel2, axis=0, keepdims=True)
    i2 = jnp.min(jnp.where(sel2 == m2, erow, EXPERTS_PER_GROUP), axis=0, keepdims=True)
    e2 = jnp.exp(m2 - m1)
    w1 = 1.0 / (1.0 + e2)
    w2 = e2 / (1.0 + e2)
    pe0 = g_idx * EXPERTS_PER_GROUP + i1
    pe1 = g_idx * EXPERTS_PER_GROUP + i2
    pe_ref[...] = jnp.concatenate([pe0, pe1], axis=0)
    pw_ref[...] = jnp.concatenate([g_p * w1, g_p * w2], axis=0)
    xrow = lax.broadcasted_iota(jnp.int32, (N_EXPERTS, rows), 0)
    oh0 = xrow == pe0
    oh1 = xrow == pe1
    oh = jnp.where(jnp.logical_or(oh0, oh1), 1.0, 0.0)
    carry = carry_ref[...]
    prefix = jnp.dot(oh.astype(BF16), tri_ref[...], preferred_element_type=F32) + carry[:, 0:1]
    rank0 = jnp.sum(jnp.where(oh0, prefix, 0.0), axis=0, keepdims=True)
    rank1 = jnp.sum(jnp.where(oh1, prefix, 0.0), axis=0, keepdims=True)
    rank_ref[...] = jnp.concatenate([rank0, rank1], axis=0).astype(jnp.int32)
    carry = carry + jnp.sum(oh, axis=1, keepdims=True)
    carry_ref[...] = carry
    cnt_ref[...] = carry.astype(jnp.int32)


def _dispatch_kernel(nv_ref, dest_ref, h_ref, buf_ref, zero_ref, sem, zsem):
    rows = h_ref.shape[0]

    @pl.when(pl.program_id(0) == 0)
    def _():
        zero_ref[...] = jnp.zeros_like(zero_ref)

        def zcopy(blk):
            dst = buf_ref.at[pl.ds(pl.multiple_of(blk * EXPERT_ROWS, EXPERT_ROWS), EXPERT_ROWS), :]
            return pltpu.make_async_copy(zero_ref, dst, zsem)

        def zstart(blk, _):
            @pl.when(nv_ref[blk] < EXPERT_ROWS)
            def _():
                zcopy(blk).start()
            return 0
        lax.fori_loop(0, nv_ref.shape[0], zstart, 0)

        def zwait(blk, _):
            @pl.when(nv_ref[blk] < EXPERT_ROWS)
            def _():
                zcopy(blk).wait()
            return 0
        lax.fori_loop(0, nv_ref.shape[0], zwait, 0)

    def copy(t, slot):
        return pltpu.make_async_copy(h_ref.at[pl.ds(t, 1), :], buf_ref.at[pl.ds(dest_ref[slot, t], 1), :], sem)

    def start(t, _):
        copy(t, 0).start()
        copy(t, 1).start()
        return 0
    lax.fori_loop(0, rows, start, 0)

    def wait(t, _):
        copy(t, 0).wait()
        copy(t, 1).wait()
        return 0
    lax.fori_loop(0, rows, wait, 0)


def _expert_kernel(be_ref, nv_ref, bi_ref, x_ref, wg_ref, wu_ref, wd_ref, y_ref):
    i = pl.program_id(0)
    nv = nv_ref[i]

    @pl.when(nv == 0)
    def _():
        y_ref[...] = jnp.zeros_like(y_ref)

    @pl.when(nv > 0)
    def _():
        x = x_ref[...].astype(BF16)
        gate = jnp.dot(x, wg_ref[...], preferred_element_type=F32)
        up = jnp.dot(x, wu_ref[...], preferred_element_type=F32)
        act = (gate * jax.nn.sigmoid(gate) * up).astype(BF16)
        y_ref[...] = jnp.dot(act, wd_ref[...], preferred_element_type=F32)


def _combine_kernel(dest_ref, h_ref, pw_ref, g_ref, b_ref, y_ref, o_ref, ybuf_ref, sem):
    rows = h_ref.shape[0]

    def copy(t, slot):
        return pltpu.make_async_copy(y_ref.at[pl.ds(dest_ref[slot, t], 1), :], ybuf_ref.at[slot, pl.ds(t, 1), :], sem)

    def start(t, _):
        copy(t, 0).start()
        copy(t, 1).start()
        return 0
    lax.fori_loop(0, rows, start, 0)

    def wait(t, _):
        copy(t, 0).wait()
        copy(t, 1).wait()
        return 0
    lax.fori_loop(0, rows, wait, 0)

    pw = pw_ref[...]
    y = pw[:, 0:1] * ybuf_ref[0] + pw[:, 1:2] * ybuf_ref[1]
    o_ref[...] = _layer_norm(ALPHA * h_ref[...] + y, g_ref[...], b_ref[...], LN_EPS)


def _tables(seq):
    log_g = jnp.log1p(-jnp.power(2.0, -5.0 - jnp.arange(RET_HEADS, dtype=F32)))
    i = jnp.arange(CHUNK, dtype=F32)
    rel = i[:, None] - i[None, :]
    dmask = jnp.where(rel[None] >= 0, jnp.exp(jnp.maximum(rel, 0.0)[None] * log_g[:, None, None]), 0.0)
    dmask = dmask.reshape(RET_HEADS * CHUNK, CHUNK)
    lg_lane = jnp.repeat(log_g, RET_DK)
    xi = jnp.exp((i + 1)[:, None] * lg_lane[None, :])
    zeta = jnp.exp((CHUNK - 1 - i)[:, None] * lg_lane[None, :])
    decay = jnp.broadcast_to(jnp.exp(CHUNK * lg_lane)[:, None], (RET_QK_WIDTH, RET_WIDTH))
    bd = (jnp.arange(RET_QK_WIDTH)[:, None] // RET_DK == jnp.arange(RET_WIDTH)[None, :] // RET_DV).astype(F32)
    zeta_meta = zeta[CHUNK - N_META:, :]
    half = RET_DK // 2
    inv = ROPE_BASE ** (-jnp.arange(half, dtype=F32) / half)
    pos = jnp.arange(N_META + seq, dtype=F32)
    ang = pos[:, None] * inv[None, :]
    cos = jnp.tile(jnp.cos(ang), (1, 2 * RET_HEADS))
    sin = jnp.tile(jnp.concatenate([-jnp.sin(ang), jnp.sin(ang)], axis=1), (1, RET_HEADS))
    return dict(dmask=dmask, xi=xi, zeta=zeta, decay=decay, bd=bd, zeta_meta=zeta_meta,
                cos_meta=cos[:N_META], sin_meta=sin[:N_META], cos=cos[N_META:], sin=sin[N_META:])


def _full(shape):
    return pl.BlockSpec(shape, lambda *_: (0,) * len(shape))


def kernel(x, meta_tokens, ln_emb_g, ln_emb_b, w_in, pool_w, pool_b, pool_scale, w_out, ln1_g, ln1_b, router_group_w, router_group_b, router_expert_w, router_expert_b, expert_w_gate, expert_w_up, expert_w_down, ln2_g, ln2_b):
    batch, seq, d = x.shape
    assert d == D_MODEL and seq % MIX_ROWS == 0 and (batch * seq) % DISPATCH_ROWS == 0
    n_tok = batch * seq
    t_blocks = seq // MIX_ROWS
    tb = _tables(seq)

    row = lambda a: a.reshape(1, -1).astype(F32)
    win_b = w_in[0].astype(BF16)
    wout_b = w_out[0].astype(BF16)
    poolw_b = pool_w[0].astype(BF16)
    wr = jnp.zeros((ROUTER_ROWS, D_MODEL), F32)
    wr = wr.at[0:N_GROUPS].set(router_group_w[0].T).at[8:8 + N_EXPERTS].set(router_expert_w[0].T).astype(BF16)
    br = jnp.zeros((ROUTER_ROWS, 1), F32)
    br = br.at[0:N_GROUPS, 0].set(router_group_b[0]).at[8:8 + N_EXPERTS, 0].set(router_expert_b[0])

    s0, tail0 = pl.pallas_call(
        _meta_kernel,
        out_shape=(jax.ShapeDtypeStruct((RET_QK_WIDTH, RET_WIDTH), F32), jax.ShapeDtypeStruct((N_META, POOL_WIDTH), F32)),
        name="meta_prep",
    )(meta_tokens.astype(F32), row(ln_emb_g), row(ln_emb_b), win_b, tb["cos_meta"], tb["sin_meta"], tb["zeta_meta"], tb["bd"])

    tok_spec = pl.BlockSpec((None, MIX_ROWS, D_MODEL), lambda b, j: (b, j, 0))
    pair_spec = pl.BlockSpec((TOP_K, MIX_ROWS), lambda b, j: (0, b * t_blocks + j))
    rope_spec = pl.BlockSpec((MIX_ROWS, RET_QK_WIDTH), lambda b, j: (j, 0))
    h1, pair_e, rank, pair_w, counts = pl.pallas_call(
        _mixer_kernel,
        grid=(batch, t_blocks),
        in_specs=[tok_spec, _full((1, D_MODEL)), _full((1, D_MODEL)), _full((D_MODEL, IN_COLS)), _full((D_MODEL, D_MODEL)),
                  _full((POOL_GROUPS, POOL_CH, POOL_CH)), _full((1, POOL_WIDTH)), _full((1, POOL_WIDTH)),
                  _full((1, D_MODEL)), _full((1, D_MODEL)), rope_spec, rope_spec,
                  _full((RET_HEADS * CHUNK, CHUNK)), _full((CHUNK, RET_QK_WIDTH)), _full((CHUNK, RET_QK_WIDTH)),
                  _full((RET_QK_WIDTH, RET_WIDTH)), _full((RET_QK_WIDTH, RET_WIDTH)),
                  _full((RET_QK_WIDTH, RET_WIDTH)), _full((N_META, POOL_WIDTH)),
                  _full((ROUTER_ROWS, D_MODEL)), _full((ROUTER_ROWS, 1))],
        out_specs=[tok_spec, pair_spec, pair_spec, pair_spec, _full((N_EXPERTS, 128))],
        out_shape=[jax.ShapeDtypeStruct((batch, seq, D_MODEL), F32),
                   jax.ShapeDtypeStruct((TOP_K, n_tok), jnp.int32),
                   jax.ShapeDtypeStruct((TOP_K, n_tok), jnp.int32),
                   jax.ShapeDtypeStruct((TOP_K, n_tok), F32),
                   jax.ShapeDtypeStruct((N_EXPERTS, 128), jnp.int32)],
        scratch_shapes=[pltpu.VMEM((RET_QK_WIDTH, RET_WIDTH), F32),
                        pltpu.VMEM((N_META, POOL_WIDTH), F32),
                        pltpu.VMEM((MIX_ROWS + N_META, POOL_WIDTH), F32),
                        pltpu.VMEM((MIX_ROWS, IN_COLS), F32),
                        pltpu.VMEM((MIX_ROWS, D_MODEL), F32),
                        pltpu.VMEM((MIX_ROWS, D_MODEL), BF16),
                        pltpu.VMEM((MIX_ROWS, D_MODEL), BF16),
                        pltpu.VMEM((MIX_ROWS, MIX_ROWS), BF16),
                        pltpu.VMEM((N_EXPERTS, 128), F32)],
        compiler_params=pltpu.CompilerParams(dimension_semantics=("arbitrary", "arbitrary"),
                                             vmem_limit_bytes=VMEM_LIMIT),
        name="mixer",
    )(x, row(ln_emb_g), row(ln_emb_b), win_b, wout_b, poolw_b, row(pool_b[0]), row(pool_scale[0]),
      row(ln1_g[0]), row(ln1_b[0]), tb["cos"], tb["sin"], tb["dmask"], tb["xi"], tb["zeta"], tb["decay"], tb["bd"],
      s0, tail0, wr, br)
    h1 = h1.reshape(n_tok, D_MODEL)

    cnt = counts[:, 0]
    nblk = (cnt + EXPERT_ROWS - 1) // EXPERT_ROWS
    blk_end = jnp.cumsum(nblk)
    row_start = (blk_end - nblk) * EXPERT_ROWS
    n_blocks = (n_tok * TOP_K) // EXPERT_ROWS + N_EXPERTS
    bidx = jnp.arange(n_blocks, dtype=jnp.int32)
    used = bidx < blk_end[-1]
    last = jnp.maximum(blk_end[-1] - 1, 0).astype(jnp.int32)
    blk_src = jnp.where(used, bidx, last)
    blk_e = jnp.minimum(jnp.searchsorted(blk_end, blk_src, side="right"), N_EXPERTS - 1).astype(jnp.int32)
    blk_nv = jnp.where(used, jnp.clip(cnt[blk_e] - (blk_src - (blk_end - nblk)[blk_e]) * EXPERT_ROWS, 0, EXPERT_ROWS), 0)
    blk_nv = blk_nv.astype(jnp.int32)
    dest = (row_start[pair_e] + rank).astype(jnp.int32)

    buf = pl.pallas_call(
        _dispatch_kernel,
        grid_spec=pltpu.PrefetchScalarGridSpec(
            num_scalar_prefetch=1,
            grid=(n_tok // DISPATCH_ROWS,),
            in_specs=[pl.BlockSpec((TOP_K, DISPATCH_ROWS), lambda i, nv: (0, i), memory_space=pltpu.SMEM),
                      pl.BlockSpec((DISPATCH_ROWS, D_MODEL), lambda i, nv: (i, 0))],
            out_specs=pl.BlockSpec(memory_space=pl.ANY),
            scratch_shapes=[pltpu.VMEM((EXPERT_ROWS, D_MODEL), F32), pltpu.SemaphoreType.DMA, pltpu.SemaphoreType.DMA],
        ),
        out_shape=jax.ShapeDtypeStruct((n_blocks * EXPERT_ROWS, D_MODEL), F32),
        compiler_params=pltpu.CompilerParams(dimension_semantics=("arbitrary",)),
        name="dispatch",
    )(blk_nv, dest, h1)

    y_sorted = pl.pallas_call(
        _expert_kernel,
        grid_spec=pltpu.PrefetchScalarGridSpec(
            num_scalar_prefetch=3,
            grid=(n_blocks,),
            in_specs=[pl.BlockSpec((EXPERT_ROWS, D_MODEL), lambda i, be, nv, bi: (bi[i], 0)),
                      pl.BlockSpec((None, D_MODEL, D_EXPERT), lambda i, be, nv, bi: (be[i], 0, 0)),
                      pl.BlockSpec((None, D_MODEL, D_EXPERT), lambda i, be, nv, bi: (be[i], 0, 0)),
                      pl.BlockSpec((None, D_EXPERT, D_MODEL), lambda i, be, nv, bi: (be[i], 0, 0))],
            out_specs=pl.BlockSpec((EXPERT_ROWS, D_MODEL), lambda i, be, nv, bi: (i, 0)),
        ),
        out_shape=jax.ShapeDtypeStruct((n_blocks * EXPERT_ROWS, D_MODEL), F32),
        compiler_params=pltpu.CompilerParams(dimension_semantics=("arbitrary",), vmem_limit_bytes=VMEM_LIMIT),
        name="experts",
    )(blk_e, blk_nv, blk_src, buf, expert_w_gate[0].astype(BF16), expert_w_up[0].astype(BF16),
      expert_w_down[0].astype(BF16))

    smem_pair = pl.BlockSpec((TOP_K, DISPATCH_ROWS), lambda i: (0, i), memory_space=pltpu.SMEM)
    rows_spec = pl.BlockSpec((DISPATCH_ROWS, D_MODEL), lambda i: (i, 0))
    out = pl.pallas_call(
        _combine_kernel,
        grid=(n_tok // DISPATCH_ROWS,),
        in_specs=[smem_pair, rows_spec, pl.BlockSpec((DISPATCH_ROWS, TOP_K), lambda i: (i, 0)),
                  pl.BlockSpec((1, D_MODEL), lambda i: (0, 0)), pl.BlockSpec((1, D_MODEL), lambda i: (0, 0)),
                  pl.BlockSpec(memory_space=pl.ANY)],
        out_specs=rows_spec,
        out_shape=jax.ShapeDtypeStruct((n_tok, D_MODEL), F32),
        scratch_shapes=[pltpu.VMEM((TOP_K, DISPATCH_ROWS, D_MODEL), F32), pltpu.SemaphoreType.DMA],
        compiler_params=pltpu.CompilerParams(dimension_semantics=("arbitrary",)),
        name="combine",
    )(dest, h1, pair_w.T, row(ln2_g[0]), row(ln2_b[0]), y_sorted)
    return out.reshape(batch, seq, D_MODEL)
```

```python
import functools

import jax
import jax.numpy as jnp
from jax import lax
from jax.experimental import pallas as pl
from jax.experimental.pallas import tpu as pltpu

D_MODEL = 1024
DEPTH = 1
N_META = 16
RET_HEADS = 4
RET_WIDTH = D_MODEL // 2
RET_DV = RET_WIDTH // RET_HEADS
RET_DK = RET_DV // 2
RET_QK_WIDTH = RET_HEADS * RET_DK
CHUNK = 128
ROPE_BASE = 10000.0
POOL_WINDOWS = (2, 4, 8, 16)
POOL_GROUPS = len(POOL_WINDOWS)
POOL_WIDTH = D_MODEL // 2
POOL_CH = POOL_WIDTH // POOL_GROUPS
IN_COLS = 2 * RET_QK_WIDTH + 2 * RET_WIDTH + POOL_WIDTH
N_GROUPS = 4
EXPERTS_PER_GROUP = 8
N_EXPERTS = N_GROUPS * EXPERTS_PER_GROUP
D_EXPERT = D_MODEL // 2
TOP_K = 2
LN_EPS = 1e-5
GN_EPS = 1e-6
ALPHA = (2 * DEPTH) ** 0.25

Q0, K0, V0, G0, U0 = 0, RET_QK_WIDTH, 2 * RET_QK_WIDTH, 2 * RET_QK_WIDTH + RET_WIDTH, 2 * RET_QK_WIDTH + 2 * RET_WIDTH

MIX_ROWS = 512
EXPERT_ROWS = 512
DISPATCH_ROWS = 256
ROUTER_ROWS = 40
DMA_UNROLL = 8
DEST_LANES = 8192
VMEM_LIMIT = 56 * 1024 * 1024

F32 = jnp.float32
BF16 = jnp.bfloat16


def _layer_norm(x, g, b, eps):
    mu = jnp.mean(x, axis=-1, keepdims=True)
    xc = x - mu
    var = jnp.mean(xc * xc, axis=-1, keepdims=True)
    return xc * lax.rsqrt(var + eps) * g + b


def _rotary(z, cos, sin_signed, first_half):
    partner = jnp.where(first_half, pltpu.roll(z, RET_QK_WIDTH - RET_DK // 2, 1), pltpu.roll(z, RET_DK // 2, 1))
    return z * cos + partner * sin_signed


def _first_half_mask(rows):
    lane = lax.broadcasted_iota(jnp.int32, (rows, RET_QK_WIDTH), 1)
    return (lane % RET_DK) < (RET_DK // 2)


def _meta_kernel(meta_ref, g_ref, b_ref, win_ref, cos_ref, sin_ref, zeta_ref, bd_ref, s0_ref, tail_ref):
    h = _layer_norm(meta_ref[...], g_ref[...], b_ref[...], LN_EPS)
    proj = jnp.dot(h.astype(BF16), win_ref[...], preferred_element_type=F32)
    k = _rotary(proj[:, K0:V0], cos_ref[...], sin_ref[...], _first_half_mask(N_META)) * (RET_DK ** -0.5)
    kz = (k * zeta_ref[...]).astype(BF16)
    v = proj[:, V0:G0].astype(BF16)
    kv = lax.dot_general(kz, v, (((0,), (0,)), ((), ())), preferred_element_type=F32)
    s0_ref[...] = kv * bd_ref[...]
    tail_ref[...] = proj[:, U0:]


def _mixer_kernel(x_ref, lng_ref, lnb_ref, win_ref, wout_ref, poolw_ref, poolb_ref, pools_ref, ln1g_ref, ln1b_ref,
                  cos_ref, sin_ref, dmask_ref, xi_ref, zeta_ref, decay_ref, bd_ref, s0_ref, tail0_ref,
                  wr_ref, br_ref,
                  h1_ref, pe_ref, rank_ref, pw_ref, cnt_ref,
                  state_ref, tail_ref, uext_ref, proj_ref, h0_ref, h0b_ref, mixin_ref, tri_ref, carry_ref):
    rows = x_ref.shape[0]
    n_chunks = rows // CHUNK
    first_step = jnp.logical_and(pl.program_id(0) == 0, pl.program_id(1) == 0)

    @pl.when(first_step)
    def _():
        r = lax.broadcasted_iota(jnp.int32, (rows, rows), 0)
        c = lax.broadcasted_iota(jnp.int32, (rows, rows), 1)
        tri_ref[...] = jnp.where(r < c, 1.0, 0.0).astype(BF16)
        carry_ref[...] = jnp.zeros_like(carry_ref)

    @pl.when(pl.program_id(1) == 0)
    def _():
        state_ref[...] = s0_ref[...]
        tail_ref[...] = tail0_ref[...]

    def ln_body(c, _):
        sl = pl.ds(pl.multiple_of(c * CHUNK, CHUNK), CHUNK)
        h0 = _layer_norm(x_ref[sl, :], lng_ref[...], lnb_ref[...], LN_EPS)
        h0_ref[sl, :] = h0
        h0b_ref[sl, :] = h0.astype(BF16)
        return 0
    lax.fori_loop(0, n_chunks, ln_body, 0)

    proj_ref[...] = jnp.dot(h0b_ref[...], win_ref[...], preferred_element_type=F32)

    first_half = _first_half_mask(CHUNK)
    head_of_lane = lax.broadcasted_iota(jnp.int32, (CHUNK, RET_QK_WIDTH), 1) // RET_DK

    def ret_body(c, _):
        sl = pl.ds(pl.multiple_of(c * CHUNK, CHUNK), CHUNK)
        cos = cos_ref[sl, :]
        sin = sin_ref[sl, :]
        q = _rotary(proj_ref[sl, Q0:K0], cos, sin, first_half)
        k = _rotary(proj_ref[sl, K0:V0], cos, sin, first_half) * (RET_DK ** -0.5)
        qb = q.astype(BF16)
        kb = k.astype(BF16)
        vb = proj_ref[sl, V0:G0].astype(BF16)
        q_heads = jnp.concatenate(
            [jnp.where(head_of_lane == h, qb, jnp.zeros_like(qb)) for h in range(RET_HEADS)], axis=0)
        scores = lax.dot_general(q_heads, kb, (((1,), (1,)), ((), ())), preferred_element_type=F32)
        p = (scores * dmask_ref[...]).astype(BF16)
        inner = jnp.concatenate(
            [jnp.dot(p[h * CHUNK:(h + 1) * CHUNK, :], vb[:, h * RET_DV:(h + 1) * RET_DV],
                     preferred_element_type=F32) for h in range(RET_HEADS)], axis=1)
        state = state_ref[...]
        cross = jnp.dot((q * xi_ref[...]).astype(BF16), state.astype(BF16), preferred_element_type=F32)
        kz = (k * zeta_ref[...]).astype(BF16)
        kv = lax.dot_general(kz, vb, (((0,), (0,)), ((), ())), preferred_element_type=F32)
        state_ref[...] = state * decay_ref[...] + kv * bd_ref[...]
        ret = inner + cross
        gate = proj_ref[sl, G0:U0]
        outs = []
        for h in range(RET_HEADS):
            o = ret[:, h * RET_DV:(h + 1) * RET_DV]
            mu = jnp.mean(o, axis=-1, keepdims=True)
            oc = o - mu
            var = jnp.mean(oc * oc, axis=-1, keepdims=True)
            outs.append(oc * lax.rsqrt(var + GN_EPS))
        gn = jnp.concatenate(outs, axis=1)
        mixin_ref[sl, 0:RET_WIDTH] = (gate * jax.nn.sigmoid(gate) * gn).astype(BF16)
        return 0
    lax.fori_loop(0, n_chunks, ret_body, 0)

    uext_ref[0:N_META, :] = tail_ref[...]
    uext_ref[N_META:, :] = proj_ref[:, U0:]
    tail_ref[...] = uext_ref[rows:, :]
    for g, w in enumerate(POOL_WINDOWS):
        lanes = slice(g * POOL_CH, (g + 1) * POOL_CH)
        e = uext_ref[:, lanes]
        acc = e
        shift = 1
        while shift < w:
            acc = acc + pltpu.roll(acc, shift, 0)
            shift *= 2
        pooled = acc[N_META:, :] * (1.0 / w) - e[N_META:, :]
        mixed = jnp.dot(pooled.astype(BF16), poolw_ref[g], preferred_element_type=F32) + poolb_ref[:, lanes]
        mixin_ref[:, RET_WIDTH + g * POOL_CH:RET_WIDTH + (g + 1) * POOL_CH] = (mixed * pools_ref[:, lanes]).astype(BF16)

    proj_ref[:, 0:D_MODEL] = jnp.dot(mixin_ref[...], wout_ref[...], preferred_element_type=F32)

    def ln1_body(c, _):
        sl = pl.ds(pl.multiple_of(c * CHUNK, CHUNK), CHUNK)
        h1 = _layer_norm(ALPHA * h0_ref[sl, :] + proj_ref[sl, 0:D_MODEL], ln1g_ref[...], ln1b_ref[...], LN_EPS)
        h1_ref[sl, :] = h1
        h0b_ref[sl, :] = h1.astype(BF16)
        return 0
    lax.fori_loop(0, n_chunks, ln1_body, 0)

    logits = lax.dot_general(wr_ref[...], h0b_ref[...], (((1,), (1,)), ((), ())), preferred_element_type=F32)
    logits = logits + br_ref[...]
    gl = logits[0:N_GROUPS, :]
    gmax = jnp.max(gl, axis=0, keepdims=True)
    g_p = 1.0 / jnp.sum(jnp.exp(gl - gmax), axis=0, keepdims=True)
    grow = lax.broadcasted_iota(jnp.int32, gl.shape, 0)
    g_idx = jnp.min(jnp.where(gl == gmax, grow, N_GROUPS), axis=0, keepdims=True)
    sel = logits[8:8 + EXPERTS_PER_GROUP, :]
    for g in range(1, N_GROUPS):
        sel = jnp.where(g_idx == g, logits[8 + g * EXPERTS_PER_GROUP:8 + (g + 1) * EXPERTS_PER_GROUP, :], sel)
    erow = lax.broadcasted_iota(jnp.int32, sel.shape, 0)
    m1 = jnp.max(sel, axis=0, keepdims=True)
    i1 = jnp.min(jnp.where(sel == m1, erow, EXPERTS_PER_GROUP), axis=0, keepdims=True)
    sel2 = jnp.where(erow == i1, -jnp.inf, sel)
    m2 = jnp.max(sel2, axis=0, keepdims=True)
    i2 = jnp.min(jnp.where(sel2 == m2, erow, EXPERTS_PER_GROUP), axis=0, keepdims=True)
    e2 = jnp.exp(m2 - m1)
    w1 = 1.0 / (1.0 + e2)
    w2 = e2 / (1.0 + e2)
    pe0 = g_idx * EXPERTS_PER_GROUP + i1
    pe1 = g_idx * EXPERTS_PER_GROUP + i2
    pe_ref[...] = jnp.concatenate([pe0, pe1], axis=0)
    pw_ref[...] = jnp.concatenate([g_p * w1, g_p * w2], axis=0)
    xrow = lax.broadcasted_iota(jnp.int32, (N_EXPERTS, rows), 0)
    oh0 = xrow == pe0
    oh1 = xrow == pe1
    oh = jnp.where(jnp.logical_or(oh0, oh1), 1.0, 0.0)
    carry = carry_ref[...]
    prefix = jnp.dot(oh.astype(BF16), tri_ref[...], preferred_element_type=F32) + carry[:, 0:1]
    rank0 = jnp.sum(jnp.where(oh0, prefix, 0.0), axis=0, keepdims=True)
    rank1 = jnp.sum(jnp.where(oh1, prefix, 0.0), axis=0, keepdims=True)
    rank_ref[...] = jnp.concatenate([rank0, rank1], axis=0).astype(jnp.int32)
    carry = carry + jnp.sum(oh, axis=1, keepdims=True)
    carry_ref[...] = carry
    cnt_ref[...] = carry.astype(jnp.int32)


def _dest_kernel(start_ref, pe_ref, rank_ref, dest_ref):
    pe = pe_ref[...]
    base = jnp.zeros_like(pe)
    for e in range(N_EXPERTS):
        base = jnp.where(pe == e, start_ref[e], base)
    dest_ref[...] = base + rank_ref[...]


def _dispatch_kernel(nv_ref, dest_ref, h_ref, buf_ref, zero_ref, sem, zsem):
    rows = h_ref.shape[0]

    @pl.when(pl.program_id(0) == 0)
    def _():
        zero_ref[...] = jnp.zeros_like(zero_ref)

        def zcopy(blk):
            dst = buf_ref.at[pl.ds(pl.multiple_of(blk * EXPERT_ROWS, EXPERT_ROWS), EXPERT_ROWS), :]
            return pltpu.make_async_copy(zero_ref, dst, zsem)

        def zstart(blk, _):
            @pl.when(nv_ref[blk] < EXPERT_ROWS)
            def _():
                zcopy(blk).start()
            return 0
        lax.fori_loop(0, nv_ref.shape[0], zstart, 0)

        def zwait(blk, _):
            @pl.when(nv_ref[blk] < EXPERT_ROWS)
            def _():
                zcopy(blk).wait()
            return 0
        lax.fori_loop(0, nv_ref.shape[0], zwait, 0)

    def copy(t, slot):
        return pltpu.make_async_copy(h_ref.at[pl.ds(t, 1), :], buf_ref.at[pl.ds(dest_ref[slot, t], 1), :], sem)

    def start(t, _):
        copy(t, 0).start()
        copy(t, 1).start()
        return 0
    lax.fori_loop(0, rows, start, 0, unroll=DMA_UNROLL)

    for _ in range(TOP_K):
        pltpu.make_async_copy(h_ref, buf_ref.at[pl.ds(0, rows), :], sem).wait()


def _expert_kernel(be_ref, nv_ref, bi_ref, x_ref, wg_ref, wu_ref, wd_ref, y_ref):
    i = pl.program_id(0)
    nv = nv_ref[i]

    @pl.when(nv == 0)
    def _():
        y_ref[...] = jnp.zeros_like(y_ref)

    @pl.when(nv > 0)
    def _():
        x = x_ref[...].astype(BF16)
        gate = jnp.dot(x, wg_ref[...], preferred_element_type=F32)
        up = jnp.dot(x, wu_ref[...], preferred_element_type=F32)
        act = (gate * jax.nn.sigmoid(gate) * up).astype(BF16)
        y_ref[...] = jnp.dot(act, wd_ref[...], preferred_element_type=F32)


def _combine_kernel(dest_ref, h_ref, pw_ref, g_ref, b_ref, y_ref, o_ref, ybuf_ref, sem):
    rows = h_ref.shape[0]

    def copy(t, slot):
        return pltpu.make_async_copy(y_ref.at[pl.ds(dest_ref[slot, t], 1), :], ybuf_ref.at[slot, pl.ds(t, 1), :], sem)

    def start(t, _):
        copy(t, 0).start()
        copy(t, 1).start()
        return 0
    lax.fori_loop(0, rows, start, 0, unroll=DMA_UNROLL)

    for slot in range(TOP_K):
        pltpu.make_async_copy(y_ref.at[pl.ds(0, rows), :], ybuf_ref.at[slot], sem).wait()

    pw = pw_ref[...]
    y = pw[:, 0:1] * ybuf_ref[0] + pw[:, 1:2] * ybuf_ref[1]
    o_ref[...] = _layer_norm(ALPHA * h_ref[...] + y, g_ref[...], b_ref[...], LN_EPS)


def _tables(seq):
    log_g = jnp.log1p(-jnp.power(2.0, -5.0 - jnp.arange(RET_HEADS, dtype=F32)))
    i = jnp.arange(CHUNK, dtype=F32)
    rel = i[:, None] - i[None, :]
    dmask = jnp.where(rel[None] >= 0, jnp.exp(jnp.maximum(rel, 0.0)[None] * log_g[:, None, None]), 0.0)
    dmask = dmask.reshape(RET_HEADS * CHUNK, CHUNK)
    lg_lane = jnp.repeat(log_g, RET_DK)
    xi = jnp.exp((i + 1)[:, None] * lg_lane[None, :])
    zeta = jnp.exp((CHUNK - 1 - i)[:, None] * lg_lane[None, :])
    decay = jnp.broadcast_to(jnp.exp(CHUNK * lg_lane)[:, None], (RET_QK_WIDTH, RET_WIDTH))
    bd = (jnp.arange(RET_QK_WIDTH)[:, None] // RET_DK == jnp.arange(RET_WIDTH)[None, :] // RET_DV).astype(F32)
    zeta_meta = zeta[CHUNK - N_META:, :]
    half = RET_DK // 2
    inv = ROPE_BASE ** (-jnp.arange(half, dtype=F32) / half)
    pos = jnp.arange(N_META + seq, dtype=F32)
    ang = pos[:, None] * inv[None, :]
    cos = jnp.tile(jnp.cos(ang), (1, 2 * RET_HEADS))
    sin = jnp.tile(jnp.concatenate([-jnp.sin(ang), jnp.sin(ang)], axis=1), (1, RET_HEADS))
    return dict(dmask=dmask, xi=xi, zeta=zeta, decay=decay, bd=bd, zeta_meta=zeta_meta,
                cos_meta=cos[:N_META], sin_meta=sin[:N_META], cos=cos[N_META:], sin=sin[N_META:])


def _full(shape):
    return pl.BlockSpec(shape, lambda *_: (0,) * len(shape))


def kernel(x, meta_tokens, ln_emb_g, ln_emb_b, w_in, pool_w, pool_b, pool_scale, w_out, ln1_g, ln1_b, router_group_w, router_group_b, router_expert_w, router_expert_b, expert_w_gate, expert_w_up, expert_w_down, ln2_g, ln2_b):
    batch, seq, d = x.shape
    assert d == D_MODEL and seq % MIX_ROWS == 0 and (batch * seq) % DISPATCH_ROWS == 0
    n_tok = batch * seq
    t_blocks = seq // MIX_ROWS
    tb = _tables(seq)

    row = lambda a: a.reshape(1, -1).astype(F32)
    win_b = w_in[0].astype(BF16)
    wout_b = w_out[0].astype(BF16)
    poolw_b = pool_w[0].astype(BF16)
    wr = jnp.zeros((ROUTER_ROWS, D_MODEL), F32)
    wr = wr.at[0:N_GROUPS].set(router_group_w[0].T).at[8:8 + N_EXPERTS].set(router_expert_w[0].T).astype(BF16)
    br = jnp.zeros((ROUTER_ROWS, 1), F32)
    br = br.at[0:N_GROUPS, 0].set(router_group_b[0]).at[8:8 + N_EXPERTS, 0].set(router_expert_b[0])

    s0, tail0 = pl.pallas_call(
        _meta_kernel,
        out_shape=(jax.ShapeDtypeStruct((RET_QK_WIDTH, RET_WIDTH), F32), jax.ShapeDtypeStruct((N_META, POOL_WIDTH), F32)),
        name="meta_prep",
    )(meta_tokens.astype(F32), row(ln_emb_g), row(ln_emb_b), win_b, tb["cos_meta"], tb["sin_meta"], tb["zeta_meta"], tb["bd"])

    tok_spec = pl.BlockSpec((None, MIX_ROWS, D_MODEL), lambda b, j: (b, j, 0))
    pair_spec = pl.BlockSpec((TOP_K, MIX_ROWS), lambda b, j: (0, b * t_blocks + j))
    rope_spec = pl.BlockSpec((MIX_ROWS, RET_QK_WIDTH), lambda b, j: (j, 0))
    h1, pair_e, rank, pair_w, counts = pl.pallas_call(
        _mixer_kernel,
        grid=(batch, t_blocks),
        in_specs=[tok_spec, _full((1, D_MODEL)), _full((1, D_MODEL)), _full((D_MODEL, IN_COLS)), _full((D_MODEL, D_MODEL)),
                  _full((POOL_GROUPS, POOL_CH, POOL_CH)), _full((1, POOL_WIDTH)), _full((1, POOL_WIDTH)),
                  _full((1, D_MODEL)), _full((1, D_MODEL)), rope_spec, rope_spec,
                  _full((RET_HEADS * CHUNK, CHUNK)), _full((CHUNK, RET_QK_WIDTH)), _full((CHUNK, RET_QK_WIDTH)),
                  _full((RET_QK_WIDTH, RET_WIDTH)), _full((RET_QK_WIDTH, RET_WIDTH)),
                  _full((RET_QK_WIDTH, RET_WIDTH)), _full((N_META, POOL_WIDTH)),
                  _full((ROUTER_ROWS, D_MODEL)), _full((ROUTER_ROWS, 1))],
        out_specs=[tok_spec, pair_spec, pair_spec, pair_spec, _full((N_EXPERTS, 128))],
        out_shape=[jax.ShapeDtypeStruct((batch, seq, D_MODEL), F32),
                   jax.ShapeDtypeStruct((TOP_K, n_tok), jnp.int32),
                   jax.ShapeDtypeStruct((TOP_K, n_tok), jnp.int32),
                   jax.ShapeDtypeStruct((TOP_K, n_tok), F32),
                   jax.ShapeDtypeStruct((N_EXPERTS, 128), jnp.int32)],
        scratch_shapes=[pltpu.VMEM((RET_QK_WIDTH, RET_WIDTH), F32),
                        pltpu.VMEM((N_META, POOL_WIDTH), F32),
                        pltpu.VMEM((MIX_ROWS + N_META, POOL_WIDTH), F32),
                        pltpu.VMEM((MIX_ROWS, IN_COLS), F32),
                        pltpu.VMEM((MIX_ROWS, D_MODEL), F32),
                        pltpu.VMEM((MIX_ROWS, D_MODEL), BF16),
                        pltpu.VMEM((MIX_ROWS, D_MODEL), BF16),
                        pltpu.VMEM((MIX_ROWS, MIX_ROWS), BF16),
                        pltpu.VMEM((N_EXPERTS, 128), F32)],
        compiler_params=pltpu.CompilerParams(dimension_semantics=("arbitrary", "arbitrary"),
                                             vmem_limit_bytes=VMEM_LIMIT),
        name="mixer",
    )(x, row(ln_emb_g), row(ln_emb_b), win_b, wout_b, poolw_b, row(pool_b[0]), row(pool_scale[0]),
      row(ln1_g[0]), row(ln1_b[0]), tb["cos"], tb["sin"], tb["dmask"], tb["xi"], tb["zeta"], tb["decay"], tb["bd"],
      s0, tail0, wr, br)
    h1 = h1.reshape(n_tok, D_MODEL)

    cnt = counts[:, 0]
    nblk = (cnt + EXPERT_ROWS - 1) // EXPERT_ROWS
    blk_end = jnp.cumsum(nblk)
    row_start = (blk_end - nblk) * EXPERT_ROWS
    n_blocks = (n_tok * TOP_K) // EXPERT_ROWS + N_EXPERTS
    bidx = jnp.arange(n_blocks, dtype=jnp.int32)
    used = bidx < blk_end[-1]
    last = jnp.maximum(blk_end[-1] - 1, 0).astype(jnp.int32)
    blk_src = jnp.where(used, bidx, last)
    in_expert = blk_src[:, None] >= blk_end[None, :]
    blk_e = jnp.minimum(jnp.sum(in_expert, axis=1), N_EXPERTS - 1).astype(jnp.int32)
    rows_in_expert = blk_src * EXPERT_ROWS - jnp.sum(jnp.where(in_expert, (nblk * EXPERT_ROWS)[None, :], 0), axis=1)
    cnt_e = jnp.sum(jnp.where(jnp.arange(N_EXPERTS)[None, :] == blk_e[:, None], cnt[None, :], 0), axis=1)
    blk_nv = jnp.where(used, jnp.clip(cnt_e - rows_in_expert, 0, EXPERT_ROWS), 0).astype(jnp.int32)

    dest_lanes = min(DEST_LANES, n_tok)
    assert n_tok % dest_lanes == 0
    dest = pl.pallas_call(
        _dest_kernel,
        grid_spec=pltpu.PrefetchScalarGridSpec(
            num_scalar_prefetch=1,
            grid=(n_tok // dest_lanes,),
            in_specs=[pl.BlockSpec((TOP_K, dest_lanes), lambda i, s: (0, i)),
                      pl.BlockSpec((TOP_K, dest_lanes), lambda i, s: (0, i))],
            out_specs=pl.BlockSpec((TOP_K, dest_lanes), lambda i, s: (0, i)),
        ),
        out_shape=jax.ShapeDtypeStruct((TOP_K, n_tok), jnp.int32),
        name="dest_index",
    )(row_start.astype(jnp.int32), pair_e, rank)

    buf = pl.pallas_call(
        _dispatch_kernel,
        grid_spec=pltpu.PrefetchScalarGridSpec(
            num_scalar_prefetch=1,
            grid=(n_tok // DISPATCH_ROWS,),
            in_specs=[pl.BlockSpec((TOP_K, DISPATCH_ROWS), lambda i, nv: (0, i), memory_space=pltpu.SMEM),
                      pl.BlockSpec((DISPATCH_ROWS, D_MODEL), lambda i, nv: (i, 0))],
            out_specs=pl.BlockSpec(memory_space=pl.ANY),
            scratch_shapes=[pltpu.VMEM((EXPERT_ROWS, D_MODEL), F32), pltpu.SemaphoreType.DMA, pltpu.SemaphoreType.DMA],
        ),
        out_shape=jax.ShapeDtypeStruct((n_blocks * EXPERT_ROWS, D_MODEL), F32),
        compiler_params=pltpu.CompilerParams(dimension_semantics=("arbitrary",)),
        name="dispatch",
    )(blk_nv, dest, h1)

    y_sorted = pl.pallas_call(
        _expert_kernel,
        grid_spec=pltpu.PrefetchScalarGridSpec(
            num_scalar_prefetch=3,
            grid=(n_blocks,),
            in_specs=[pl.BlockSpec((EXPERT_ROWS, D_MODEL), lambda i, be, nv, bi: (bi[i], 0)),
                      pl.BlockSpec((None, D_MODEL, D_EXPERT), lambda i, be, nv, bi: (be[i], 0, 0)),
                      pl.BlockSpec((None, D_MODEL, D_EXPERT), lambda i, be, nv, bi: (be[i], 0, 0)),
                      pl.BlockSpec((None, D_EXPERT, D_MODEL), lambda i, be, nv, bi: (be[i], 0, 0))],
            out_specs=pl.BlockSpec((EXPERT_ROWS, D_MODEL), lambda i, be, nv, bi: (i, 0)),
        ),
        out_shape=jax.ShapeDtypeStruct((n_blocks * EXPERT_ROWS, D_MODEL), F32),
        compiler_params=pltpu.CompilerParams(dimension_semantics=("arbitrary",), vmem_limit_bytes=VMEM_LIMIT),
        name="experts",
    )(blk_e, blk_nv, blk_src, buf, expert_w_gate[0].astype(BF16), expert_w_up[0].astype(BF16),
      expert_w_down[0].astype(BF16))

    smem_pair = pl.BlockSpec((TOP_K, DISPATCH_ROWS), lambda i: (0, i), memory_space=pltpu.SMEM)
    rows_spec = pl.BlockSpec((DISPATCH_ROWS, D_MODEL), lambda i: (i, 0))
    out = pl.pallas_call(
        _combine_kernel,
        grid=(n_tok // DISPATCH_ROWS,),
        in_specs=[smem_pair, rows_spec, pl.BlockSpec((DISPATCH_ROWS, TOP_K), lambda i: (i, 0)),
                  pl.BlockSpec((1, D_MODEL), lambda i: (0, 0)), pl.BlockSpec((1, D_MODEL), lambda i: (0, 0)),
                  pl.BlockSpec(memory_space=pl.ANY)],
        out_specs=rows_spec,
        out_shape=jax.ShapeDtypeStruct((n_tok, D_MODEL), F32),
        scratch_shapes=[pltpu.VMEM((TOP_K, DISPATCH_ROWS, D_MODEL), F32), pltpu.SemaphoreType.DMA],
        compiler_params=pltpu.CompilerParams(dimension_semantics=("arbitrary",)),
        name="combine",
    )(dest, h1, pair_w.T, row(ln2_g[0]), row(ln2_b[0]), y_sorted)
    return out.reshape(batch, seq, D_MODEL)
```

```python
import functools

import jax
import jax.numpy as jnp
from jax import lax
from jax.experimental import pallas as pl
from jax.experimental.pallas import tpu as pltpu

D_MODEL = 1024
DEPTH = 1
N_META = 16
RET_HEADS = 4
RET_WIDTH = D_MODEL // 2
RET_DV = RET_WIDTH // RET_HEADS
RET_DK = RET_DV // 2
RET_QK_WIDTH = RET_HEADS * RET_DK
CHUNK = 128
ROPE_BASE = 10000.0
POOL_WINDOWS = (2, 4, 8, 16)
POOL_GROUPS = len(POOL_WINDOWS)
POOL_WIDTH = D_MODEL // 2
POOL_CH = POOL_WIDTH // POOL_GROUPS
IN_COLS = 2 * RET_QK_WIDTH + 2 * RET_WIDTH + POOL_WIDTH
N_GROUPS = 4
EXPERTS_PER_GROUP = 8
N_EXPERTS = N_GROUPS * EXPERTS_PER_GROUP
D_EXPERT = D_MODEL // 2
TOP_K = 2
LN_EPS = 1e-5
GN_EPS = 1e-6
ALPHA = (2 * DEPTH) ** 0.25

Q0, K0, V0, G0, U0 = 0, RET_QK_WIDTH, 2 * RET_QK_WIDTH, 2 * RET_QK_WIDTH + RET_WIDTH, 2 * RET_QK_WIDTH + 2 * RET_WIDTH

MIX_ROWS = 512
EXPERT_ROWS = 512
DISPATCH_ROWS = 256
ROUTER_ROWS = 40
DEST_LANES = 8192
INDEX_GROUP = 8
VMEM_LIMIT = 56 * 1024 * 1024
LANES = 128
TOKEN_SUBLANES = D_MODEL // LANES

F32 = jnp.float32
BF16 = jnp.bfloat16


def _layer_norm(x, g, b, eps):
    mu = jnp.mean(x, axis=-1, keepdims=True)
    xc = x - mu
    var = jnp.mean(xc * xc, axis=-1, keepdims=True)
    return xc * lax.rsqrt(var + eps) * g + b


def _rotary(z, cos, sin_signed, first_half):
    partner = jnp.where(first_half, pltpu.roll(z, RET_QK_WIDTH - RET_DK // 2, 1), pltpu.roll(z, RET_DK // 2, 1))
    return z * cos + partner * sin_signed


def _store_token_tiles(ref, tok0, x):
    n = x.shape[0]
    for s in range(TOKEN_SUBLANES):
        ref[pl.ds(tok0 * TOKEN_SUBLANES + s, n, stride=TOKEN_SUBLANES), :] = x[:, s * LANES:(s + 1) * LANES]


def _load_token_tiles(ref, tok0, n):
    return jnp.concatenate(
        [ref[pl.ds(tok0 * TOKEN_SUBLANES + s, n, stride=TOKEN_SUBLANES), :] for s in range(TOKEN_SUBLANES)], axis=1)


def _token_rows(ref, tok, n=1):
    return ref.at[pl.ds(pl.multiple_of(tok * TOKEN_SUBLANES, TOKEN_SUBLANES), n * TOKEN_SUBLANES), :]


def _first_half_mask(rows):
    lane = lax.broadcasted_iota(jnp.int32, (rows, RET_QK_WIDTH), 1)
    return (lane % RET_DK) < (RET_DK // 2)


def _meta_kernel(meta_ref, g_ref, b_ref, win_ref, cos_ref, sin_ref, zeta_ref, bd_ref, s0_ref, tail_ref):
    h = _layer_norm(meta_ref[...], g_ref[...], b_ref[...], LN_EPS)
    proj = jnp.dot(h.astype(BF16), win_ref[...], preferred_element_type=F32)
    k = _rotary(proj[:, K0:V0], cos_ref[...], sin_ref[...], _first_half_mask(N_META)) * (RET_DK ** -0.5)
    kz = (k * zeta_ref[...]).astype(BF16)
    v = proj[:, V0:G0].astype(BF16)
    kv = lax.dot_general(kz, v, (((0,), (0,)), ((), ())), preferred_element_type=F32)
    s0_ref[...] = kv * bd_ref[...]
    tail_ref[...] = proj[:, U0:]


def _mixer_kernel(x_ref, lng_ref, lnb_ref, win_ref, wout_ref, poolw_ref, poolb_ref, pools_ref, ln1g_ref, ln1b_ref,
                  cos_ref, sin_ref, dmask_ref, xi_ref, zeta_ref, decay_ref, bd_ref, s0_ref, tail0_ref,
                  wr_ref, br_ref,
                  h1_ref, pe_ref, rank_ref, pw_ref, cnt_ref,
                  state_ref, tail_ref, uext_ref, proj_ref, h0_ref, h0b_ref, mixin_ref, tri_ref, carry_ref):
    rows = x_ref.shape[0]
    n_chunks = rows // CHUNK
    first_step = jnp.logical_and(pl.program_id(0) == 0, pl.program_id(1) == 0)

    @pl.when(first_step)
    def _():
        r = lax.broadcasted_iota(jnp.int32, (rows, rows), 0)
        c = lax.broadcasted_iota(jnp.int32, (rows, rows), 1)
        tri_ref[...] = jnp.where(r < c, 1.0, 0.0).astype(BF16)
        carry_ref[...] = jnp.zeros_like(carry_ref)

    @pl.when(pl.program_id(1) == 0)
    def _():
        state_ref[...] = s0_ref[...]
        tail_ref[...] = tail0_ref[...]

    def ln_body(c, _):
        sl = pl.ds(pl.multiple_of(c * CHUNK, CHUNK), CHUNK)
        h0 = _layer_norm(x_ref[sl, :], lng_ref[...], lnb_ref[...], LN_EPS)
        h0_ref[sl, :] = h0
        h0b_ref[sl, :] = h0.astype(BF16)
        return 0
    lax.fori_loop(0, n_chunks, ln_body, 0)

    proj_ref[...] = jnp.dot(h0b_ref[...], win_ref[...], preferred_element_type=F32)

    first_half = _first_half_mask(CHUNK)
    head_of_lane = lax.broadcasted_iota(jnp.int32, (CHUNK, RET_QK_WIDTH), 1) // RET_DK

    def ret_body(c, _):
        sl = pl.ds(pl.multiple_of(c * CHUNK, CHUNK), CHUNK)
        cos = cos_ref[sl, :]
        sin = sin_ref[sl, :]
        q = _rotary(proj_ref[sl, Q0:K0], cos, sin, first_half)
        k = _rotary(proj_ref[sl, K0:V0], cos, sin, first_half) * (RET_DK ** -0.5)
        qb = q.astype(BF16)
        kb = k.astype(BF16)
        vb = proj_ref[sl, V0:G0].astype(BF16)
        q_heads = jnp.concatenate(
            [jnp.where(head_of_lane == h, qb, jnp.zeros_like(qb)) for h in range(RET_HEADS)], axis=0)
        scores = lax.dot_general(q_heads, kb, (((1,), (1,)), ((), ())), preferred_element_type=F32)
        p = (scores * dmask_ref[...]).astype(BF16)
        inner = jnp.concatenate(
            [jnp.dot(p[h * CHUNK:(h + 1) * CHUNK, :], vb[:, h * RET_DV:(h + 1) * RET_DV],
                     preferred_element_type=F32) for h in range(RET_HEADS)], axis=1)
        state = state_ref[...]
        cross = jnp.dot((q * xi_ref[...]).astype(BF16), state.astype(BF16), preferred_element_type=F32)
        kz = (k * zeta_ref[...]).astype(BF16)
        kv = lax.dot_general(kz, vb, (((0,), (0,)), ((), ())), preferred_element_type=F32)
        state_ref[...] = state * decay_ref[...] + kv * bd_ref[...]
        ret = inner + cross
        gate = proj_ref[sl, G0:U0]
        outs = []
        for h in range(RET_HEADS):
            o = ret[:, h * RET_DV:(h + 1) * RET_DV]
            mu = jnp.mean(o, axis=-1, keepdims=True)
            oc = o - mu
            var = jnp.mean(oc * oc, axis=-1, keepdims=True)
            outs.append(oc * lax.rsqrt(var + GN_EPS))
        gn = jnp.concatenate(outs, axis=1)
        mixin_ref[sl, 0:RET_WIDTH] = (gate * jax.nn.sigmoid(gate) * gn).astype(BF16)
        return 0
    lax.fori_loop(0, n_chunks, ret_body, 0)

    uext_ref[0:N_META, :] = tail_ref[...]
    uext_ref[N_META:, :] = proj_ref[:, U0:]
    tail_ref[...] = uext_ref[rows:, :]
    for g, w in enumerate(POOL_WINDOWS):
        lanes = slice(g * POOL_CH, (g + 1) * POOL_CH)
        e = uext_ref[:, lanes]
        acc = e
        shift = 1
        while shift < w:
            acc = acc + pltpu.roll(acc, shift, 0)
            shift *= 2
        pooled = acc[N_META:, :] * (1.0 / w) - e[N_META:, :]
        mixed = jnp.dot(pooled.astype(BF16), poolw_ref[g], preferred_element_type=F32) + poolb_ref[:, lanes]
        mixin_ref[:, RET_WIDTH + g * POOL_CH:RET_WIDTH + (g + 1) * POOL_CH] = (mixed * pools_ref[:, lanes]).astype(BF16)

    proj_ref[:, 0:D_MODEL] = jnp.dot(mixin_ref[...], wout_ref[...], preferred_element_type=F32)

    def ln1_body(c, _):
        sl = pl.ds(pl.multiple_of(c * CHUNK, CHUNK), CHUNK)
        h1 = _layer_norm(ALPHA * h0_ref[sl, :] + proj_ref[sl, 0:D_MODEL], ln1g_ref[...], ln1b_ref[...], LN_EPS)
        _store_token_tiles(h1_ref, c * CHUNK, h1)
        h0b_ref[sl, :] = h1.astype(BF16)
        return 0
    lax.fori_loop(0, n_chunks, ln1_body, 0)

    logits = lax.dot_general(wr_ref[...], h0b_ref[...], (((1,), (1,)), ((), ())), preferred_element_type=F32)
    logits = logits + br_ref[...]
    gl = logits[0:N_GROUPS, :]
    gmax = jnp.max(gl, axis=0, keepdims=True)
    g_p = 1.0 / jnp.sum(jnp.exp(gl - gmax), axis=0, keepdims=True)
    grow = lax.broadcasted_iota(jnp.int32, gl.shape, 0)
    g_idx = jnp.min(jnp.where(gl == gmax, grow, N_GROUPS), axis=0, keepdims=True)
    sel = logits[8:8 + EXPERTS_PER_GROUP, :]
    for g in range(1, N_GROUPS):
        sel = jnp.where(g_idx == g, logits[8 + g * EXPERTS_PER_GROUP:8 + (g + 1) * EXPERTS_PER_GROUP, :], sel)
    erow = lax.broadcasted_iota(jnp.int32, sel.shape, 0)
    m1 = jnp.max(sel, axis=0, keepdims=True)
    i1 = jnp.min(jnp.where(sel == m1, erow, EXPERTS_PER_GROUP), axis=0, keepdims=True)
    sel2 = jnp.where(erow == i1, -jnp.inf, sel)
    m2 = jnp.max(sel2, axis=0, keepdims=True)
    i2 = jnp.min(jnp.where(sel2 == m2, erow, EXPERTS_PER_GROUP), axis=0, keepdims=True)
    e2 = jnp.exp(m2 - m1)
    w1 = 1.0 / (1.0 + e2)
    w2 = e2 / (1.0 + e2)
    pe0 = g_idx * EXPERTS_PER_GROUP + i1
    pe1 = g_idx * EXPERTS_PER_GROUP + i2
    pe_ref[...] = jnp.concatenate([pe0, pe1], axis=0)
    pw_ref[...] = jnp.concatenate([g_p * w1, g_p * w2], axis=0)
    xrow = lax.broadcasted_iota(jnp.int32, (N_EXPERTS, rows), 0)
    oh0 = xrow == pe0
    oh1 = xrow == pe1
    oh = jnp.where(jnp.logical_or(oh0, oh1), 1.0, 0.0)
    carry = carry_ref[...]
    prefix = jnp.dot(oh.astype(BF16), tri_ref[...], preferred_element_type=F32) + carry[:, 0:1]
    rank0 = jnp.sum(jnp.where(oh0, prefix, 0.0), axis=0, keepdims=True)
    rank1 = jnp.sum(jnp.where(oh1, prefix, 0.0), axis=0, keepdims=True)
    rank_ref[...] = jnp.concatenate([rank0, rank1], axis=0).astype(jnp.int32)
    carry = carry + jnp.sum(oh, axis=1, keepdims=True)
    carry_ref[...] = carry
    cnt_ref[...] = carry.astype(jnp.int32)


def _dest_kernel(start_ref, pe_ref, rank_ref, dest_ref):
    pe = pe_ref[...]
    base = jnp.zeros_like(pe)
    for e in range(N_EXPERTS):
        base = jnp.where(pe == e, start_ref[e], base)
    dest_ref[...] = base + rank_ref[...]


def _dispatch_kernel(nv_ref, dest_ref, h_ref, buf_ref, zero_ref, sem, zsem):
    @pl.when(pl.program_id(0) == 0)
    def _():
        zero_ref[...] = jnp.zeros_like(zero_ref)

        def zcopy(blk):
            return pltpu.make_async_copy(zero_ref, _token_rows(buf_ref, blk * EXPERT_ROWS, EXPERT_ROWS), zsem)

        def zstart(blk, _):
            @pl.when(nv_ref[blk] < EXPERT_ROWS)
            def _():
                zcopy(blk).start()
            return 0
        lax.fori_loop(0, nv_ref.shape[0], zstart, 0)

        def zwait(blk, _):
            @pl.when(nv_ref[blk] < EXPERT_ROWS)
            def _():
                zcopy(blk).wait()
            return 0
        lax.fori_loop(0, nv_ref.shape[0], zwait, 0)

    step = pl.program_id(0)
    base = step * DISPATCH_ROWS
    for t0 in range(0, DISPATCH_ROWS, INDEX_GROUP):
        dst = [[dest_ref[slot, t0 + k] for slot in range(TOP_K)] for k in range(INDEX_GROUP)]
        for k in range(INDEX_GROUP):
            src = _token_rows(h_ref, base + t0 + k)
            for slot in range(TOP_K):
                pltpu.make_async_copy(src, _token_rows(buf_ref, dst[k][slot]), sem).start()

    def wait_one_step():
        for _ in range(TOP_K):
            pltpu.make_async_copy(_token_rows(h_ref, 0, DISPATCH_ROWS), _token_rows(buf_ref, 0, DISPATCH_ROWS), sem).wait()

    @pl.when(step > 0)
    def _():
        wait_one_step()

    @pl.when(step == pl.num_programs(0) - 1)
    def _():
        wait_one_step()


def _expert_kernel(be_ref, nv_ref, bi_ref, x_ref, wg_ref, wu_ref, wd_ref, y_ref, wgb_ref, wub_ref, wdb_ref):
    i = pl.program_id(0)
    nv = nv_ref[i]
    new_expert = jnp.logical_or(i == 0, be_ref[i] != be_ref[jnp.maximum(i - 1, 0)])

    @pl.when(new_expert)
    def _():
        wgb_ref[...] = wg_ref[...].astype(BF16)
        wub_ref[...] = wu_ref[...].astype(BF16)
        wdb_ref[...] = wd_ref[...].astype(BF16)

    @pl.when(nv == 0)
    def _():
        y_ref[...] = jnp.zeros_like(y_ref)

    @pl.when(nv > 0)
    def _():
        x = _load_token_tiles(x_ref, 0, EXPERT_ROWS).astype(BF16)
        gate = jnp.dot(x, wgb_ref[...], preferred_element_type=F32)
        up = jnp.dot(x, wub_ref[...], preferred_element_type=F32)
        act = (gate * jax.nn.sigmoid(gate) * up).astype(BF16)
        _store_token_tiles(y_ref, 0, jnp.dot(act, wdb_ref[...], preferred_element_type=F32))


def _combine_kernel(dcur_ref, dnext_ref, h_ref, pw_ref, g_ref, b_ref, y_ref, o_ref, ya_ref, yb_ref, sem_a, sem_b):
    half = DISPATCH_ROWS
    step = pl.program_id(0)

    def copy(src_tok, ybuf, t, slot, sem):
        return pltpu.make_async_copy(_token_rows(y_ref, src_tok), _token_rows(ybuf.at[slot], t), sem)

    def issue(dref, col0, ybuf, sem):
        for t0 in range(0, half, INDEX_GROUP):
            src = [[dref[slot, col0 + t0 + k] for slot in range(TOP_K)] for k in range(INDEX_GROUP)]
            for k in range(INDEX_GROUP):
                for slot in range(TOP_K):
                    copy(src[k][slot], ybuf, t0 + k, slot, sem).start()

    def wait(ybuf, sem):
        for slot in range(TOP_K):
            pltpu.make_async_copy(_token_rows(y_ref, 0, half), ybuf.at[slot], sem).wait()

    def finish(ybuf, r0):
        rows = pl.ds(r0, half)
        pw = pw_ref[rows, :]
        y = pw[:, 0:1] * _load_token_tiles(ybuf.at[0], 0, half) + pw[:, 1:2] * _load_token_tiles(ybuf.at[1], 0, half)
        h = _load_token_tiles(h_ref, r0, half)
        o_ref[rows, :] = _layer_norm(ALPHA * h + y, g_ref[...], b_ref[...], LN_EPS)

    @pl.when(step == 0)
    def _():
        def first(t, _):
            for slot in range(TOP_K):
                copy(dcur_ref[slot, t], ya_ref, t, slot, sem_a).start()
            return 0
        lax.fori_loop(0, half, first, 0)

    issue(dcur_ref, half, yb_ref, sem_b)
    wait(ya_ref, sem_a)
    finish(ya_ref, 0)

    @pl.when(step + 1 < pl.num_programs(0))
    def _():
        issue(dnext_ref, 0, ya_ref, sem_a)

    wait(yb_ref, sem_b)
    finish(yb_ref, half)


def _tables(seq):
    log_g = jnp.log1p(-jnp.power(2.0, -5.0 - jnp.arange(RET_HEADS, dtype=F32)))
    i = jnp.arange(CHUNK, dtype=F32)
    rel = i[:, None] - i[None, :]
    dmask = jnp.where(rel[None] >= 0, jnp.exp(jnp.maximum(rel, 0.0)[None] * log_g[:, None, None]), 0.0)
    dmask = dmask.reshape(RET_HEADS * CHUNK, CHUNK)
    lg_lane = jnp.repeat(log_g, RET_DK)
    xi = jnp.exp((i + 1)[:, None] * lg_lane[None, :])
    zeta = jnp.exp((CHUNK - 1 - i)[:, None] * lg_lane[None, :])
    decay = jnp.broadcast_to(jnp.exp(CHUNK * lg_lane)[:, None], (RET_QK_WIDTH, RET_WIDTH))
    bd = (jnp.arange(RET_QK_WIDTH)[:, None] // RET_DK == jnp.arange(RET_WIDTH)[None, :] // RET_DV).astype(F32)
    zeta_meta = zeta[CHUNK - N_META:, :]
    half = RET_DK // 2
    inv = ROPE_BASE ** (-jnp.arange(half, dtype=F32) / half)
    pos = jnp.arange(N_META + seq, dtype=F32)
    ang = pos[:, None] * inv[None, :]
    cos = jnp.tile(jnp.cos(ang), (1, 2 * RET_HEADS))
    sin = jnp.tile(jnp.concatenate([-jnp.sin(ang), jnp.sin(ang)], axis=1), (1, RET_HEADS))
    return dict(dmask=dmask, xi=xi, zeta=zeta, decay=decay, bd=bd, zeta_meta=zeta_meta,
                cos_meta=cos[:N_META], sin_meta=sin[:N_META], cos=cos[N_META:], sin=sin[N_META:])


def _full(shape):
    return pl.BlockSpec(shape, lambda *_: (0,) * len(shape))


def kernel(x, meta_tokens, ln_emb_g, ln_emb_b, w_in, pool_w, pool_b, pool_scale, w_out, ln1_g, ln1_b, router_group_w, router_group_b, router_expert_w, router_expert_b, expert_w_gate, expert_w_up, expert_w_down, ln2_g, ln2_b):
    batch, seq, d = x.shape
    assert d == D_MODEL and seq % MIX_ROWS == 0 and (batch * seq) % DISPATCH_ROWS == 0
    n_tok = batch * seq
    t_blocks = seq // MIX_ROWS
    tb = _tables(seq)

    row = lambda a: a.reshape(1, -1).astype(F32)
    win_b = w_in[0].astype(BF16)
    wout_b = w_out[0].astype(BF16)
    poolw_b = pool_w[0].astype(BF16)
    wr = jnp.zeros((ROUTER_ROWS, D_MODEL), F32)
    wr = wr.at[0:N_GROUPS].set(router_group_w[0].T).at[8:8 + N_EXPERTS].set(router_expert_w[0].T).astype(BF16)
    br = jnp.zeros((ROUTER_ROWS, 1), F32)
    br = br.at[0:N_GROUPS, 0].set(router_group_b[0]).at[8:8 + N_EXPERTS, 0].set(router_expert_b[0])

    s0, tail0 = pl.pallas_call(
        _meta_kernel,
        out_shape=(jax.ShapeDtypeStruct((RET_QK_WIDTH, RET_WIDTH), F32), jax.ShapeDtypeStruct((N_META, POOL_WIDTH), F32)),
        name="meta_prep",
    )(meta_tokens.astype(F32), row(ln_emb_g), row(ln_emb_b), win_b, tb["cos_meta"], tb["sin_meta"], tb["zeta_meta"], tb["bd"])

    tok_spec = pl.BlockSpec((None, MIX_ROWS, D_MODEL), lambda b, j: (b, j, 0))
    pair_spec = pl.BlockSpec((TOP_K, MIX_ROWS), lambda b, j: (0, b * t_blocks + j))
    rope_spec = pl.BlockSpec((MIX_ROWS, RET_QK_WIDTH), lambda b, j: (j, 0))
    h1, pair_e, rank, pair_w, counts = pl.pallas_call(
        _mixer_kernel,
        grid=(batch, t_blocks),
        in_specs=[tok_spec, _full((1, D_MODEL)), _full((1, D_MODEL)), _full((D_MODEL, IN_COLS)), _full((D_MODEL, D_MODEL)),
                  _full((POOL_GROUPS, POOL_CH, POOL_CH)), _full((1, POOL_WIDTH)), _full((1, POOL_WIDTH)),
                  _full((1, D_MODEL)), _full((1, D_MODEL)), rope_spec, rope_spec,
                  _full((RET_HEADS * CHUNK, CHUNK)), _full((CHUNK, RET_QK_WIDTH)), _full((CHUNK, RET_QK_WIDTH)),
                  _full((RET_QK_WIDTH, RET_WIDTH)), _full((RET_QK_WIDTH, RET_WIDTH)),
                  _full((RET_QK_WIDTH, RET_WIDTH)), _full((N_META, POOL_WIDTH)),
                  _full((ROUTER_ROWS, D_MODEL)), _full((ROUTER_ROWS, 1))],
        out_specs=[pl.BlockSpec((MIX_ROWS * TOKEN_SUBLANES, LANES), lambda b, j: (b * t_blocks + j, 0)),
                   pair_spec, pair_spec, pair_spec, _full((N_EXPERTS, 128))],
        out_shape=[jax.ShapeDtypeStruct((n_tok * TOKEN_SUBLANES, LANES), F32),
                   jax.ShapeDtypeStruct((TOP_K, n_tok), jnp.int32),
                   jax.ShapeDtypeStruct((TOP_K, n_tok), jnp.int32),
                   jax.ShapeDtypeStruct((TOP_K, n_tok), F32),
                   jax.ShapeDtypeStruct((N_EXPERTS, 128), jnp.int32)],
        scratch_shapes=[pltpu.VMEM((RET_QK_WIDTH, RET_WIDTH), F32),
                        pltpu.VMEM((N_META, POOL_WIDTH), F32),
                        pltpu.VMEM((MIX_ROWS + N_META, POOL_WIDTH), F32),
                        pltpu.VMEM((MIX_ROWS, IN_COLS), F32),
                        pltpu.VMEM((MIX_ROWS, D_MODEL), F32),
                        pltpu.VMEM((MIX_ROWS, D_MODEL), BF16),
                        pltpu.VMEM((MIX_ROWS, D_MODEL), BF16),
                        pltpu.VMEM((MIX_ROWS, MIX_ROWS), BF16),
                        pltpu.VMEM((N_EXPERTS, 128), F32)],
        compiler_params=pltpu.CompilerParams(dimension_semantics=("arbitrary", "arbitrary"),
                                             vmem_limit_bytes=VMEM_LIMIT),
        name="mixer",
    )(x, row(ln_emb_g), row(ln_emb_b), win_b, wout_b, poolw_b, row(pool_b[0]), row(pool_scale[0]),
      row(ln1_g[0]), row(ln1_b[0]), tb["cos"], tb["sin"], tb["dmask"], tb["xi"], tb["zeta"], tb["decay"], tb["bd"],
      s0, tail0, wr, br)

    cnt = counts[:, 0]
    nblk = (cnt + EXPERT_ROWS - 1) // EXPERT_ROWS
    blk_end = jnp.cumsum(nblk)
    row_start = (blk_end - nblk) * EXPERT_ROWS
    n_blocks = (n_tok * TOP_K) // EXPERT_ROWS + N_EXPERTS
    bidx = jnp.arange(n_blocks, dtype=jnp.int32)
    used = bidx < blk_end[-1]
    last = jnp.maximum(blk_end[-1] - 1, 0).astype(jnp.int32)
    blk_src = jnp.where(used, bidx, last)
    in_expert = blk_src[:, None] >= blk_end[None, :]
    blk_e = jnp.minimum(jnp.sum(in_expert, axis=1), N_EXPERTS - 1).astype(jnp.int32)
    rows_in_expert = blk_src * EXPERT_ROWS - jnp.sum(jnp.where(in_expert, (nblk * EXPERT_ROWS)[None, :], 0), axis=1)
    cnt_e = jnp.sum(jnp.where(jnp.arange(N_EXPERTS)[None, :] == blk_e[:, None], cnt[None, :], 0), axis=1)
    blk_nv = jnp.where(used, jnp.clip(cnt_e - rows_in_expert, 0, EXPERT_ROWS), 0).astype(jnp.int32)

    dest_lanes = min(DEST_LANES, n_tok)
    assert n_tok % dest_lanes == 0
    dest = pl.pallas_call(
        _dest_kernel,
        grid_spec=pltpu.PrefetchScalarGridSpec(
            num_scalar_prefetch=1,
            grid=(n_tok // dest_lanes,),
            in_specs=[pl.BlockSpec((TOP_K, dest_lanes), lambda i, s: (0, i)),
                      pl.BlockSpec((TOP_K, dest_lanes), lambda i, s: (0, i))],
            out_specs=pl.BlockSpec((TOP_K, dest_lanes), lambda i, s: (0, i)),
        ),
        out_shape=jax.ShapeDtypeStruct((TOP_K, n_tok), jnp.int32),
        name="dest_index",
    )(row_start.astype(jnp.int32), pair_e, rank)

    tile_block = (EXPERT_ROWS * TOKEN_SUBLANES, LANES)
    sorted_shape = (n_blocks * EXPERT_ROWS * TOKEN_SUBLANES, LANES)
    buf = pl.pallas_call(
        _dispatch_kernel,
        grid_spec=pltpu.PrefetchScalarGridSpec(
            num_scalar_prefetch=1,
            grid=(n_tok // DISPATCH_ROWS,),
            in_specs=[pl.BlockSpec((TOP_K, DISPATCH_ROWS), lambda i, nv: (0, i), memory_space=pltpu.SMEM),
                      pl.BlockSpec(memory_space=pl.ANY)],
            out_specs=pl.BlockSpec(memory_space=pl.ANY),
            scratch_shapes=[pltpu.VMEM(tile_block, F32), pltpu.SemaphoreType.DMA, pltpu.SemaphoreType.DMA],
        ),
        out_shape=jax.ShapeDtypeStruct(sorted_shape, F32),
        compiler_params=pltpu.CompilerParams(dimension_semantics=("arbitrary",)),
        name="dispatch",
    )(blk_nv, dest, h1)

    y_sorted = pl.pallas_call(
        _expert_kernel,
        grid_spec=pltpu.PrefetchScalarGridSpec(
            num_scalar_prefetch=3,
            grid=(n_blocks,),
            in_specs=[pl.BlockSpec(tile_block, lambda i, be, nv, bi: (bi[i], 0)),
                      pl.BlockSpec((None, D_MODEL, D_EXPERT), lambda i, be, nv, bi: (be[i], 0, 0)),
                      pl.BlockSpec((None, D_MODEL, D_EXPERT), lambda i, be, nv, bi: (be[i], 0, 0)),
                      pl.BlockSpec((None, D_EXPERT, D_MODEL), lambda i, be, nv, bi: (be[i], 0, 0))],
            out_specs=pl.BlockSpec(tile_block, lambda i, be, nv, bi: (i, 0)),
            scratch_shapes=[pltpu.VMEM((D_MODEL, D_EXPERT), BF16), pltpu.VMEM((D_MODEL, D_EXPERT), BF16),
                            pltpu.VMEM((D_EXPERT, D_MODEL), BF16)],
        ),
        out_shape=jax.ShapeDtypeStruct(sorted_shape, F32),
        compiler_params=pltpu.CompilerParams(dimension_semantics=("arbitrary",), vmem_limit_bytes=VMEM_LIMIT),
        name="experts",
    )(blk_e, blk_nv, blk_src, buf, expert_w_gate[0], expert_w_up[0], expert_w_down[0])

    comb_rows = 2 * DISPATCH_ROWS
    comb_steps = n_tok // comb_rows
    assert n_tok % comb_rows == 0
    rows_spec = pl.BlockSpec((comb_rows, D_MODEL), lambda i: (i, 0))
    out = pl.pallas_call(
        _combine_kernel,
        grid=(comb_steps,),
        in_specs=[pl.BlockSpec((TOP_K, comb_rows), lambda i: (0, i), memory_space=pltpu.SMEM),
                  pl.BlockSpec((TOP_K, comb_rows), lambda i: (0, jnp.minimum(i + 1, comb_steps - 1)),
                               memory_space=pltpu.SMEM),
                  pl.BlockSpec((comb_rows * TOKEN_SUBLANES, LANES), lambda i: (i, 0)),
                  pl.BlockSpec((comb_rows, TOP_K), lambda i: (i, 0)),
                  pl.BlockSpec((1, D_MODEL), lambda i: (0, 0)), pl.BlockSpec((1, D_MODEL), lambda i: (0, 0)),
                  pl.BlockSpec(memory_space=pl.ANY)],
        out_specs=rows_spec,
        out_shape=jax.ShapeDtypeStruct((n_tok, D_MODEL), F32),
        scratch_shapes=[pltpu.VMEM((TOP_K, DISPATCH_ROWS * TOKEN_SUBLANES, LANES), F32),
                        pltpu.VMEM((TOP_K, DISPATCH_ROWS * TOKEN_SUBLANES, LANES), F32),
                        pltpu.SemaphoreType.DMA, pltpu.SemaphoreType.DMA],
        compiler_params=pltpu.CompilerParams(dimension_semantics=("arbitrary",), vmem_limit_bytes=VMEM_LIMIT),
        name="combine",
    )(dest, dest, h1, pair_w.T, row(ln2_g[0]), row(ln2_b[0]), y_sorted)
    return out.reshape(batch, seq, D_MODEL)
```

```python
import functools

import jax
import jax.numpy as jnp
from jax import lax
from jax.experimental import pallas as pl
from jax.experimental.pallas import tpu as pltpu

D_MODEL = 1024
DEPTH = 1
N_META = 16
RET_HEADS = 4
RET_WIDTH = D_MODEL // 2
RET_DV = RET_WIDTH // RET_HEADS
RET_DK = RET_DV // 2
RET_QK_WIDTH = RET_HEADS * RET_DK
CHUNK = 128
ROPE_BASE = 10000.0
POOL_WINDOWS = (2, 4, 8, 16)
POOL_GROUPS = len(POOL_WINDOWS)
POOL_WIDTH = D_MODEL // 2
POOL_CH = POOL_WIDTH // POOL_GROUPS
IN_COLS = 2 * RET_QK_WIDTH + 2 * RET_WIDTH + POOL_WIDTH
N_GROUPS = 4
EXPERTS_PER_GROUP = 8
N_EXPERTS = N_GROUPS * EXPERTS_PER_GROUP
D_EXPERT = D_MODEL // 2
TOP_K = 2
LN_EPS = 1e-5
GN_EPS = 1e-6
ALPHA = (2 * DEPTH) ** 0.25

Q0, K0, V0, G0, U0 = 0, RET_QK_WIDTH, 2 * RET_QK_WIDTH, 2 * RET_QK_WIDTH + RET_WIDTH, 2 * RET_QK_WIDTH + 2 * RET_WIDTH

MIX_ROWS = 512
EXPERT_ROWS = 512
DISPATCH_ROWS = 256
ROUTER_ROWS = 40
DEST_LANES = 8192
INDEX_GROUP = 8
VMEM_LIMIT = 56 * 1024 * 1024
LANES = 128
TOKEN_SUBLANES = D_MODEL // LANES

F32 = jnp.float32
BF16 = jnp.bfloat16


def _layer_norm(x, g, b, eps):
    mu = jnp.mean(x, axis=-1, keepdims=True)
    xc = x - mu
    var = jnp.mean(xc * xc, axis=-1, keepdims=True)
    return xc * lax.rsqrt(var + eps) * g + b


def _rotary(z, cos, sin_signed, first_half):
    partner = jnp.where(first_half, pltpu.roll(z, RET_QK_WIDTH - RET_DK // 2, 1), pltpu.roll(z, RET_DK // 2, 1))
    return z * cos + partner * sin_signed


def _store_token_tiles(ref, tok0, x):
    n = x.shape[0]
    for s in range(TOKEN_SUBLANES):
        ref[pl.ds(tok0 * TOKEN_SUBLANES + s, n, stride=TOKEN_SUBLANES), :] = x[:, s * LANES:(s + 1) * LANES]


def _load_token_tiles(ref, tok0, n):
    return jnp.concatenate(
        [ref[pl.ds(tok0 * TOKEN_SUBLANES + s, n, stride=TOKEN_SUBLANES), :] for s in range(TOKEN_SUBLANES)], axis=1)


def _token_rows(ref, tok, n=1):
    return ref.at[pl.ds(pl.multiple_of(tok * TOKEN_SUBLANES, TOKEN_SUBLANES), n * TOKEN_SUBLANES), :]


def _first_half_mask(rows):
    lane = lax.broadcasted_iota(jnp.int32, (rows, RET_QK_WIDTH), 1)
    return (lane % RET_DK) < (RET_DK // 2)


def _meta_kernel(meta_ref, g_ref, b_ref, win_ref, cos_ref, sin_ref, zeta_ref, bd_ref, s0_ref, tail_ref):
    h = _layer_norm(meta_ref[...], g_ref[...], b_ref[...], LN_EPS)
    proj = jnp.dot(h.astype(BF16), win_ref[...], preferred_element_type=F32)
    k = _rotary(proj[:, K0:V0], cos_ref[...], sin_ref[...], _first_half_mask(N_META)) * (RET_DK ** -0.5)
    kz = (k * zeta_ref[...]).astype(BF16)
    v = proj[:, V0:G0].astype(BF16)
    kv = lax.dot_general(kz, v, (((0,), (0,)), ((), ())), preferred_element_type=F32)
    s0_ref[...] = kv * bd_ref[...]
    tail_ref[...] = proj[:, U0:]


def _mixer_kernel(x_ref, lng_ref, lnb_ref, win_ref, wout_ref, poolw_ref, poolb_ref, pools_ref, ln1g_ref, ln1b_ref,
                  cos_ref, sin_ref, dmask_ref, xi_ref, zeta_ref, decay_ref, bd_ref, s0_ref, tail0_ref,
                  wr_ref, br_ref,
                  h1_ref, pe_ref, rank_ref, pw_ref, cnt_ref,
                  state_ref, tail_ref, uext_ref, proj_ref, h0_ref, h0b_ref, mixin_ref, tri_ref, carry_ref):
    rows = x_ref.shape[0]
    n_chunks = rows // CHUNK
    first_step = jnp.logical_and(pl.program_id(0) == 0, pl.program_id(1) == 0)

    @pl.when(first_step)
    def _():
        r = lax.broadcasted_iota(jnp.int32, (rows, rows), 0)
        c = lax.broadcasted_iota(jnp.int32, (rows, rows), 1)
        tri_ref[...] = jnp.where(r < c, 1.0, 0.0).astype(BF16)
        carry_ref[...] = jnp.zeros_like(carry_ref)

    @pl.when(pl.program_id(1) == 0)
    def _():
        state_ref[...] = s0_ref[...]
        tail_ref[...] = tail0_ref[...]

    def ln_body(c, _):
        sl = pl.ds(pl.multiple_of(c * CHUNK, CHUNK), CHUNK)
        h0 = _layer_norm(x_ref[sl, :], lng_ref[...], lnb_ref[...], LN_EPS)
        h0_ref[sl, :] = h0
        h0b_ref[sl, :] = h0.astype(BF16)
        return 0
    lax.fori_loop(0, n_chunks, ln_body, 0, unroll=True)

    proj_ref[...] = jnp.dot(h0b_ref[...], win_ref[...], preferred_element_type=F32)

    first_half = _first_half_mask(CHUNK)
    head_of_lane = lax.broadcasted_iota(jnp.int32, (CHUNK, RET_QK_WIDTH), 1) // RET_DK

    def ret_body(c, _):
        sl = pl.ds(pl.multiple_of(c * CHUNK, CHUNK), CHUNK)
        cos = cos_ref[sl, :]
        sin = sin_ref[sl, :]
        q = _rotary(proj_ref[sl, Q0:K0], cos, sin, first_half)
        k = _rotary(proj_ref[sl, K0:V0], cos, sin, first_half) * (RET_DK ** -0.5)
        qb = q.astype(BF16)
        kb = k.astype(BF16)
        vb = proj_ref[sl, V0:G0].astype(BF16)
        q_heads = jnp.concatenate(
            [jnp.where(head_of_lane == h, qb, jnp.zeros_like(qb)) for h in range(RET_HEADS)], axis=0)
        scores = lax.dot_general(q_heads, kb, (((1,), (1,)), ((), ())), preferred_element_type=F32)
        p = (scores * dmask_ref[...]).astype(BF16)
        inner = jnp.concatenate(
            [jnp.dot(p[h * CHUNK:(h + 1) * CHUNK, :], vb[:, h * RET_DV:(h + 1) * RET_DV],
                     preferred_element_type=F32) for h in range(RET_HEADS)], axis=1)
        state = state_ref[...]
        cross = jnp.dot((q * xi_ref[...]).astype(BF16), state.astype(BF16), preferred_element_type=F32)
        kz = (k * zeta_ref[...]).astype(BF16)
        kv = lax.dot_general(kz, vb, (((0,), (0,)), ((), ())), preferred_element_type=F32)
        state_ref[...] = state * decay_ref[...] + kv * bd_ref[...]
        ret = inner + cross
        gate = proj_ref[sl, G0:U0]
        outs = []
        for h in range(RET_HEADS):
            o = ret[:, h * RET_DV:(h + 1) * RET_DV]
            mu = jnp.mean(o, axis=-1, keepdims=True)
            oc = o - mu
            var = jnp.mean(oc * oc, axis=-1, keepdims=True)
            outs.append(oc * lax.rsqrt(var + GN_EPS))
        gn = jnp.concatenate(outs, axis=1)
        mixin_ref[sl, 0:RET_WIDTH] = (gate * jax.nn.sigmoid(gate) * gn).astype(BF16)
        return 0
    lax.fori_loop(0, n_chunks, ret_body, 0, unroll=True)

    uext_ref[0:N_META, :] = tail_ref[...]
    uext_ref[N_META:, :] = proj_ref[:, U0:]
    tail_ref[...] = uext_ref[rows:, :]
    for g, w in enumerate(POOL_WINDOWS):
        lanes = slice(g * POOL_CH, (g + 1) * POOL_CH)
        e = uext_ref[:, lanes]
        acc = e
        shift = 1
        while shift < w:
            acc = acc + pltpu.roll(acc, shift, 0)
            shift *= 2
        pooled = acc[N_META:, :] * (1.0 / w) - e[N_META:, :]
        mixed = jnp.dot(pooled.astype(BF16), poolw_ref[g], preferred_element_type=F32) + poolb_ref[:, lanes]
        mixin_ref[:, RET_WIDTH + g * POOL_CH:RET_WIDTH + (g + 1) * POOL_CH] = (mixed * pools_ref[:, lanes]).astype(BF16)

    proj_ref[:, 0:D_MODEL] = jnp.dot(mixin_ref[...], wout_ref[...], preferred_element_type=F32)

    def ln1_body(c, _):
        sl = pl.ds(pl.multiple_of(c * CHUNK, CHUNK), CHUNK)
        h1 = _layer_norm(ALPHA * h0_ref[sl, :] + proj_ref[sl, 0:D_MODEL], ln1g_ref[...], ln1b_ref[...], LN_EPS)
        _store_token_tiles(h1_ref, c * CHUNK, h1)
        h0b_ref[sl, :] = h1.astype(BF16)
        return 0
    lax.fori_loop(0, n_chunks, ln1_body, 0, unroll=True)

    logits = lax.dot_general(wr_ref[...], h0b_ref[...], (((1,), (1,)), ((), ())), preferred_element_type=F32)
    logits = logits + br_ref[...]
    gl = logits[0:N_GROUPS, :]
    gmax = jnp.max(gl, axis=0, keepdims=True)
    g_p = 1.0 / jnp.sum(jnp.exp(gl - gmax), axis=0, keepdims=True)
    grow = lax.broadcasted_iota(jnp.int32, gl.shape, 0)
    g_idx = jnp.min(jnp.where(gl == gmax, grow, N_GROUPS), axis=0, keepdims=True)
    sel = logits[8:8 + EXPERTS_PER_GROUP, :]
    for g in range(1, N_GROUPS):
        sel = jnp.where(g_idx == g, logits[8 + g * EXPERTS_PER_GROUP:8 + (g + 1) * EXPERTS_PER_GROUP, :], sel)
    erow = lax.broadcasted_iota(jnp.int32, sel.shape, 0)
    m1 = jnp.max(sel, axis=0, keepdims=True)
    i1 = jnp.min(jnp.where(sel == m1, erow, EXPERTS_PER_GROUP), axis=0, keepdims=True)
    sel2 = jnp.where(erow == i1, -jnp.inf, sel)
    m2 = jnp.max(sel2, axis=0, keepdims=True)
    i2 = jnp.min(jnp.where(sel2 == m2, erow, EXPERTS_PER_GROUP), axis=0, keepdims=True)
    e2 = jnp.exp(m2 - m1)
    w1 = 1.0 / (1.0 + e2)
    w2 = e2 / (1.0 + e2)
    pe0 = g_idx * EXPERTS_PER_GROUP + i1
    pe1 = g_idx * EXPERTS_PER_GROUP + i2
    pe_ref[...] = jnp.concatenate([pe0, pe1], axis=0)
    pw_ref[...] = jnp.concatenate([g_p * w1, g_p * w2], axis=0)
    xrow = lax.broadcasted_iota(jnp.int32, (N_EXPERTS, rows), 0)
    oh0 = xrow == pe0
    oh1 = xrow == pe1
    oh = jnp.where(jnp.logical_or(oh0, oh1), 1.0, 0.0)
    carry = carry_ref[...]
    prefix = jnp.dot(oh.astype(BF16), tri_ref[...], preferred_element_type=F32) + carry[:, 0:1]
    rank0 = jnp.sum(jnp.where(oh0, prefix, 0.0), axis=0, keepdims=True)
    rank1 = jnp.sum(jnp.where(oh1, prefix, 0.0), axis=0, keepdims=True)
    rank_ref[...] = jnp.concatenate([rank0, rank1], axis=0).astype(jnp.int32)
    carry = carry + jnp.sum(oh, axis=1, keepdims=True)
    carry_ref[...] = carry
    cnt_ref[...] = carry.astype(jnp.int32)


def _dest_kernel(start_ref, pe_ref, rank_ref, dest_ref):
    pe = pe_ref[...]
    base = jnp.zeros_like(pe)
    for e in range(N_EXPERTS):
        base = jnp.where(pe == e, start_ref[e], base)
    dest_ref[...] = base + rank_ref[...]


def _dispatch_kernel(nv_ref, dest_ref, h_ref, buf_ref, zero_ref, hbuf_ref, sem, lsem, zsem):
    @pl.when(pl.program_id(0) == 0)
    def _():
        zero_ref[...] = jnp.zeros_like(zero_ref)

        def zcopy(blk):
            return pltpu.make_async_copy(zero_ref, _token_rows(buf_ref, blk * EXPERT_ROWS, EXPERT_ROWS), zsem)

        def zstart(blk, _):
            @pl.when(nv_ref[blk] < EXPERT_ROWS)
            def _():
                zcopy(blk).start()
            return 0
        lax.fori_loop(0, nv_ref.shape[0], zstart, 0)

        def zwait(blk, _):
            @pl.when(nv_ref[blk] < EXPERT_ROWS)
            def _():
                zcopy(blk).wait()
            return 0
        lax.fori_loop(0, nv_ref.shape[0], zwait, 0)

    step = pl.program_id(0)
    n_steps = pl.num_programs(0)
    cur = lax.rem(step, 3)
    prev = lax.rem(step + 2, 3)

    def load(blk, b):
        return pltpu.make_async_copy(_token_rows(h_ref, blk * DISPATCH_ROWS, DISPATCH_ROWS), hbuf_ref.at[b], lsem.at[b])

    def wait_scatter(b):
        for _ in range(TOP_K):
            pltpu.make_async_copy(hbuf_ref.at[b], _token_rows(buf_ref, 0, DISPATCH_ROWS), sem.at[b]).wait()

    @pl.when(step == 0)
    def _():
        load(0, 0).start()

        @pl.when(n_steps > 1)
        def _():
            load(1, 1).start()

    load(step, cur).wait()
    src_buf = hbuf_ref.at[cur]
    for t0 in range(0, DISPATCH_ROWS, INDEX_GROUP):
        dst = [[dest_ref[slot, t0 + k] for slot in range(TOP_K)] for k in range(INDEX_GROUP)]
        for k in range(INDEX_GROUP):
            for slot in range(TOP_K):
                pltpu.make_async_copy(_token_rows(src_buf, t0 + k), _token_rows(buf_ref, dst[k][slot]), sem.at[cur]).start()

    @pl.when(step > 0)
    def _():
        wait_scatter(prev)

    @pl.when(step + 2 < n_steps)
    def _():
        load(step + 2, prev).start()

    @pl.when(step == n_steps - 1)
    def _():
        wait_scatter(cur)


def _expert_kernel(be_ref, nv_ref, bi_ref, x_ref, wg_ref, wu_ref, wd_ref, y_ref, wgb_ref, wub_ref, wdb_ref):
    i = pl.program_id(0)
    nv = nv_ref[i]
    new_expert = jnp.logical_or(i == 0, be_ref[i] != be_ref[jnp.maximum(i - 1, 0)])

    @pl.when(new_expert)
    def _():
        wgb_ref[...] = wg_ref[...].astype(BF16)
        wub_ref[...] = wu_ref[...].astype(BF16)
        wdb_ref[...] = wd_ref[...].astype(BF16)

    @pl.when(nv == 0)
    def _():
        y_ref[...] = jnp.zeros_like(y_ref)

    @pl.when(nv > 0)
    def _():
        x = _load_token_tiles(x_ref, 0, EXPERT_ROWS).astype(BF16)
        gate = jnp.dot(x, wgb_ref[...], preferred_element_type=F32)
        up = jnp.dot(x, wub_ref[...], preferred_element_type=F32)
        act = (gate * jax.nn.sigmoid(gate) * up).astype(BF16)
        _store_token_tiles(y_ref, 0, jnp.dot(act, wdb_ref[...], preferred_element_type=F32))


def _combine_kernel(dcur_ref, dnext_ref, h_ref, pw_ref, g_ref, b_ref, y_ref, o_ref, ya_ref, yb_ref, sem_a, sem_b):
    half = DISPATCH_ROWS
    step = pl.program_id(0)

    def copy(src_tok, ybuf, t, slot, sem):
        return pltpu.make_async_copy(_token_rows(y_ref, src_tok), _token_rows(ybuf.at[slot], t), sem)

    def issue(dref, col0, ybuf, sem):
        for t0 in range(0, half, INDEX_GROUP):
            src = [[dref[slot, col0 + t0 + k] for slot in range(TOP_K)] for k in range(INDEX_GROUP)]
            for k in range(INDEX_GROUP):
                for slot in range(TOP_K):
                    copy(src[k][slot], ybuf, t0 + k, slot, sem).start()

    def wait(ybuf, sem):
        for slot in range(TOP_K):
            pltpu.make_async_copy(_token_rows(y_ref, 0, half), ybuf.at[slot], sem).wait()

    def finish(ybuf, r0):
        rows = pl.ds(r0, half)
        pw = pw_ref[rows, :]
        y = pw[:, 0:1] * _load_token_tiles(ybuf.at[0], 0, half) + pw[:, 1:2] * _load_token_tiles(ybuf.at[1], 0, half)
        h = _load_token_tiles(h_ref, r0, half)
        o_ref[rows, :] = _layer_norm(ALPHA * h + y, g_ref[...], b_ref[...], LN_EPS)

    @pl.when(step == 0)
    def _():
        def first(t, _):
            for slot in range(TOP_K):
                copy(dcur_ref[slot, t], ya_ref, t, slot, sem_a).start()
            return 0
        lax.fori_loop(0, half, first, 0)

    issue(dcur_ref, half, yb_ref, sem_b)
    wait(ya_ref, sem_a)
    finish(ya_ref, 0)

    @pl.when(step + 1 < pl.num_programs(0))
    def _():
        issue(dnext_ref, 0, ya_ref, sem_a)

    wait(yb_ref, sem_b)
    finish(yb_ref, half)


def _tables(seq):
    log_g = jnp.log1p(-jnp.power(2.0, -5.0 - jnp.arange(RET_HEADS, dtype=F32)))
    i = jnp.arange(CHUNK, dtype=F32)
    rel = i[:, None] - i[None, :]
    dmask = jnp.where(rel[None] >= 0, jnp.exp(jnp.maximum(rel, 0.0)[None] * log_g[:, None, None]), 0.0)
    dmask = dmask.reshape(RET_HEADS * CHUNK, CHUNK)
    lg_lane = jnp.repeat(log_g, RET_DK)
    xi = jnp.exp((i + 1)[:, None] * lg_lane[None, :])
    zeta = jnp.exp((CHUNK - 1 - i)[:, None] * lg_lane[None, :])
    decay = jnp.broadcast_to(jnp.exp(CHUNK * lg_lane)[:, None], (RET_QK_WIDTH, RET_WIDTH))
    bd = (jnp.arange(RET_QK_WIDTH)[:, None] // RET_DK == jnp.arange(RET_WIDTH)[None, :] // RET_DV).astype(F32)
    zeta_meta = zeta[CHUNK - N_META:, :]
    half = RET_DK // 2
    inv = ROPE_BASE ** (-jnp.arange(half, dtype=F32) / half)
    pos = jnp.arange(N_META + seq, dtype=F32)
    ang = pos[:, None] * inv[None, :]
    cos = jnp.tile(jnp.cos(ang), (1, 2 * RET_HEADS))
    sin = jnp.tile(jnp.concatenate([-jnp.sin(ang), jnp.sin(ang)], axis=1), (1, RET_HEADS))
    return dict(dmask=dmask, xi=xi, zeta=zeta, decay=decay, bd=bd, zeta_meta=zeta_meta,
                cos_meta=cos[:N_META], sin_meta=sin[:N_META], cos=cos[N_META:], sin=sin[N_META:])


def _full(shape):
    return pl.BlockSpec(shape, lambda *_: (0,) * len(shape))


def kernel(x, meta_tokens, ln_emb_g, ln_emb_b, w_in, pool_w, pool_b, pool_scale, w_out, ln1_g, ln1_b, router_group_w, router_group_b, router_expert_w, router_expert_b, expert_w_gate, expert_w_up, expert_w_down, ln2_g, ln2_b):
    batch, seq, d = x.shape
    assert d == D_MODEL and seq % MIX_ROWS == 0 and (batch * seq) % DISPATCH_ROWS == 0
    n_tok = batch * seq
    t_blocks = seq // MIX_ROWS
    tb = _tables(seq)

    row = lambda a: a.reshape(1, -1).astype(F32)
    win_b = w_in[0].astype(BF16)
    wout_b = w_out[0].astype(BF16)
    poolw_b = pool_w[0].astype(BF16)
    wr = jnp.zeros((ROUTER_ROWS, D_MODEL), F32)
    wr = wr.at[0:N_GROUPS].set(router_group_w[0].T).at[8:8 + N_EXPERTS].set(router_expert_w[0].T).astype(BF16)
    br = jnp.zeros((ROUTER_ROWS, 1), F32)
    br = br.at[0:N_GROUPS, 0].set(router_group_b[0]).at[8:8 + N_EXPERTS, 0].set(router_expert_b[0])

    s0, tail0 = pl.pallas_call(
        _meta_kernel,
        out_shape=(jax.ShapeDtypeStruct((RET_QK_WIDTH, RET_WIDTH), F32), jax.ShapeDtypeStruct((N_META, POOL_WIDTH), F32)),
        name="meta_prep",
    )(meta_tokens.astype(F32), row(ln_emb_g), row(ln_emb_b), win_b, tb["cos_meta"], tb["sin_meta"], tb["zeta_meta"], tb["bd"])

    tok_spec = pl.BlockSpec((None, MIX_ROWS, D_MODEL), lambda b, j: (b, j, 0))
    pair_spec = pl.BlockSpec((TOP_K, MIX_ROWS), lambda b, j: (0, b * t_blocks + j))
    rope_spec = pl.BlockSpec((MIX_ROWS, RET_QK_WIDTH), lambda b, j: (j, 0))
    h1, pair_e, rank, pair_w, counts = pl.pallas_call(
        _mixer_kernel,
        grid=(batch, t_blocks),
        in_specs=[tok_spec, _full((1, D_MODEL)), _full((1, D_MODEL)), _full((D_MODEL, IN_COLS)), _full((D_MODEL, D_MODEL)),
                  _full((POOL_GROUPS, POOL_CH, POOL_CH)), _full((1, POOL_WIDTH)), _full((1, POOL_WIDTH)),
                  _full((1, D_MODEL)), _full((1, D_MODEL)), rope_spec, rope_spec,
                  _full((RET_HEADS * CHUNK, CHUNK)), _full((CHUNK, RET_QK_WIDTH)), _full((CHUNK, RET_QK_WIDTH)),
                  _full((RET_QK_WIDTH, RET_WIDTH)), _full((RET_QK_WIDTH, RET_WIDTH)),
                  _full((RET_QK_WIDTH, RET_WIDTH)), _full((N_META, POOL_WIDTH)),
                  _full((ROUTER_ROWS, D_MODEL)), _full((ROUTER_ROWS, 1))],
        out_specs=[pl.BlockSpec((MIX_ROWS * TOKEN_SUBLANES, LANES), lambda b, j: (b * t_blocks + j, 0)),
                   pair_spec, pair_spec, pair_spec, _full((N_EXPERTS, 128))],
        out_shape=[jax.ShapeDtypeStruct((n_tok * TOKEN_SUBLANES, LANES), F32),
                   jax.ShapeDtypeStruct((TOP_K, n_tok), jnp.int32),
                   jax.ShapeDtypeStruct((TOP_K, n_tok), jnp.int32),
                   jax.ShapeDtypeStruct((TOP_K, n_tok), F32),
                   jax.ShapeDtypeStruct((N_EXPERTS, 128), jnp.int32)],
        scratch_shapes=[pltpu.VMEM((RET_QK_WIDTH, RET_WIDTH), F32),
                        pltpu.VMEM((N_META, POOL_WIDTH), F32),
                        pltpu.VMEM((MIX_ROWS + N_META, POOL_WIDTH), F32),
                        pltpu.VMEM((MIX_ROWS, IN_COLS), F32),
                        pltpu.VMEM((MIX_ROWS, D_MODEL), F32),
                        pltpu.VMEM((MIX_ROWS, D_MODEL), BF16),
                        pltpu.VMEM((MIX_ROWS, D_MODEL), BF16),
                        pltpu.VMEM((MIX_ROWS, MIX_ROWS), BF16),
                        pltpu.VMEM((N_EXPERTS, 128), F32)],
        compiler_params=pltpu.CompilerParams(dimension_semantics=("arbitrary", "arbitrary"),
                                             vmem_limit_bytes=VMEM_LIMIT),
        name="mixer",
    )(x, row(ln_emb_g), row(ln_emb_b), win_b, wout_b, poolw_b, row(pool_b[0]), row(pool_scale[0]),
      row(ln1_g[0]), row(ln1_b[0]), tb["cos"], tb["sin"], tb["dmask"], tb["xi"], tb["zeta"], tb["decay"], tb["bd"],
      s0, tail0, wr, br)

    cnt = counts[:, 0]
    nblk = (cnt + EXPERT_ROWS - 1) // EXPERT_ROWS
    blk_end = jnp.cumsum(nblk)
    row_start = (blk_end - nblk) * EXPERT_ROWS
    n_blocks = (n_tok * TOP_K) // EXPERT_ROWS + N_EXPERTS
    bidx = jnp.arange(n_blocks, dtype=jnp.int32)
    used = bidx < blk_end[-1]
    last = jnp.maximum(blk_end[-1] - 1, 0).astype(jnp.int32)
    blk_src = jnp.where(used, bidx, last)
    in_expert = blk_src[:, None] >= blk_end[None, :]
    blk_e = jnp.minimum(jnp.sum(in_expert, axis=1), N_EXPERTS - 1).astype(jnp.int32)
    rows_in_expert = blk_src * EXPERT_ROWS - jnp.sum(jnp.where(in_expert, (nblk * EXPERT_ROWS)[None, :], 0), axis=1)
    cnt_e = jnp.sum(jnp.where(jnp.arange(N_EXPERTS)[None, :] == blk_e[:, None], cnt[None, :], 0), axis=1)
    blk_nv = jnp.where(used, jnp.clip(cnt_e - rows_in_expert, 0, EXPERT_ROWS), 0).astype(jnp.int32)

    dest_lanes = min(DEST_LANES, n_tok)
    assert n_tok % dest_lanes == 0
    dest = pl.pallas_call(
        _dest_kernel,
        grid_spec=pltpu.PrefetchScalarGridSpec(
            num_scalar_prefetch=1,
            grid=(n_tok // dest_lanes,),
            in_specs=[pl.BlockSpec((TOP_K, dest_lanes), lambda i, s: (0, i)),
                      pl.BlockSpec((TOP_K, dest_lanes), lambda i, s: (0, i))],
            out_specs=pl.BlockSpec((TOP_K, dest_lanes), lambda i, s: (0, i)),
        ),
        out_shape=jax.ShapeDtypeStruct((TOP_K, n_tok), jnp.int32),
        name="dest_index",
    )(row_start.astype(jnp.int32), pair_e, rank)

    tile_block = (EXPERT_ROWS * TOKEN_SUBLANES, LANES)
    sorted_shape = (n_blocks * EXPERT_ROWS * TOKEN_SUBLANES, LANES)
    buf = pl.pallas_call(
        _dispatch_kernel,
        grid_spec=pltpu.PrefetchScalarGridSpec(
            num_scalar_prefetch=1,
            grid=(n_tok // DISPATCH_ROWS,),
            in_specs=[pl.BlockSpec((TOP_K, DISPATCH_ROWS), lambda i, nv: (0, i), memory_space=pltpu.SMEM),
                      pl.BlockSpec(memory_space=pl.ANY)],
            out_specs=pl.BlockSpec(memory_space=pl.ANY),
            scratch_shapes=[pltpu.VMEM(tile_block, F32),
                            pltpu.VMEM((3, DISPATCH_ROWS * TOKEN_SUBLANES, LANES), F32),
                            pltpu.SemaphoreType.DMA((3,)), pltpu.SemaphoreType.DMA((3,)), pltpu.SemaphoreType.DMA],
        ),
        out_shape=jax.ShapeDtypeStruct(sorted_shape, F32),
        compiler_params=pltpu.CompilerParams(dimension_semantics=("arbitrary",)),
        name="dispatch",
    )(blk_nv, dest, h1)

    y_sorted = pl.pallas_call(
        _expert_kernel,
        grid_spec=pltpu.PrefetchScalarGridSpec(
            num_scalar_prefetch=3,
            grid=(n_blocks,),
            in_specs=[pl.BlockSpec(tile_block, lambda i, be, nv, bi: (bi[i], 0)),
                      pl.BlockSpec((None, D_MODEL, D_EXPERT), lambda i, be, nv, bi: (be[i], 0, 0)),
                      pl.BlockSpec((None, D_MODEL, D_EXPERT), lambda i, be, nv, bi: (be[i], 0, 0)),
                      pl.BlockSpec((None, D_EXPERT, D_MODEL), lambda i, be, nv, bi: (be[i], 0, 0))],
            out_specs=pl.BlockSpec(tile_block, lambda i, be, nv, bi: (i, 0)),
            scratch_shapes=[pltpu.VMEM((D_MODEL, D_EXPERT), BF16), pltpu.VMEM((D_MODEL, D_EXPERT), BF16),
                            pltpu.VMEM((D_EXPERT, D_MODEL), BF16)],
        ),
        out_shape=jax.ShapeDtypeStruct(sorted_shape, F32),
        compiler_params=pltpu.CompilerParams(dimension_semantics=("arbitrary",), vmem_limit_bytes=VMEM_LIMIT),
        name="experts",
    )(blk_e, blk_nv, blk_src, buf, expert_w_gate[0], expert_w_up[0], expert_w_down[0])

    comb_rows = 2 * DISPATCH_ROWS
    comb_steps = n_tok // comb_rows
    assert n_tok % comb_rows == 0
    rows_spec = pl.BlockSpec((comb_rows, D_MODEL), lambda i: (i, 0))
    out = pl.pallas_call(
        _combine_kernel,
        grid=(comb_steps,),
        in_specs=[pl.BlockSpec((TOP_K, comb_rows), lambda i: (0, i), memory_space=pltpu.SMEM),
                  pl.BlockSpec((TOP_K, comb_rows), lambda i: (0, jnp.minimum(i + 1, comb_steps - 1)),
                               memory_space=pltpu.SMEM),
                  pl.BlockSpec((comb_rows * TOKEN_SUBLANES, LANES), lambda i: (i, 0)),
                  pl.BlockSpec((comb_rows, TOP_K), lambda i: (i, 0)),
                  pl.BlockSpec((1, D_MODEL), lambda i: (0, 0)), pl.BlockSpec((1, D_MODEL), lambda i: (0, 0)),
                  pl.BlockSpec(memory_space=pl.ANY)],
        out_specs=rows_spec,
        out_shape=jax.ShapeDtypeStruct((n_tok, D_MODEL), F32),
        scratch_shapes=[pltpu.VMEM((TOP_K, DISPATCH_ROWS * TOKEN_SUBLANES, LANES), F32),
                        pltpu.VMEM((TOP_K, DISPATCH_ROWS * TOKEN_SUBLANES, LANES), F32),
                        pltpu.SemaphoreType.DMA, pltpu.SemaphoreType.DMA],
        compiler_params=pltpu.CompilerParams(dimension_semantics=("arbitrary",), vmem_limit_bytes=VMEM_LIMIT),
        name="combine",
    )(dest, dest, h1, pair_w.T, row(ln2_g[0]), row(ln2_b[0]), y_sorted)
    return out.reshape(batch, seq, D_MODEL)
```

```python
import functools

import jax
import jax.numpy as jnp
from jax import lax
from jax.experimental import pallas as pl
from jax.experimental.pallas import tpu as pltpu

D_MODEL = 1024
DEPTH = 1
N_META = 16
RET_HEADS = 4
RET_WIDTH = D_MODEL // 2
RET_DV = RET_WIDTH // RET_HEADS
RET_DK = RET_DV // 2
RET_QK_WIDTH = RET_HEADS * RET_DK
CHUNK = 128
ROPE_BASE = 10000.0
POOL_WINDOWS = (2, 4, 8, 16)
POOL_GROUPS = len(POOL_WINDOWS)
POOL_WIDTH = D_MODEL // 2
POOL_CH = POOL_WIDTH // POOL_GROUPS
IN_COLS = 2 * RET_QK_WIDTH + 2 * RET_WIDTH + POOL_WIDTH
N_GROUPS = 4
EXPERTS_PER_GROUP = 8
N_EXPERTS = N_GROUPS * EXPERTS_PER_GROUP
D_EXPERT = D_MODEL // 2
TOP_K = 2
LN_EPS = 1e-5
GN_EPS = 1e-6
ALPHA = (2 * DEPTH) ** 0.25

Q0, K0, V0, G0, U0 = 0, RET_QK_WIDTH, 2 * RET_QK_WIDTH, 2 * RET_QK_WIDTH + RET_WIDTH, 2 * RET_QK_WIDTH + 2 * RET_WIDTH

MIX_ROWS = 512
EXPERT_ROWS = 512
DISPATCH_ROWS = 256
ROUTER_ROWS = 40
DEST_LANES = 8192
COMBINE_PHASES = 4
COMBINE_PART_ROWS = 128
INDEX_GROUP = 8
VMEM_LIMIT = 56 * 1024 * 1024
LANES = 128
TOKEN_SUBLANES = D_MODEL // LANES

F32 = jnp.float32
BF16 = jnp.bfloat16


def _layer_norm(x, g, b, eps):
    mu = jnp.mean(x, axis=-1, keepdims=True)
    xc = x - mu
    var = jnp.mean(xc * xc, axis=-1, keepdims=True)
    return xc * lax.rsqrt(var + eps) * g + b


def _rotary(z, cos, sin_signed, first_half):
    partner = jnp.where(first_half, pltpu.roll(z, RET_QK_WIDTH - RET_DK // 2, 1), pltpu.roll(z, RET_DK // 2, 1))
    return z * cos + partner * sin_signed


def _store_token_tiles(ref, tok0, x):
    n = x.shape[0]
    for s in range(TOKEN_SUBLANES):
        ref[pl.ds(tok0 * TOKEN_SUBLANES + s, n, stride=TOKEN_SUBLANES), :] = x[:, s * LANES:(s + 1) * LANES]


def _load_token_tiles(ref, tok0, n):
    return jnp.concatenate(
        [ref[pl.ds(tok0 * TOKEN_SUBLANES + s, n, stride=TOKEN_SUBLANES), :] for s in range(TOKEN_SUBLANES)], axis=1)


def _token_rows(ref, tok, n=1):
    return ref.at[pl.ds(pl.multiple_of(tok * TOKEN_SUBLANES, TOKEN_SUBLANES), n * TOKEN_SUBLANES), :]


def _first_half_mask(rows):
    lane = lax.broadcasted_iota(jnp.int32, (rows, RET_QK_WIDTH), 1)
    return (lane % RET_DK) < (RET_DK // 2)


def _meta_kernel(meta_ref, g_ref, b_ref, win_ref, cos_ref, sin_ref, zeta_ref, bd_ref, s0_ref, tail_ref):
    h = _layer_norm(meta_ref[...], g_ref[...], b_ref[...], LN_EPS)
    proj = jnp.dot(h.astype(BF16), win_ref[...], preferred_element_type=F32)
    k = _rotary(proj[:, K0:V0], cos_ref[...], sin_ref[...], _first_half_mask(N_META)) * (RET_DK ** -0.5)
    kz = (k * zeta_ref[...]).astype(BF16)
    v = proj[:, V0:G0].astype(BF16)
    kv = lax.dot_general(kz, v, (((0,), (0,)), ((), ())), preferred_element_type=F32)
    s0_ref[...] = kv * bd_ref[...]
    tail_ref[...] = proj[:, U0:]


def _mixer_kernel(x_ref, lng_ref, lnb_ref, win_ref, wout_ref, poolw_ref, poolb_ref, pools_ref, ln1g_ref, ln1b_ref,
                  cos_ref, sin_ref, dmask_ref, xi_ref, zeta_ref, decay_ref, bd_ref, s0_ref, tail0_ref,
                  wr_ref, br_ref,
                  h1_ref, pe_ref, rank_ref, pw_ref, cnt_ref,
                  state_ref, tail_ref, uext_ref, proj_ref, h0_ref, h0b_ref, mixin_ref, tri_ref, carry_ref):
    rows = x_ref.shape[0]
    n_chunks = rows // CHUNK
    first_step = jnp.logical_and(pl.program_id(0) == 0, pl.program_id(1) == 0)

    @pl.when(first_step)
    def _():
        r = lax.broadcasted_iota(jnp.int32, (rows, rows), 0)
        c = lax.broadcasted_iota(jnp.int32, (rows, rows), 1)
        tri_ref[...] = jnp.where(r < c, 1.0, 0.0).astype(BF16)
        carry_ref[...] = jnp.zeros_like(carry_ref)

    @pl.when(pl.program_id(1) == 0)
    def _():
        state_ref[...] = s0_ref[...]
        tail_ref[...] = tail0_ref[...]

    def ln_body(c, _):
        sl = pl.ds(pl.multiple_of(c * CHUNK, CHUNK), CHUNK)
        h0 = _layer_norm(x_ref[sl, :], lng_ref[...], lnb_ref[...], LN_EPS)
        h0_ref[sl, :] = h0
        h0b_ref[sl, :] = h0.astype(BF16)
        return 0
    lax.fori_loop(0, n_chunks, ln_body, 0, unroll=True)

    proj_ref[...] = jnp.dot(h0b_ref[...], win_ref[...], preferred_element_type=F32)

    first_half = _first_half_mask(CHUNK)
    head_of_lane = lax.broadcasted_iota(jnp.int32, (CHUNK, RET_QK_WIDTH), 1) // RET_DK

    def ret_body(c, _):
        sl = pl.ds(pl.multiple_of(c * CHUNK, CHUNK), CHUNK)
        cos = cos_ref[sl, :]
        sin = sin_ref[sl, :]
        q = _rotary(proj_ref[sl, Q0:K0], cos, sin, first_half)
        k = _rotary(proj_ref[sl, K0:V0], cos, sin, first_half) * (RET_DK ** -0.5)
        qb = q.astype(BF16)
        kb = k.astype(BF16)
        vb = proj_ref[sl, V0:G0].astype(BF16)
        q_heads = jnp.concatenate(
            [jnp.where(head_of_lane == h, qb, jnp.zeros_like(qb)) for h in range(RET_HEADS)], axis=0)
        scores = lax.dot_general(q_heads, kb, (((1,), (1,)), ((), ())), preferred_element_type=F32)
        p = (scores * dmask_ref[...]).astype(BF16)
        inner = jnp.concatenate(
            [jnp.dot(p[h * CHUNK:(h + 1) * CHUNK, :], vb[:, h * RET_DV:(h + 1) * RET_DV],
                     preferred_element_type=F32) for h in range(RET_HEADS)], axis=1)
        state = state_ref[...]
        cross = jnp.dot((q * xi_ref[...]).astype(BF16), state.astype(BF16), preferred_element_type=F32)
        kz = (k * zeta_ref[...]).astype(BF16)
        kv = lax.dot_general(kz, vb, (((0,), (0,)), ((), ())), preferred_element_type=F32)
        state_ref[...] = state * decay_ref[...] + kv * bd_ref[...]
        ret = inner + cross
        gate = proj_ref[sl, G0:U0]
        outs = []
        for h in range(RET_HEADS):
            o = ret[:, h * RET_DV:(h + 1) * RET_DV]
            mu = jnp.mean(o, axis=-1, keepdims=True)
            oc = o - mu
            var = jnp.mean(oc * oc, axis=-1, keepdims=True)
            outs.append(oc * lax.rsqrt(var + GN_EPS))
        gn = jnp.concatenate(outs, axis=1)
        mixin_ref[sl, 0:RET_WIDTH] = (gate * jax.nn.sigmoid(gate) * gn).astype(BF16)
        return 0
    lax.fori_loop(0, n_chunks, ret_body, 0, unroll=True)

    uext_ref[0:N_META, :] = tail_ref[...]
    uext_ref[N_META:, :] = proj_ref[:, U0:]
    tail_ref[...] = uext_ref[rows:, :]
    for g, w in enumerate(POOL_WINDOWS):
        lanes = slice(g * POOL_CH, (g + 1) * POOL_CH)
        e = uext_ref[:, lanes]
        acc = e
        shift = 1
        while shift < w:
            acc = acc + pltpu.roll(acc, shift, 0)
            shift *= 2
        pooled = acc[N_META:, :] * (1.0 / w) - e[N_META:, :]
        mixed = jnp.dot(pooled.astype(BF16), poolw_ref[g], preferred_element_type=F32) + poolb_ref[:, lanes]
        mixin_ref[:, RET_WIDTH + g * POOL_CH:RET_WIDTH + (g + 1) * POOL_CH] = (mixed * pools_ref[:, lanes]).astype(BF16)

    proj_ref[:, 0:D_MODEL] = jnp.dot(mixin_ref[...], wout_ref[...], preferred_element_type=F32)

    def ln1_body(c, _):
        sl = pl.ds(pl.multiple_of(c * CHUNK, CHUNK), CHUNK)
        h1 = _layer_norm(ALPHA * h0_ref[sl, :] + proj_ref[sl, 0:D_MODEL], ln1g_ref[...], ln1b_ref[...], LN_EPS)
        _store_token_tiles(h1_ref, c * CHUNK, h1)
        h0b_ref[sl, :] = h1.astype(BF16)
        return 0
    lax.fori_loop(0, n_chunks, ln1_body, 0, unroll=True)

    logits = lax.dot_general(wr_ref[...], h0b_ref[...], (((1,), (1,)), ((), ())), preferred_element_type=F32)
    logits = logits + br_ref[...]
    gl = logits[0:N_GROUPS, :]
    gmax = jnp.max(gl, axis=0, keepdims=True)
    g_p = 1.0 / jnp.sum(jnp.exp(gl - gmax), axis=0, keepdims=True)
    grow = lax.broadcasted_iota(jnp.int32, gl.shape, 0)
    g_idx = jnp.min(jnp.where(gl == gmax, grow, N_GROUPS), axis=0, keepdims=True)
    sel = logits[8:8 + EXPERTS_PER_GROUP, :]
    for g in range(1, N_GROUPS):
        sel = jnp.where(g_idx == g, logits[8 + g * EXPERTS_PER_GROUP:8 + (g + 1) * EXPERTS_PER_GROUP, :], sel)
    erow = lax.broadcasted_iota(jnp.int32, sel.shape, 0)
    m1 = jnp.max(sel, axis=0, keepdims=True)
    i1 = jnp.min(jnp.where(sel == m1, erow, EXPERTS_PER_GROUP), axis=0, keepdims=True)
    sel2 = jnp.where(erow == i1, -jnp.inf, sel)
    m2 = jnp.max(sel2, axis=0, keepdims=True)
    i2 = jnp.min(jnp.where(sel2 == m2, erow, EXPERTS_PER_GROUP), axis=0, keepdims=True)
    e2 = jnp.exp(m2 - m1)
    w1 = 1.0 / (1.0 + e2)
    w2 = e2 / (1.0 + e2)
    pe0 = g_idx * EXPERTS_PER_GROUP + i1
    pe1 = g_idx * EXPERTS_PER_GROUP + i2
    pe_ref[...] = jnp.concatenate([pe0, pe1], axis=0)
    pw_ref[...] = jnp.concatenate([g_p * w1, g_p * w2], axis=0)
    xrow = lax.broadcasted_iota(jnp.int32, (N_EXPERTS, rows), 0)
    oh0 = xrow == pe0
    oh1 = xrow == pe1
    oh = jnp.where(jnp.logical_or(oh0, oh1), 1.0, 0.0)
    carry = carry_ref[...]
    prefix = jnp.dot(oh.astype(BF16), tri_ref[...], preferred_element_type=F32) + carry[:, 0:1]
    rank0 = jnp.sum(jnp.where(oh0, prefix, 0.0), axis=0, keepdims=True)
    rank1 = jnp.sum(jnp.where(oh1, prefix, 0.0), axis=0, keepdims=True)
    rank_ref[...] = jnp.concatenate([rank0, rank1], axis=0).astype(jnp.int32)
    carry = carry + jnp.sum(oh, axis=1, keepdims=True)
    carry_ref[...] = carry
    cnt_ref[...] = carry.astype(jnp.int32)


def _dest_kernel(start_ref, pe_ref, rank_ref, dest_ref):
    pe = pe_ref[...]
    base = jnp.zeros_like(pe)
    for e in range(N_EXPERTS):
        base = jnp.where(pe == e, start_ref[e], base)
    dest_ref[...] = base + rank_ref[...]


def _dispatch_kernel(nv_ref, dest_ref, h_ref, buf_ref, zero_ref, hbuf_ref, sem, lsem, zsem):
    @pl.when(pl.program_id(0) == 0)
    def _():
        zero_ref[...] = jnp.zeros_like(zero_ref)

        def zcopy(blk):
            return pltpu.make_async_copy(zero_ref, _token_rows(buf_ref, blk * EXPERT_ROWS, EXPERT_ROWS), zsem)

        def zstart(blk, _):
            @pl.when(nv_ref[blk] < EXPERT_ROWS)
            def _():
                zcopy(blk).start()
            return 0
        lax.fori_loop(0, nv_ref.shape[0], zstart, 0)

        def zwait(blk, _):
            @pl.when(nv_ref[blk] < EXPERT_ROWS)
            def _():
                zcopy(blk).wait()
            return 0
        lax.fori_loop(0, nv_ref.shape[0], zwait, 0)

    step = pl.program_id(0)
    n_steps = pl.num_programs(0)
    cur = lax.rem(step, 3)
    prev = lax.rem(step + 2, 3)

    def load(blk, b):
        return pltpu.make_async_copy(_token_rows(h_ref, blk * DISPATCH_ROWS, DISPATCH_ROWS), hbuf_ref.at[b], lsem.at[b])

    def wait_scatter(b):
        for _ in range(TOP_K):
            pltpu.make_async_copy(hbuf_ref.at[b], _token_rows(buf_ref, 0, DISPATCH_ROWS), sem.at[b]).wait()

    @pl.when(step == 0)
    def _():
        load(0, 0).start()

        @pl.when(n_steps > 1)
        def _():
            load(1, 1).start()

    load(step, cur).wait()
    src_buf = hbuf_ref.at[cur]
    for t0 in range(0, DISPATCH_ROWS, INDEX_GROUP):
        dst = [[dest_ref[slot, t0 + k] for slot in range(TOP_K)] for k in range(INDEX_GROUP)]
        for k in range(INDEX_GROUP):
            for slot in range(TOP_K):
                pltpu.make_async_copy(_token_rows(src_buf, t0 + k), _token_rows(buf_ref, dst[k][slot]),
                                      sem.at[cur]).start(priority=slot % 2)

    @pl.when(step > 0)
    def _():
        wait_scatter(prev)

    @pl.when(step + 2 < n_steps)
    def _():
        load(step + 2, prev).start()

    @pl.when(step == n_steps - 1)
    def _():
        wait_scatter(cur)


def _expert_kernel(be_ref, nv_ref, bi_ref, x_ref, wg_ref, wu_ref, wd_ref, y_ref, wgb_ref, wub_ref, wdb_ref):
    i = pl.program_id(0)
    nv = nv_ref[i]
    new_expert = jnp.logical_or(i == 0, be_ref[i] != be_ref[jnp.maximum(i - 1, 0)])

    @pl.when(new_expert)
    def _():
        wgb_ref[...] = wg_ref[...].astype(BF16)
        wub_ref[...] = wu_ref[...].astype(BF16)
        wdb_ref[...] = wd_ref[...].astype(BF16)

    @pl.when(nv == 0)
    def _():
        y_ref[...] = jnp.zeros_like(y_ref)

    @pl.when(nv > 0)
    def _():
        x = _load_token_tiles(x_ref, 0, EXPERT_ROWS).astype(BF16)
        gate = jnp.dot(x, wgb_ref[...], preferred_element_type=F32)
        up = jnp.dot(x, wub_ref[...], preferred_element_type=F32)
        act = (gate * jax.nn.sigmoid(gate) * up).astype(BF16)
        _store_token_tiles(y_ref, 0, jnp.dot(act, wdb_ref[...], preferred_element_type=F32))


def _combine_kernel(dcur_ref, dnext_ref, h_ref, pw_ref, g_ref, b_ref, y_ref, o_ref, yb0, yb1, yb2, yb3, sems):
    part = COMBINE_PART_ROWS
    bufs = (yb0, yb1, yb2, yb3)
    step = pl.program_id(0)
    last = pl.num_programs(0) - 1

    def copy(src_tok, ybuf, t, slot, sem):
        return pltpu.make_async_copy(_token_rows(y_ref, src_tok), _token_rows(ybuf.at[slot], t), sem)

    def issue(dref, col0, ybuf, sem):
        for t0 in range(0, part, INDEX_GROUP):
            src = [[dref[slot, col0 + t0 + k] for slot in range(TOP_K)] for k in range(INDEX_GROUP)]
            for k in range(INDEX_GROUP):
                for slot in range(TOP_K):
                    copy(src[k][slot], ybuf, t0 + k, slot, sem).start(priority=slot % 2)

    def wait(ybuf, sem):
        for slot in range(TOP_K):
            pltpu.make_async_copy(_token_rows(y_ref, 0, part), ybuf.at[slot], sem).wait()

    def finish(ybuf, r0):
        rows = pl.ds(r0, part)
        pw = pw_ref[rows, :]
        y = pw[:, 0:1] * _load_token_tiles(ybuf.at[0], 0, part) + pw[:, 1:2] * _load_token_tiles(ybuf.at[1], 0, part)
        h = _load_token_tiles(h_ref, r0, part)
        o_ref[rows, :] = _layer_norm(ALPHA * h + y, g_ref[...], b_ref[...], LN_EPS)

    @pl.when(step == 0)
    def _():
        for p in range(2):
            def first(t, _, p=p):
                for slot in range(TOP_K):
                    copy(dcur_ref[slot, p * part + t], bufs[p], t, slot, sems.at[p]).start()
                return 0
            lax.fori_loop(0, part, first, 0)

    for p in range(COMBINE_PHASES):
        ahead = (p + 2) % COMBINE_PHASES
        wait(bufs[p], sems.at[p])
        if p + 2 < COMBINE_PHASES:
            issue(dcur_ref, (p + 2) * part, bufs[ahead], sems.at[ahead])
        else:
            issue(dnext_ref, ahead * part, bufs[ahead], sems.at[ahead])
        finish(bufs[p], p * part)

    @pl.when(step == last)
    def _():
        wait(bufs[0], sems.at[0])
        wait(bufs[1], sems.at[1])


def _tables(seq):
    log_g = jnp.log1p(-jnp.power(2.0, -5.0 - jnp.arange(RET_HEADS, dtype=F32)))
    i = jnp.arange(CHUNK, dtype=F32)
    rel = i[:, None] - i[None, :]
    dmask = jnp.where(rel[None] >= 0, jnp.exp(jnp.maximum(rel, 0.0)[None] * log_g[:, None, None]), 0.0)
    dmask = dmask.reshape(RET_HEADS * CHUNK, CHUNK)
    lg_lane = jnp.repeat(log_g, RET_DK)
    xi = jnp.exp((i + 1)[:, None] * lg_lane[None, :])
    zeta = jnp.exp((CHUNK - 1 - i)[:, None] * lg_lane[None, :])
    decay = jnp.broadcast_to(jnp.exp(CHUNK * lg_lane)[:, None], (RET_QK_WIDTH, RET_WIDTH))
    bd = (jnp.arange(RET_QK_WIDTH)[:, None] // RET_DK == jnp.arange(RET_WIDTH)[None, :] // RET_DV).astype(F32)
    zeta_meta = zeta[CHUNK - N_META:, :]
    half = RET_DK // 2
    inv = ROPE_BASE ** (-jnp.arange(half, dtype=F32) / half)
    pos = jnp.arange(N_META + seq, dtype=F32)
    ang = pos[:, None] * inv[None, :]
    cos = jnp.tile(jnp.cos(ang), (1, 2 * RET_HEADS))
    sin = jnp.tile(jnp.concatenate([-jnp.sin(ang), jnp.sin(ang)], axis=1), (1, RET_HEADS))
    return dict(dmask=dmask, xi=xi, zeta=zeta, decay=decay, bd=bd, zeta_meta=zeta_meta,
                cos_meta=cos[:N_META], sin_meta=sin[:N_META], cos=cos[N_META:], sin=sin[N_META:])


def _full(shape):
    return pl.BlockSpec(shape, lambda *_: (0,) * len(shape))


def kernel(x, meta_tokens, ln_emb_g, ln_emb_b, w_in, pool_w, pool_b, pool_scale, w_out, ln1_g, ln1_b, router_group_w, router_group_b, router_expert_w, router_expert_b, expert_w_gate, expert_w_up, expert_w_down, ln2_g, ln2_b):
    batch, seq, d = x.shape
    assert d == D_MODEL and seq % MIX_ROWS == 0 and (batch * seq) % DISPATCH_ROWS == 0
    n_tok = batch * seq
    t_blocks = seq // MIX_ROWS
    tb = _tables(seq)

    row = lambda a: a.reshape(1, -1).astype(F32)
    win_b = w_in[0].astype(BF16)
    wout_b = w_out[0].astype(BF16)
    poolw_b = pool_w[0].astype(BF16)
    wr = jnp.zeros((ROUTER_ROWS, D_MODEL), F32)
    wr = wr.at[0:N_GROUPS].set(router_group_w[0].T).at[8:8 + N_EXPERTS].set(router_expert_w[0].T).astype(BF16)
    br = jnp.zeros((ROUTER_ROWS, 1), F32)
    br = br.at[0:N_GROUPS, 0].set(router_group_b[0]).at[8:8 + N_EXPERTS, 0].set(router_expert_b[0])

    s0, tail0 = pl.pallas_call(
        _meta_kernel,
        out_shape=(jax.ShapeDtypeStruct((RET_QK_WIDTH, RET_WIDTH), F32), jax.ShapeDtypeStruct((N_META, POOL_WIDTH), F32)),
        name="meta_prep",
    )(meta_tokens.astype(F32), row(ln_emb_g), row(ln_emb_b), win_b, tb["cos_meta"], tb["sin_meta"], tb["zeta_meta"], tb["bd"])

    tok_spec = pl.BlockSpec((None, MIX_ROWS, D_MODEL), lambda b, j: (b, j, 0))
    pair_spec = pl.BlockSpec((TOP_K, MIX_ROWS), lambda b, j: (0, b * t_blocks + j))
    rope_spec = pl.BlockSpec((MIX_ROWS, RET_QK_WIDTH), lambda b, j: (j, 0))
    h1, pair_e, rank, pair_w, counts = pl.pallas_call(
        _mixer_kernel,
        grid=(batch, t_blocks),
        in_specs=[tok_spec, _full((1, D_MODEL)), _full((1, D_MODEL)), _full((D_MODEL, IN_COLS)), _full((D_MODEL, D_MODEL)),
                  _full((POOL_GROUPS, POOL_CH, POOL_CH)), _full((1, POOL_WIDTH)), _full((1, POOL_WIDTH)),
                  _full((1, D_MODEL)), _full((1, D_MODEL)), rope_spec, rope_spec,
                  _full((RET_HEADS * CHUNK, CHUNK)), _full((CHUNK, RET_QK_WIDTH)), _full((CHUNK, RET_QK_WIDTH)),
                  _full((RET_QK_WIDTH, RET_WIDTH)), _full((RET_QK_WIDTH, RET_WIDTH)),
                  _full((RET_QK_WIDTH, RET_WIDTH)), _full((N_META, POOL_WIDTH)),
                  _full((ROUTER_ROWS, D_MODEL)), _full((ROUTER_ROWS, 1))],
        out_specs=[pl.BlockSpec((MIX_ROWS * TOKEN_SUBLANES, LANES), lambda b, j: (b * t_blocks + j, 0)),
                   pair_spec, pair_spec, pair_spec, _full((N_EXPERTS, 128))],
        out_shape=[jax.ShapeDtypeStruct((n_tok * TOKEN_SUBLANES, LANES), F32),
                   jax.ShapeDtypeStruct((TOP_K, n_tok), jnp.int32),
                   jax.ShapeDtypeStruct((TOP_K, n_tok), jnp.int32),
                   jax.ShapeDtypeStruct((TOP_K, n_tok), F32),
                   jax.ShapeDtypeStruct((N_EXPERTS, 128), jnp.int32)],
        scratch_shapes=[pltpu.VMEM((RET_QK_WIDTH, RET_WIDTH), F32),
                        pltpu.VMEM((N_META, POOL_WIDTH), F32),
                        pltpu.VMEM((MIX_ROWS + N_META, POOL_WIDTH), F32),
                        pltpu.VMEM((MIX_ROWS, IN_COLS), F32),
                        pltpu.VMEM((MIX_ROWS, D_MODEL), F32),
                        pltpu.VMEM((MIX_ROWS, D_MODEL), BF16),
                        pltpu.VMEM((MIX_ROWS, D_MODEL), BF16),
                        pltpu.VMEM((MIX_ROWS, MIX_ROWS), BF16),
                        pltpu.VMEM((N_EXPERTS, 128), F32)],
        compiler_params=pltpu.CompilerParams(dimension_semantics=("arbitrary", "arbitrary"),
                                             vmem_limit_bytes=VMEM_LIMIT),
        name="mixer",
    )(x, row(ln_emb_g), row(ln_emb_b), win_b, wout_b, poolw_b, row(pool_b[0]), row(pool_scale[0]),
      row(ln1_g[0]), row(ln1_b[0]), tb["cos"], tb["sin"], tb["dmask"], tb["xi"], tb["zeta"], tb["decay"], tb["bd"],
      s0, tail0, wr, br)

    cnt = counts[:, 0]
    nblk = (cnt + EXPERT_ROWS - 1) // EXPERT_ROWS
    blk_end = jnp.cumsum(nblk)
    row_start = (blk_end - nblk) * EXPERT_ROWS
    n_blocks = (n_tok * TOP_K) // EXPERT_ROWS + N_EXPERTS
    bidx = jnp.arange(n_blocks, dtype=jnp.int32)
    used = bidx < blk_end[-1]
    last = jnp.maximum(blk_end[-1] - 1, 0).astype(jnp.int32)
    blk_src = jnp.where(used, bidx, last)
    in_expert = blk_src[:, None] >= blk_end[None, :]
    blk_e = jnp.minimum(jnp.sum(in_expert, axis=1), N_EXPERTS - 1).astype(jnp.int32)
    rows_in_expert = blk_src * EXPERT_ROWS - jnp.sum(jnp.where(in_expert, (nblk * EXPERT_ROWS)[None, :], 0), axis=1)
    cnt_e = jnp.sum(jnp.where(jnp.arange(N_EXPERTS)[None, :] == blk_e[:, None], cnt[None, :], 0), axis=1)
    blk_nv = jnp.where(used, jnp.clip(cnt_e - rows_in_expert, 0, EXPERT_ROWS), 0).astype(jnp.int32)

    dest_lanes = min(DEST_LANES, n_tok)
    assert n_tok % dest_lanes == 0
    dest = pl.pallas_call(
        _dest_kernel,
        grid_spec=pltpu.PrefetchScalarGridSpec(
            num_scalar_prefetch=1,
            grid=(n_tok // dest_lanes,),
            in_specs=[pl.BlockSpec((TOP_K, dest_lanes), lambda i, s: (0, i)),
                      pl.BlockSpec((TOP_K, dest_lanes), lambda i, s: (0, i))],
            out_specs=pl.BlockSpec((TOP_K, dest_lanes), lambda i, s: (0, i)),
        ),
        out_shape=jax.ShapeDtypeStruct((TOP_K, n_tok), jnp.int32),
        name="dest_index",
    )(row_start.astype(jnp.int32), pair_e, rank)

    tile_block = (EXPERT_ROWS * TOKEN_SUBLANES, LANES)
    sorted_shape = (n_blocks * EXPERT_ROWS * TOKEN_SUBLANES, LANES)
    buf = pl.pallas_call(
        _dispatch_kernel,
        grid_spec=pltpu.PrefetchScalarGridSpec(
            num_scalar_prefetch=1,
            grid=(n_tok // DISPATCH_ROWS,),
            in_specs=[pl.BlockSpec((TOP_K, DISPATCH_ROWS), lambda i, nv: (0, i), memory_space=pltpu.SMEM),
                      pl.BlockSpec(memory_space=pl.ANY)],
            out_specs=pl.BlockSpec(memory_space=pl.ANY),
            scratch_shapes=[pltpu.VMEM(tile_block, F32),
                            pltpu.VMEM((3, DISPATCH_ROWS * TOKEN_SUBLANES, LANES), F32),
                            pltpu.SemaphoreType.DMA((3,)), pltpu.SemaphoreType.DMA((3,)), pltpu.SemaphoreType.DMA],
        ),
        out_shape=jax.ShapeDtypeStruct(sorted_shape, F32),
        compiler_params=pltpu.CompilerParams(dimension_semantics=("arbitrary",)),
        name="dispatch",
    )(blk_nv, dest, h1)

    y_sorted = pl.pallas_call(
        _expert_kernel,
        grid_spec=pltpu.PrefetchScalarGridSpec(
            num_scalar_prefetch=3,
            grid=(n_blocks,),
            in_specs=[pl.BlockSpec(tile_block, lambda i, be, nv, bi: (bi[i], 0)),
                      pl.BlockSpec((None, D_MODEL, D_EXPERT), lambda i, be, nv, bi: (be[i], 0, 0)),
                      pl.BlockSpec((None, D_MODEL, D_EXPERT), lambda i, be, nv, bi: (be[i], 0, 0)),
                      pl.BlockSpec((None, D_EXPERT, D_MODEL), lambda i, be, nv, bi: (be[i], 0, 0))],
            out_specs=pl.BlockSpec(tile_block, lambda i, be, nv, bi: (i, 0)),
            scratch_shapes=[pltpu.VMEM((D_MODEL, D_EXPERT), BF16), pltpu.VMEM((D_MODEL, D_EXPERT), BF16),
                            pltpu.VMEM((D_EXPERT, D_MODEL), BF16)],
        ),
        out_shape=jax.ShapeDtypeStruct(sorted_shape, F32),
        compiler_params=pltpu.CompilerParams(dimension_semantics=("arbitrary",), vmem_limit_bytes=VMEM_LIMIT),
        name="experts",
    )(blk_e, blk_nv, blk_src, buf, expert_w_gate[0], expert_w_up[0], expert_w_down[0])

    comb_rows = COMBINE_PHASES * COMBINE_PART_ROWS
    comb_steps = n_tok // comb_rows
    assert n_tok % comb_rows == 0
    rows_spec = pl.BlockSpec((comb_rows, D_MODEL), lambda i: (i, 0))
    out = pl.pallas_call(
        _combine_kernel,
        grid=(comb_steps,),
        in_specs=[pl.BlockSpec((TOP_K, comb_rows), lambda i: (0, i), memory_space=pltpu.SMEM),
                  pl.BlockSpec((TOP_K, comb_rows), lambda i: (0, jnp.minimum(i + 1, comb_steps - 1)),
                               memory_space=pltpu.SMEM),
                  pl.BlockSpec((comb_rows * TOKEN_SUBLANES, LANES), lambda i: (i, 0)),
                  pl.BlockSpec((comb_rows, TOP_K), lambda i: (i, 0)),
                  pl.BlockSpec((1, D_MODEL), lambda i: (0, 0)), pl.BlockSpec((1, D_MODEL), lambda i: (0, 0)),
                  pl.BlockSpec(memory_space=pl.ANY)],
        out_specs=rows_spec,
        out_shape=jax.ShapeDtypeStruct((n_tok, D_MODEL), F32),
        scratch_shapes=[pltpu.VMEM((TOP_K, COMBINE_PART_ROWS * TOKEN_SUBLANES, LANES), F32)] * COMBINE_PHASES
                       + [pltpu.SemaphoreType.DMA((COMBINE_PHASES,))],
        compiler_params=pltpu.CompilerParams(dimension_semantics=("arbitrary",), vmem_limit_bytes=VMEM_LIMIT),
        name="combine",
    )(dest, dest, h1, pair_w.T, row(ln2_g[0]), row(ln2_b[0]), y_sorted)
    return out.reshape(batch, seq, D_MODEL)
```

```python
import functools

import jax
import jax.numpy as jnp
from jax import lax
from jax.experimental import pallas as pl
from jax.experimental.pallas import tpu as pltpu

D_MODEL = 1024
DEPTH = 1
N_META = 16
RET_HEADS = 4
RET_WIDTH = D_MODEL // 2
RET_DV = RET_WIDTH // RET_HEADS
RET_DK = RET_DV // 2
RET_QK_WIDTH = RET_HEADS * RET_DK
CHUNK = 128
ROPE_BASE = 10000.0
POOL_WINDOWS = (2, 4, 8, 16)
POOL_GROUPS = len(POOL_WINDOWS)
POOL_WIDTH = D_MODEL // 2
POOL_CH = POOL_WIDTH // POOL_GROUPS
IN_COLS = 2 * RET_QK_WIDTH + 2 * RET_WIDTH + POOL_WIDTH
N_GROUPS = 4
EXPERTS_PER_GROUP = 8
N_EXPERTS = N_GROUPS * EXPERTS_PER_GROUP
D_EXPERT = D_MODEL // 2
TOP_K = 2
LN_EPS = 1e-5
GN_EPS = 1e-6
ALPHA = (2 * DEPTH) ** 0.25

Q0, K0, V0, G0, U0 = 0, RET_QK_WIDTH, 2 * RET_QK_WIDTH, 2 * RET_QK_WIDTH + RET_WIDTH, 2 * RET_QK_WIDTH + 2 * RET_WIDTH

MIX_ROWS = 512
EXPERT_ROWS = 512
ROUTER_ROWS = 40
BLOCK_TABLE_LANES = 384
COMBINE_PHASES = 4
COMBINE_PART_ROWS = 128
INDEX_GROUP = 8
VMEM_LIMIT = 56 * 1024 * 1024
LANES = 128
TOKEN_SUBLANES = D_MODEL // LANES

F32 = jnp.float32
BF16 = jnp.bfloat16


def _layer_norm(x, g, b, eps):
    mu = jnp.mean(x, axis=-1, keepdims=True)
    xc = x - mu
    var = jnp.mean(xc * xc, axis=-1, keepdims=True)
    return xc * lax.rsqrt(var + eps) * g + b


def _rotary(z, cos, sin_signed, first_half):
    partner = jnp.where(first_half, pltpu.roll(z, RET_QK_WIDTH - RET_DK // 2, 1), pltpu.roll(z, RET_DK // 2, 1))
    return z * cos + partner * sin_signed


def _store_token_tiles(ref, tok0, x):
    n = x.shape[0]
    for s in range(TOKEN_SUBLANES):
        ref[pl.ds(tok0 * TOKEN_SUBLANES + s, n, stride=TOKEN_SUBLANES), :] = x[:, s * LANES:(s + 1) * LANES]


def _load_token_tiles(ref, tok0, n):
    return jnp.concatenate(
        [ref[pl.ds(tok0 * TOKEN_SUBLANES + s, n, stride=TOKEN_SUBLANES), :] for s in range(TOKEN_SUBLANES)], axis=1)


def _token_rows(ref, tok, n=1):
    return ref.at[pl.ds(pl.multiple_of(tok * TOKEN_SUBLANES, TOKEN_SUBLANES), n * TOKEN_SUBLANES), :]


def _first_half_mask(rows):
    lane = lax.broadcasted_iota(jnp.int32, (rows, RET_QK_WIDTH), 1)
    return (lane % RET_DK) < (RET_DK // 2)


def _meta_kernel(meta_ref, g_ref, b_ref, win_ref, cos_ref, sin_ref, zeta_ref, bd_ref, s0_ref, tail_ref):
    h = _layer_norm(meta_ref[...], g_ref[...], b_ref[...], LN_EPS)
    proj = jnp.dot(h.astype(BF16), win_ref[...], preferred_element_type=F32)
    k = _rotary(proj[:, K0:V0], cos_ref[...], sin_ref[...], _first_half_mask(N_META)) * (RET_DK ** -0.5)
    kz = (k * zeta_ref[...]).astype(BF16)
    v = proj[:, V0:G0].astype(BF16)
    kv = lax.dot_general(kz, v, (((0,), (0,)), ((), ())), preferred_element_type=F32)
    s0_ref[...] = kv * bd_ref[...]
    tail_ref[...] = proj[:, U0:]


def _mixer_kernel(x_ref, lng_ref, lnb_ref, win_ref, wout_ref, poolw_ref, poolb_ref, pools_ref, ln1g_ref, ln1b_ref,
                  cos_ref, sin_ref, dmask_ref, xi_ref, zeta_ref, decay_ref, bd_ref, s0_ref, tail0_ref,
                  wr_ref, br_ref,
                  h1_ref, dest_ref, pw_ref, cnt_ref, cur_ref, table_ref, buf_ref,
                  state_ref, tail_ref, uext_ref, proj_ref, h0_ref, h0b_ref, mixin_ref, tri_ref, carry_ref,
                  curblk_ref, nalloc_ref, tab_ref, hkeep_ref, zero_ref, dvm_ref, dsm_ref, csm_ref,
                  scat_sem, h1_sem, idx_sem, zsem):
    rows = x_ref.shape[0]
    n_chunks = rows // CHUNK
    step = pl.program_id(0) * pl.num_programs(1) + pl.program_id(1)
    n_steps = pl.num_programs(0) * pl.num_programs(1)
    first_step = step == 0
    slot = lax.rem(step, 2)
    prev = 1 - slot
    n_blocks = buf_ref.shape[0] // (EXPERT_ROWS * TOKEN_SUBLANES) - 2

    def scatter_wait(b):
        for _ in range(TOP_K):
            pltpu.make_async_copy(hkeep_ref.at[b], _token_rows(buf_ref, 0, rows), scat_sem.at[b]).wait()

    def h1_writeback(b, blk):
        return pltpu.make_async_copy(hkeep_ref.at[b], _token_rows(h1_ref, blk * rows, rows), h1_sem.at[b])

    @pl.when(first_step)
    def _():
        r = lax.broadcasted_iota(jnp.int32, (rows, rows), 0)
        c = lax.broadcasted_iota(jnp.int32, (rows, rows), 1)
        tri_ref[...] = jnp.where(r < c, 1.0, 0.0).astype(BF16)
        carry_ref[...] = jnp.zeros_like(carry_ref)
        curblk_ref[...] = jnp.full(curblk_ref.shape, -1.0, F32)
        nalloc_ref[...] = jnp.zeros_like(nalloc_ref)
        tab_ref[...] = jnp.full(tab_ref.shape, float(N_EXPERTS), F32)
        zero_ref[...] = jnp.zeros_like(zero_ref)
        hkeep_ref[1] = jnp.zeros(hkeep_ref.shape[1:], F32)

        def spare(t, _):
            for s in range(TOP_K):
                dsm_ref[s, t] = (n_blocks + s) * EXPERT_ROWS + t
            return 0
        lax.fori_loop(0, rows, spare, 0)

    @pl.when(pl.program_id(1) == 0)
    def _():
        state_ref[...] = s0_ref[...]
        tail_ref[...] = tail0_ref[...]

    @pl.when(step > 0)
    def _():
        pltpu.make_async_copy(dvm_ref, dsm_ref, idx_sem).wait()

    for t0 in range(0, rows, INDEX_GROUP):
        dst = [[dsm_ref[s, t0 + k] for s in range(TOP_K)] for k in range(INDEX_GROUP)]
        for k in range(INDEX_GROUP):
            for s in range(TOP_K):
                pltpu.make_async_copy(_token_rows(hkeep_ref.at[prev], t0 + k), _token_rows(buf_ref, dst[k][s]),
                                      scat_sem.at[prev]).start(priority=s % 2)

    def ln_body(c, _):
        sl = pl.ds(pl.multiple_of(c * CHUNK, CHUNK), CHUNK)
        h0 = _layer_norm(x_ref[sl, :], lng_ref[...], lnb_ref[...], LN_EPS)
        h0_ref[sl, :] = h0
        h0b_ref[sl, :] = h0.astype(BF16)
        return 0
    lax.fori_loop(0, n_chunks, ln_body, 0, unroll=True)

    proj_ref[...] = jnp.dot(h0b_ref[...], win_ref[...], preferred_element_type=F32)

    first_half = _first_half_mask(CHUNK)
    head_of_lane = lax.broadcasted_iota(jnp.int32, (CHUNK, RET_QK_WIDTH), 1) // RET_DK

    def ret_body(c, _):
        sl = pl.ds(pl.multiple_of(c * CHUNK, CHUNK), CHUNK)
        cos = cos_ref[sl, :]
        sin = sin_ref[sl, :]
        q = _rotary(proj_ref[sl, Q0:K0], cos, sin, first_half)
        k = _rotary(proj_ref[sl, K0:V0], cos, sin, first_half) * (RET_DK ** -0.5)
        qb = q.astype(BF16)
        kb = k.astype(BF16)
        vb = proj_ref[sl, V0:G0].astype(BF16)
        q_heads = jnp.concatenate(
            [jnp.where(head_of_lane == h, qb, jnp.zeros_like(qb)) for h in range(RET_HEADS)], axis=0)
        scores = lax.dot_general(q_heads, kb, (((1,), (1,)), ((), ())), preferred_element_type=F32)
        p = (scores * dmask_ref[...]).astype(BF16)
        inner = jnp.concatenate(
            [jnp.dot(p[h * CHUNK:(h + 1) * CHUNK, :], vb[:, h * RET_DV:(h + 1) * RET_DV],
                     preferred_element_type=F32) for h in range(RET_HEADS)], axis=1)
        state = state_ref[...]
        cross = jnp.dot((q * xi_ref[...]).astype(BF16), state.astype(BF16), preferred_element_type=F32)
        kz = (k * zeta_ref[...]).astype(BF16)
        kv = lax.dot_general(kz, vb, (((0,), (0,)), ((), ())), preferred_element_type=F32)
        state_ref[...] = state * decay_ref[...] + kv * bd_ref[...]
        ret = inner + cross
        gate = proj_ref[sl, G0:U0]
        outs = []
        for h in range(RET_HEADS):
            o = ret[:, h * RET_DV:(h + 1) * RET_DV]
            mu = jnp.mean(o, axis=-1, keepdims=True)
            oc = o - mu
            var = jnp.mean(oc * oc, axis=-1, keepdims=True)
            outs.append(oc * lax.rsqrt(var + GN_EPS))
        gn = jnp.concatenate(outs, axis=1)
        mixin_ref[sl, 0:RET_WIDTH] = (gate * jax.nn.sigmoid(gate) * gn).astype(BF16)
        return 0
    lax.fori_loop(0, n_chunks, ret_body, 0, unroll=True)

    uext_ref[0:N_META, :] = tail_ref[...]
    uext_ref[N_META:, :] = proj_ref[:, U0:]
    tail_ref[...] = uext_ref[rows:, :]
    for g, w in enumerate(POOL_WINDOWS):
        lanes = slice(g * POOL_CH, (g + 1) * POOL_CH)
        e = uext_ref[:, lanes]
        acc = e
        shift = 1
        while shift < w:
            acc = acc + pltpu.roll(acc, shift, 0)
            shift *= 2
        pooled = acc[N_META:, :] * (1.0 / w) - e[N_META:, :]
        mixed = jnp.dot(pooled.astype(BF16), poolw_ref[g], preferred_element_type=F32) + poolb_ref[:, lanes]
        mixin_ref[:, RET_WIDTH + g * POOL_CH:RET_WIDTH + (g + 1) * POOL_CH] = (mixed * pools_ref[:, lanes]).astype(BF16)

    proj_ref[:, 0:D_MODEL] = jnp.dot(mixin_ref[...], wout_ref[...], preferred_element_type=F32)

    @pl.when(step >= 1)
    def _():
        scatter_wait(slot)

    @pl.when(step >= 2)
    def _():
        h1_writeback(slot, 0).wait()

    keep = hkeep_ref.at[slot]

    def ln1_body(c, _):
        sl = pl.ds(pl.multiple_of(c * CHUNK, CHUNK), CHUNK)
        h1 = _layer_norm(ALPHA * h0_ref[sl, :] + proj_ref[sl, 0:D_MODEL], ln1g_ref[...], ln1b_ref[...], LN_EPS)
        _store_token_tiles(keep, c * CHUNK, h1)
        h0b_ref[sl, :] = h1.astype(BF16)
        return 0
    lax.fori_loop(0, n_chunks, ln1_body, 0, unroll=True)
    h1_writeback(slot, step).start()

    logits = lax.dot_general(wr_ref[...], h0b_ref[...], (((1,), (1,)), ((), ())), preferred_element_type=F32)
    logits = logits + br_ref[...]
    gl = logits[0:N_GROUPS, :]
    gmax = jnp.max(gl, axis=0, keepdims=True)
    g_p = 1.0 / jnp.sum(jnp.exp(gl - gmax), axis=0, keepdims=True)
    grow = lax.broadcasted_iota(jnp.int32, gl.shape, 0)
    g_idx = jnp.min(jnp.where(gl == gmax, grow, N_GROUPS), axis=0, keepdims=True)
    sel = logits[8:8 + EXPERTS_PER_GROUP, :]
    for g in range(1, N_GROUPS):
        sel = jnp.where(g_idx == g, logits[8 + g * EXPERTS_PER_GROUP:8 + (g + 1) * EXPERTS_PER_GROUP, :], sel)
    erow = lax.broadcasted_iota(jnp.int32, sel.shape, 0)
    m1 = jnp.max(sel, axis=0, keepdims=True)
    i1 = jnp.min(jnp.where(sel == m1, erow, EXPERTS_PER_GROUP), axis=0, keepdims=True)
    sel2 = jnp.where(erow == i1, -jnp.inf, sel)
    m2 = jnp.max(sel2, axis=0, keepdims=True)
    i2 = jnp.min(jnp.where(sel2 == m2, erow, EXPERTS_PER_GROUP), axis=0, keepdims=True)
    e2 = jnp.exp(m2 - m1)
    w1 = 1.0 / (1.0 + e2)
    w2 = e2 / (1.0 + e2)
    pe0 = g_idx * EXPERTS_PER_GROUP + i1
    pe1 = g_idx * EXPERTS_PER_GROUP + i2
    pw_ref[...] = jnp.concatenate([g_p * w1, g_p * w2], axis=0)
    xrow = lax.broadcasted_iota(jnp.int32, (N_EXPERTS, rows), 0)
    oh0 = xrow == pe0
    oh1 = xrow == pe1
    oh = jnp.where(jnp.logical_or(oh0, oh1), 1.0, 0.0)
    carry = carry_ref[...]
    count = carry[:, 0:1]
    prefix = jnp.dot(oh.astype(BF16), tri_ref[...], preferred_element_type=F32) + count

    inv_rows = 1.0 / EXPERT_ROWS
    added = jnp.sum(oh, axis=1, keepdims=True)
    blocks_old = jnp.floor((count + (EXPERT_ROWS - 1)) * inv_rows)
    blocks_new = jnp.floor((count + added + (EXPERT_ROWS - 1)) * inv_rows)
    need = blocks_new - blocks_old
    er = lax.broadcasted_iota(jnp.int32, (N_EXPERTS, N_EXPERTS), 0)
    ec = lax.broadcasted_iota(jnp.int32, (N_EXPERTS, N_EXPERTS), 1)
    earlier = jnp.where(ec < er, 1.0, 0.0)
    need_lanes = jnp.broadcast_to(need, (N_EXPERTS, LANES))
    before = jnp.dot(earlier.astype(BF16), need_lanes.astype(BF16), preferred_element_type=F32)[:, 0:1]
    nalloc = nalloc_ref[0:1, 0:1]
    new_blk = nalloc + before
    cur_blk = curblk_ref[:, 0:1]
    ordinal = jnp.floor(prefix * inv_rows)
    blk = jnp.where(ordinal < blocks_old, cur_blk, new_blk)
    row_in_buf = blk * EXPERT_ROWS + (prefix - ordinal * EXPERT_ROWS)
    dest0 = jnp.sum(jnp.where(oh0, row_in_buf, 0.0), axis=0, keepdims=True)
    dest1 = jnp.sum(jnp.where(oh1, row_in_buf, 0.0), axis=0, keepdims=True)
    dest = jnp.concatenate([dest0, dest1], axis=0).astype(jnp.int32)
    dest_ref[...] = dest
    dvm_ref[...] = dest
    pltpu.make_async_copy(dvm_ref, dsm_ref, idx_sem).start()

    got_new = need > 0.0
    blk_lane = lax.broadcasted_iota(jnp.int32, (N_EXPERTS, BLOCK_TABLE_LANES), 1).astype(F32)
    erow_f = lax.broadcasted_iota(jnp.int32, (N_EXPERTS, BLOCK_TABLE_LANES), 0).astype(F32)
    owner = jnp.max(jnp.where(jnp.logical_and(got_new, blk_lane == new_blk), erow_f, -1.0), axis=0, keepdims=True)
    tab = jnp.where(owner >= 0.0, owner, tab_ref[0:1, :])
    tab_ref[...] = jnp.broadcast_to(tab, tab_ref.shape)
    cur_blk = jnp.where(got_new, new_blk, cur_blk)
    curblk_ref[...] = jnp.broadcast_to(cur_blk, curblk_ref.shape)
    nalloc_ref[...] = jnp.broadcast_to(nalloc + jnp.sum(need, axis=0, keepdims=True), nalloc_ref.shape)
    carry = carry + added
    carry_ref[...] = carry
    cnt_ref[...] = carry.astype(jnp.int32)
    cur_ref[...] = jnp.broadcast_to(cur_blk, cur_ref.shape).astype(jnp.int32)
    table_ref[...] = jnp.broadcast_to(tab, table_ref.shape).astype(jnp.int32)

    @pl.when(step == n_steps - 1)
    def _():
        pltpu.make_async_copy(dvm_ref, dsm_ref, idx_sem).wait()

        def last_rows(t, _):
            for s in range(TOP_K):
                pltpu.make_async_copy(_token_rows(keep, t), _token_rows(buf_ref, dsm_ref[s, t]), scat_sem.at[slot]).start()
            return 0
        lax.fori_loop(0, rows, last_rows, 0)

        csm_cnt = pltpu.make_async_copy(cnt_ref, csm_ref.at[0], zsem)
        csm_cur = pltpu.make_async_copy(cur_ref, csm_ref.at[1], zsem)
        csm_cnt.start()
        csm_cur.start()
        csm_cnt.wait()
        csm_cur.wait()

        def tail_fill(e, do_start):
            cnt_e = csm_ref[0, e, 0]
            gap = (EXPERT_ROWS - lax.rem(cnt_e, EXPERT_ROWS)) % EXPERT_ROWS
            first = csm_ref[1, e, 0] * EXPERT_ROWS + (EXPERT_ROWS - gap)
            for bit in range(EXPERT_ROWS.bit_length() - 1):
                run = 1 << bit

                @pl.when((gap >> bit) & 1 == 1)
                def _():
                    cp = pltpu.make_async_copy(_token_rows(zero_ref, 0, run),
                                               _token_rows(buf_ref, first + (gap & (run - 1)), run), zsem)
                    cp.start() if do_start else cp.wait()

        def unused_fill(b, do_start):
            cp = pltpu.make_async_copy(zero_ref, _token_rows(buf_ref, b * EXPERT_ROWS, EXPERT_ROWS), zsem)
            cp.start() if do_start else cp.wait()

        total = jnp.int32(0)
        for e in range(N_EXPERTS):
            total = total + (csm_ref[0, e, 0] + (EXPERT_ROWS - 1)) // EXPERT_ROWS
        for do_start in (True, False):
            lax.fori_loop(0, N_EXPERTS, lambda e, c, d=do_start: (tail_fill(e, d), c)[1], 0)
            lax.fori_loop(total, n_blocks, lambda b, c, d=do_start: (unused_fill(b, d), c)[1], 0)

        scatter_wait(prev)
        scatter_wait(slot)
        h1_writeback(slot, 0).wait()

        @pl.when(n_steps >= 2)
        def _():
            h1_writeback(prev, 0).wait()


def _expert_kernel(be_ref, nv_ref, bi_ref, bo_ref, x_ref, wg_ref, wu_ref, wd_ref, y_ref, wgb_ref, wub_ref, wdb_ref):
    i = pl.program_id(0)
    nv = nv_ref[i]
    new_expert = jnp.logical_or(i == 0, be_ref[i] != be_ref[jnp.maximum(i - 1, 0)])

    @pl.when(new_expert)
    def _():
        wgb_ref[...] = wg_ref[...].astype(BF16)
        wub_ref[...] = wu_ref[...].astype(BF16)
        wdb_ref[...] = wd_ref[...].astype(BF16)

    @pl.when(nv == 0)
    def _():
        y_ref[...] = jnp.zeros_like(y_ref)

    @pl.when(nv > 0)
    def _():
        x = _load_token_tiles(x_ref, 0, EXPERT_ROWS).astype(BF16)
        gate = jnp.dot(x, wgb_ref[...], preferred_element_type=F32)
        up = jnp.dot(x, wub_ref[...], preferred_element_type=F32)
        act = (gate * jax.nn.sigmoid(gate) * up).astype(BF16)
        _store_token_tiles(y_ref, 0, jnp.dot(act, wdb_ref[...], preferred_element_type=F32))


def _combine_kernel(dcur_ref, dnext_ref, h_ref, pw_ref, g_ref, b_ref, y_ref, o_ref, yb0, yb1, yb2, yb3, sems):
    part = COMBINE_PART_ROWS
    bufs = (yb0, yb1, yb2, yb3)
    step = pl.program_id(0)
    last = pl.num_programs(0) - 1

    def copy(src_tok, ybuf, t, slot, sem):
        return pltpu.make_async_copy(_token_rows(y_ref, src_tok), _token_rows(ybuf.at[slot], t), sem)

    def issue(dref, col0, ybuf, sem):
        for t0 in range(0, part, INDEX_GROUP):
            src = [[dref[slot, col0 + t0 + k] for slot in range(TOP_K)] for k in range(INDEX_GROUP)]
            for k in range(INDEX_GROUP):
                for slot in range(TOP_K):
                    copy(src[k][slot], ybuf, t0 + k, slot, sem).start(priority=slot % 2)

    def wait(ybuf, sem):
        for slot in range(TOP_K):
            pltpu.make_async_copy(_token_rows(y_ref, 0, part), ybuf.at[slot], sem).wait()

    def finish(ybuf, r0):
        rows = pl.ds(r0, part)
        pw = pw_ref[rows, :]
        y = pw[:, 0:1] * _load_token_tiles(ybuf.at[0], 0, part) + pw[:, 1:2] * _load_token_tiles(ybuf.at[1], 0, part)
        h = _load_token_tiles(h_ref, r0, part)
        o_ref[rows, :] = _layer_norm(ALPHA * h + y, g_ref[...], b_ref[...], LN_EPS)

    @pl.when(step == 0)
    def _():
        for p in range(2):
            def first(t, _, p=p):
                for slot in range(TOP_K):
                    copy(dcur_ref[slot, p * part + t], bufs[p], t, slot, sems.at[p]).start()
                return 0
            lax.fori_loop(0, part, first, 0)

    for p in range(COMBINE_PHASES):
        ahead = (p + 2) % COMBINE_PHASES
        wait(bufs[p], sems.at[p])
        if p + 2 < COMBINE_PHASES:
            issue(dcur_ref, (p + 2) * part, bufs[ahead], sems.at[ahead])
        else:
            issue(dnext_ref, ahead * part, bufs[ahead], sems.at[ahead])
        finish(bufs[p], p * part)

    @pl.when(step == last)
    def _():
        wait(bufs[0], sems.at[0])
        wait(bufs[1], sems.at[1])


def _tables(seq):
    log_g = jnp.log1p(-jnp.power(2.0, -5.0 - jnp.arange(RET_HEADS, dtype=F32)))
    i = jnp.arange(CHUNK, dtype=F32)
    rel = i[:, None] - i[None, :]
    dmask = jnp.where(rel[None] >= 0, jnp.exp(jnp.maximum(rel, 0.0)[None] * log_g[:, None, None]), 0.0)
    dmask = dmask.reshape(RET_HEADS * CHUNK, CHUNK)
    lg_lane = jnp.repeat(log_g, RET_DK)
    xi = jnp.exp((i + 1)[:, None] * lg_lane[None, :])
    zeta = jnp.exp((CHUNK - 1 - i)[:, None] * lg_lane[None, :])
    decay = jnp.broadcast_to(jnp.exp(CHUNK * lg_lane)[:, None], (RET_QK_WIDTH, RET_WIDTH))
    bd = (jnp.arange(RET_QK_WIDTH)[:, None] // RET_DK == jnp.arange(RET_WIDTH)[None, :] // RET_DV).astype(F32)
    zeta_meta = zeta[CHUNK - N_META:, :]
    half = RET_DK // 2
    inv = ROPE_BASE ** (-jnp.arange(half, dtype=F32) / half)
    pos = jnp.arange(N_META + seq, dtype=F32)
    ang = pos[:, None] * inv[None, :]
    cos = jnp.tile(jnp.cos(ang), (1, 2 * RET_HEADS))
    sin = jnp.tile(jnp.concatenate([-jnp.sin(ang), jnp.sin(ang)], axis=1), (1, RET_HEADS))
    return dict(dmask=dmask, xi=xi, zeta=zeta, decay=decay, bd=bd, zeta_meta=zeta_meta,
                cos_meta=cos[:N_META], sin_meta=sin[:N_META], cos=cos[N_META:], sin=sin[N_META:])


def _full(shape):
    return pl.BlockSpec(shape, lambda *_: (0,) * len(shape))


def kernel(x, meta_tokens, ln_emb_g, ln_emb_b, w_in, pool_w, pool_b, pool_scale, w_out, ln1_g, ln1_b, router_group_w, router_group_b, router_expert_w, router_expert_b, expert_w_gate, expert_w_up, expert_w_down, ln2_g, ln2_b):
    batch, seq, d = x.shape
    assert d == D_MODEL and seq % MIX_ROWS == 0 and MIX_ROWS <= EXPERT_ROWS
    n_tok = batch * seq
    t_blocks = seq // MIX_ROWS
    tb = _tables(seq)

    row = lambda a: a.reshape(1, -1).astype(F32)
    win_b = w_in[0].astype(BF16)
    wout_b = w_out[0].astype(BF16)
    poolw_b = pool_w[0].astype(BF16)
    wr = jnp.zeros((ROUTER_ROWS, D_MODEL), F32)
    wr = wr.at[0:N_GROUPS].set(router_group_w[0].T).at[8:8 + N_EXPERTS].set(router_expert_w[0].T).astype(BF16)
    br = jnp.zeros((ROUTER_ROWS, 1), F32)
    br = br.at[0:N_GROUPS, 0].set(router_group_b[0]).at[8:8 + N_EXPERTS, 0].set(router_expert_b[0])

    s0, tail0 = pl.pallas_call(
        _meta_kernel,
        out_shape=(jax.ShapeDtypeStruct((RET_QK_WIDTH, RET_WIDTH), F32), jax.ShapeDtypeStruct((N_META, POOL_WIDTH), F32)),
        name="meta_prep",
    )(meta_tokens.astype(F32), row(ln_emb_g), row(ln_emb_b), win_b, tb["cos_meta"], tb["sin_meta"], tb["zeta_meta"], tb["bd"])

    tok_spec = pl.BlockSpec((None, MIX_ROWS, D_MODEL), lambda b, j: (b, j, 0))
    pair_spec = pl.BlockSpec((TOP_K, MIX_ROWS), lambda b, j: (0, b * t_blocks + j))
    rope_spec = pl.BlockSpec((MIX_ROWS, RET_QK_WIDTH), lambda b, j: (j, 0))
    n_blocks = (n_tok * TOP_K) // EXPERT_ROWS + N_EXPERTS
    assert n_blocks + 2 <= BLOCK_TABLE_LANES
    tile_block = (EXPERT_ROWS * TOKEN_SUBLANES, LANES)
    sorted_shape = (n_blocks * EXPERT_ROWS * TOKEN_SUBLANES, LANES)
    buf_shape = ((n_blocks + 2) * EXPERT_ROWS * TOKEN_SUBLANES, LANES)
    stage_block = (MIX_ROWS * TOKEN_SUBLANES, LANES)
    h1, dest, pair_w, counts, cur_blk, blk_table, buf = pl.pallas_call(
        _mixer_kernel,
        grid=(batch, t_blocks),
        in_specs=[tok_spec, _full((1, D_MODEL)), _full((1, D_MODEL)), _full((D_MODEL, IN_COLS)), _full((D_MODEL, D_MODEL)),
                  _full((POOL_GROUPS, POOL_CH, POOL_CH)), _full((1, POOL_WIDTH)), _full((1, POOL_WIDTH)),
                  _full((1, D_MODEL)), _full((1, D_MODEL)), rope_spec, rope_spec,
                  _full((RET_HEADS * CHUNK, CHUNK)), _full((CHUNK, RET_QK_WIDTH)), _full((CHUNK, RET_QK_WIDTH)),
                  _full((RET_QK_WIDTH, RET_WIDTH)), _full((RET_QK_WIDTH, RET_WIDTH)),
                  _full((RET_QK_WIDTH, RET_WIDTH)), _full((N_META, POOL_WIDTH)),
                  _full((ROUTER_ROWS, D_MODEL)), _full((ROUTER_ROWS, 1))],
        out_specs=[pl.BlockSpec(memory_space=pl.ANY), pair_spec, pair_spec,
                   _full((N_EXPERTS, LANES)), _full((N_EXPERTS, LANES)), _full((8, BLOCK_TABLE_LANES)),
                   pl.BlockSpec(memory_space=pl.ANY)],
        out_shape=[jax.ShapeDtypeStruct((n_tok * TOKEN_SUBLANES, LANES), F32),
                   jax.ShapeDtypeStruct((TOP_K, n_tok), jnp.int32),
                   jax.ShapeDtypeStruct((TOP_K, n_tok), F32),
                   jax.ShapeDtypeStruct((N_EXPERTS, LANES), jnp.int32),
                   jax.ShapeDtypeStruct((N_EXPERTS, LANES), jnp.int32),
                   jax.ShapeDtypeStruct((8, BLOCK_TABLE_LANES), jnp.int32),
                   jax.ShapeDtypeStruct(buf_shape, F32)],
        scratch_shapes=[pltpu.VMEM((RET_QK_WIDTH, RET_WIDTH), F32),
                        pltpu.VMEM((N_META, POOL_WIDTH), F32),
                        pltpu.VMEM((MIX_ROWS + N_META, POOL_WIDTH), F32),
                        pltpu.VMEM((MIX_ROWS, IN_COLS), F32),
                        pltpu.VMEM((MIX_ROWS, D_MODEL), F32),
                        pltpu.VMEM((MIX_ROWS, D_MODEL), BF16),
                        pltpu.VMEM((MIX_ROWS, D_MODEL), BF16),
                        pltpu.VMEM((MIX_ROWS, MIX_ROWS), BF16),
                        pltpu.VMEM((N_EXPERTS, LANES), F32),
                        pltpu.VMEM((N_EXPERTS, LANES), F32),
                        pltpu.VMEM((8, LANES), F32),
                        pltpu.VMEM((8, BLOCK_TABLE_LANES), F32),
                        pltpu.VMEM((2,) + stage_block, F32),
                        pltpu.VMEM(tile_block, F32),
                        pltpu.VMEM((TOP_K, MIX_ROWS), jnp.int32),
                        pltpu.SMEM((TOP_K, MIX_ROWS), jnp.int32),
                        pltpu.SMEM((2, N_EXPERTS, LANES), jnp.int32),
                        pltpu.SemaphoreType.DMA((2,)), pltpu.SemaphoreType.DMA((2,)),
                        pltpu.SemaphoreType.DMA, pltpu.SemaphoreType.DMA],
        compiler_params=pltpu.CompilerParams(dimension_semantics=("arbitrary", "arbitrary"),
                                             vmem_limit_bytes=VMEM_LIMIT),
        name="mixer",
    )(x, row(ln_emb_g), row(ln_emb_b), win_b, wout_b, poolw_b, row(pool_b[0]), row(pool_scale[0]),
      row(ln1_g[0]), row(ln1_b[0]), tb["cos"], tb["sin"], tb["dmask"], tb["xi"], tb["zeta"], tb["decay"], tb["bd"],
      s0, tail0, wr, br)

    del counts, cur_blk
    bidx = jnp.arange(n_blocks, dtype=jnp.int32)
    owner = blk_table[0, :n_blocks]
    key = owner * BLOCK_TABLE_LANES + bidx
    place = jnp.sum(key[None, :] < key[:, None], axis=1)
    order = jnp.sum(jnp.where(place[None, :] == bidx[:, None], bidx[None, :], 0), axis=1).astype(jnp.int32)
    owner_in_order = jnp.sum(jnp.where(place[None, :] == bidx[:, None], owner[None, :], 0), axis=1)
    n_used = jnp.sum(owner < N_EXPERTS)
    used = bidx < n_used
    last_used = jnp.maximum(n_used - 1, 0)
    blk_in = jnp.where(used, order, order[last_used]).astype(jnp.int32)
    blk_e = jnp.where(used, owner_in_order, owner_in_order[last_used]).astype(jnp.int32)
    blk_nv = used.astype(jnp.int32)

    y_sorted = pl.pallas_call(
        _expert_kernel,
        grid_spec=pltpu.PrefetchScalarGridSpec(
            num_scalar_prefetch=4,
            grid=(n_blocks,),
            in_specs=[pl.BlockSpec(tile_block, lambda i, be, nv, bi, bo: (bi[i], 0)),
                      pl.BlockSpec((None, D_MODEL, D_EXPERT), lambda i, be, nv, bi, bo: (be[i], 0, 0)),
                      pl.BlockSpec((None, D_MODEL, D_EXPERT), lambda i, be, nv, bi, bo: (be[i], 0, 0)),
                      pl.BlockSpec((None, D_EXPERT, D_MODEL), lambda i, be, nv, bi, bo: (be[i], 0, 0))],
            out_specs=pl.BlockSpec(tile_block, lambda i, be, nv, bi, bo: (bo[i], 0)),
            scratch_shapes=[pltpu.VMEM((D_MODEL, D_EXPERT), BF16), pltpu.VMEM((D_MODEL, D_EXPERT), BF16),
                            pltpu.VMEM((D_EXPERT, D_MODEL), BF16)],
        ),
        out_shape=jax.ShapeDtypeStruct(sorted_shape, F32),
        compiler_params=pltpu.CompilerParams(dimension_semantics=("arbitrary",), vmem_limit_bytes=VMEM_LIMIT),
        name="experts",
    )(blk_e, blk_nv, blk_in, order, buf, expert_w_gate[0], expert_w_up[0], expert_w_down[0])

    comb_rows = COMBINE_PHASES * COMBINE_PART_ROWS
    comb_steps = n_tok // comb_rows
    assert n_tok % comb_rows == 0
    rows_spec = pl.BlockSpec((comb_rows, D_MODEL), lambda i: (i, 0))
    out = pl.pallas_call(
        _combine_kernel,
        grid=(comb_steps,),
        in_specs=[pl.BlockSpec((TOP_K, comb_rows), lambda i: (0, i), memory_space=pltpu.SMEM),
                  pl.BlockSpec((TOP_K, comb_rows), lambda i: (0, jnp.minimum(i + 1, comb_steps - 1)),
                               memory_space=pltpu.SMEM),
                  pl.BlockSpec((comb_rows * TOKEN_SUBLANES, LANES), lambda i: (i, 0)),
                  pl.BlockSpec((comb_rows, TOP_K), lambda i: (i, 0)),
                  pl.BlockSpec((1, D_MODEL), lambda i: (0, 0)), pl.BlockSpec((1, D_MODEL), lambda i: (0, 0)),
                  pl.BlockSpec(memory_space=pl.ANY)],
        out_specs=rows_spec,
        out_shape=jax.ShapeDtypeStruct((n_tok, D_MODEL), F32),
        scratch_shapes=[pltpu.VMEM((TOP_K, COMBINE_PART_ROWS * TOKEN_SUBLANES, LANES), F32)] * COMBINE_PHASES
                       + [pltpu.SemaphoreType.DMA((COMBINE_PHASES,))],
        compiler_params=pltpu.CompilerParams(dimension_semantics=("arbitrary",), vmem_limit_bytes=VMEM_LIMIT),
        name="combine",
    )(dest, dest, h1, pair_w.T, row(ln2_g[0]), row(ln2_b[0]), y_sorted)
    return out.reshape(batch, seq, D_MODEL)
```

```python
import functools

import jax
import jax.numpy as jnp
from jax import lax
from jax.experimental import pallas as pl
from jax.experimental.pallas import tpu as pltpu

D_MODEL = 1024
DEPTH = 1
N_META = 16
RET_HEADS = 4
RET_WIDTH = D_MODEL // 2
RET_DV = RET_WIDTH // RET_HEADS
RET_DK = RET_DV // 2
RET_QK_WIDTH = RET_HEADS * RET_DK
CHUNK = 128
ROPE_BASE = 10000.0
POOL_WINDOWS = (2, 4, 8, 16)
POOL_GROUPS = len(POOL_WINDOWS)
POOL_WIDTH = D_MODEL // 2
POOL_CH = POOL_WIDTH // POOL_GROUPS
IN_COLS = 2 * RET_QK_WIDTH + 2 * RET_WIDTH + POOL_WIDTH
N_GROUPS = 4
EXPERTS_PER_GROUP = 8
N_EXPERTS = N_GROUPS * EXPERTS_PER_GROUP
D_EXPERT = D_MODEL // 2
TOP_K = 2
LN_EPS = 1e-5
GN_EPS = 1e-6
ALPHA = (2 * DEPTH) ** 0.25

Q0, K0, V0, G0, U0 = 0, RET_QK_WIDTH, 2 * RET_QK_WIDTH, 2 * RET_QK_WIDTH + RET_WIDTH, 2 * RET_QK_WIDTH + 2 * RET_WIDTH

MIX_ROWS = 512
EXPERT_ROWS = 512
ROUTER_ROWS = 40
BLOCK_TABLE_LANES = 384
COMBINE_PHASES = 4
COMBINE_PART_ROWS = 128
INDEX_GROUP = 8
VMEM_LIMIT = 56 * 1024 * 1024
LANES = 128
TOKEN_SUBLANES = D_MODEL // LANES
PACKED_SUBLANES = TOKEN_SUBLANES // 2

F32 = jnp.float32
BF16 = jnp.bfloat16


def _layer_norm(x, g, b, eps):
    mu = jnp.mean(x, axis=-1, keepdims=True)
    xc = x - mu
    var = jnp.mean(xc * xc, axis=-1, keepdims=True)
    return xc * lax.rsqrt(var + eps) * g + b


def _rotary(z, cos, sin_signed, first_half):
    partner = jnp.where(first_half, pltpu.roll(z, RET_QK_WIDTH - RET_DK // 2, 1), pltpu.roll(z, RET_DK // 2, 1))
    return z * cos + partner * sin_signed


def _store_token_tiles(ref, tok0, x):
    n = x.shape[0]
    for s in range(TOKEN_SUBLANES):
        ref[pl.ds(tok0 * TOKEN_SUBLANES + s, n, stride=TOKEN_SUBLANES), :] = x[:, s * LANES:(s + 1) * LANES]


def _load_token_tiles(ref, tok0, n):
    return jnp.concatenate(
        [ref[pl.ds(tok0 * TOKEN_SUBLANES + s, n, stride=TOKEN_SUBLANES), :] for s in range(TOKEN_SUBLANES)], axis=1)


def _token_rows(ref, tok, n=1):
    return ref.at[pl.ds(pl.multiple_of(tok * TOKEN_SUBLANES, TOKEN_SUBLANES), n * TOKEN_SUBLANES), :]


def _store_packed_tokens(ref, tok0, x):
    n, half = x.shape[0], D_MODEL // 2
    lo = lax.bitcast_convert_type(x[:, :half].astype(BF16).astype(F32), jnp.uint32) >> 16
    hi = lax.bitcast_convert_type(x[:, half:].astype(BF16).astype(F32), jnp.uint32) & jnp.uint32(0xFFFF0000)
    words = lo | hi
    for s in range(PACKED_SUBLANES):
        ref[pl.ds(tok0 * PACKED_SUBLANES + s, n, stride=PACKED_SUBLANES), :] = words[:, s * LANES:(s + 1) * LANES]


def _load_packed_tokens(ref, tok0, n):
    words = [ref[pl.ds(tok0 * PACKED_SUBLANES + s, n, stride=PACKED_SUBLANES), :] for s in range(PACKED_SUBLANES)]
    lo = [lax.bitcast_convert_type(w << 16, F32).astype(BF16) for w in words]
    hi = [lax.bitcast_convert_type(w & jnp.uint32(0xFFFF0000), F32).astype(BF16) for w in words]
    return jnp.concatenate(lo + hi, axis=1)


def _packed_rows(ref, tok, n=1):
    return ref.at[pl.ds(pl.multiple_of(tok * PACKED_SUBLANES, PACKED_SUBLANES), n * PACKED_SUBLANES), :]


def _first_half_mask(rows):
    lane = lax.broadcasted_iota(jnp.int32, (rows, RET_QK_WIDTH), 1)
    return (lane % RET_DK) < (RET_DK // 2)


def _meta_kernel(meta_ref, g_ref, b_ref, win_ref, cos_ref, sin_ref, zeta_ref, bd_ref, s0_ref, tail_ref):
    h = _layer_norm(meta_ref[...], g_ref[...], b_ref[...], LN_EPS)
    proj = jnp.dot(h.astype(BF16), win_ref[...], preferred_element_type=F32)
    k = _rotary(proj[:, K0:V0], cos_ref[...], sin_ref[...], _first_half_mask(N_META)) * (RET_DK ** -0.5)
    kz = (k * zeta_ref[...]).astype(BF16)
    v = proj[:, V0:G0].astype(BF16)
    kv = lax.dot_general(kz, v, (((0,), (0,)), ((), ())), preferred_element_type=F32)
    s0_ref[...] = kv * bd_ref[...]
    tail_ref[...] = proj[:, U0:]


def _mixer_kernel(x_ref, lng_ref, lnb_ref, win_ref, wout_ref, poolw_ref, poolb_ref, pools_ref, ln1g_ref, ln1b_ref,
                  cos_ref, sin_ref, dmask_ref, xi_ref, zeta_ref, decay_ref, bd_ref, s0_ref, tail0_ref,
                  wr_ref, br_ref,
                  h1_ref, dest_ref, pw_ref, cnt_ref, cur_ref, table_ref, buf_ref,
                  state_ref, tail_ref, uext_ref, proj_ref, h0_ref, h0b_ref, mixin_ref, tri_ref, carry_ref,
                  curblk_ref, nalloc_ref, tab_ref, hkeep_ref, zero_ref, dvm_ref, dsm_ref, csm_ref,
                  scat_sem, idx_sem, zsem):
    rows = x_ref.shape[0]
    n_chunks = rows // CHUNK
    step = pl.program_id(0) * pl.num_programs(1) + pl.program_id(1)
    n_steps = pl.num_programs(0) * pl.num_programs(1)
    first_step = step == 0
    slot = lax.rem(step, 2)
    prev = 1 - slot
    n_blocks = buf_ref.shape[0] // (EXPERT_ROWS * PACKED_SUBLANES) - 2

    def scatter_wait(b):
        for _ in range(TOP_K):
            pltpu.make_async_copy(hkeep_ref.at[b], _packed_rows(buf_ref, 0, rows), scat_sem.at[b]).wait()

    @pl.when(first_step)
    def _():
        r = lax.broadcasted_iota(jnp.int32, (rows, rows), 0)
        c = lax.broadcasted_iota(jnp.int32, (rows, rows), 1)
        tri_ref[...] = jnp.where(r < c, 1.0, 0.0).astype(BF16)
        carry_ref[...] = jnp.zeros_like(carry_ref)
        curblk_ref[...] = jnp.full(curblk_ref.shape, -1.0, F32)
        nalloc_ref[...] = jnp.zeros_like(nalloc_ref)
        tab_ref[...] = jnp.full(tab_ref.shape, float(N_EXPERTS), F32)
        zero_ref[...] = jnp.zeros_like(zero_ref)
        hkeep_ref[1] = jnp.zeros(hkeep_ref.shape[1:], hkeep_ref.dtype)

        def spare(t, _):
            for s in range(TOP_K):
                dsm_ref[s, t] = (n_blocks + s) * EXPERT_ROWS + t
            return 0
        lax.fori_loop(0, rows, spare, 0)

    @pl.when(pl.program_id(1) == 0)
    def _():
        state_ref[...] = s0_ref[...]
        tail_ref[...] = tail0_ref[...]

    @pl.when(step > 0)
    def _():
        pltpu.make_async_copy(dvm_ref, dsm_ref, idx_sem).wait()

    for t0 in range(0, rows, INDEX_GROUP):
        dst = [[dsm_ref[s, t0 + k] for s in range(TOP_K)] for k in range(INDEX_GROUP)]
        for k in range(INDEX_GROUP):
            for s in range(TOP_K):
                pltpu.make_async_copy(_packed_rows(hkeep_ref.at[prev], t0 + k), _packed_rows(buf_ref, dst[k][s]),
                                      scat_sem.at[prev]).start(priority=s % 2)

    def ln_body(c, _):
        sl = pl.ds(pl.multiple_of(c * CHUNK, CHUNK), CHUNK)
        h0 = _layer_norm(x_ref[sl, :], lng_ref[...], lnb_ref[...], LN_EPS)
        h0_ref[sl, :] = h0
        h0b_ref[sl, :] = h0.astype(BF16)
        return 0
    lax.fori_loop(0, n_chunks, ln_body, 0, unroll=True)

    proj_ref[...] = jnp.dot(h0b_ref[...], win_ref[...], preferred_element_type=F32)

    first_half = _first_half_mask(CHUNK)
    head_of_lane = lax.broadcasted_iota(jnp.int32, (CHUNK, RET_QK_WIDTH), 1) // RET_DK

    def ret_body(c, _):
        sl = pl.ds(pl.multiple_of(c * CHUNK, CHUNK), CHUNK)
        cos = cos_ref[sl, :]
        sin = sin_ref[sl, :]
        q = _rotary(proj_ref[sl, Q0:K0], cos, sin, first_half)
        k = _rotary(proj_ref[sl, K0:V0], cos, sin, first_half) * (RET_DK ** -0.5)
        qb = q.astype(BF16)
        kb = k.astype(BF16)
        vb = proj_ref[sl, V0:G0].astype(BF16)
        q_heads = jnp.concatenate(
            [jnp.where(head_of_lane == h, qb, jnp.zeros_like(qb)) for h in range(RET_HEADS)], axis=0)
        scores = lax.dot_general(q_heads, kb, (((1,), (1,)), ((), ())), preferred_element_type=F32)
        p = (scores * dmask_ref[...]).astype(BF16)
        inner = jnp.concatenate(
            [jnp.dot(p[h * CHUNK:(h + 1) * CHUNK, :], vb[:, h * RET_DV:(h + 1) * RET_DV],
                     preferred_element_type=F32) for h in range(RET_HEADS)], axis=1)
        state = state_ref[...]
        cross = jnp.dot((q * xi_ref[...]).astype(BF16), state.astype(BF16), preferred_element_type=F32)
        kz = (k * zeta_ref[...]).astype(BF16)
        kv = lax.dot_general(kz, vb, (((0,), (0,)), ((), ())), preferred_element_type=F32)
        state_ref[...] = state * decay_ref[...] + kv * bd_ref[...]
        ret = inner + cross
        gate = proj_ref[sl, G0:U0]
        outs = []
        for h in range(RET_HEADS):
            o = ret[:, h * RET_DV:(h + 1) * RET_DV]
            mu = jnp.mean(o, axis=-1, keepdims=True)
            oc = o - mu
            var = jnp.mean(oc * oc, axis=-1, keepdims=True)
            outs.append(oc * lax.rsqrt(var + GN_EPS))
        gn = jnp.concatenate(outs, axis=1)
        mixin_ref[sl, 0:RET_WIDTH] = (gate * jax.nn.sigmoid(gate) * gn).astype(BF16)
        return 0
    lax.fori_loop(0, n_chunks, ret_body, 0, unroll=True)

    uext_ref[0:N_META, :] = tail_ref[...]
    uext_ref[N_META:, :] = proj_ref[:, U0:]
    tail_ref[...] = uext_ref[rows:, :]
    for g, w in enumerate(POOL_WINDOWS):
        lanes = slice(g * POOL_CH, (g + 1) * POOL_CH)
        e = uext_ref[:, lanes]
        acc = e
        shift = 1
        while shift < w:
            acc = acc + pltpu.roll(acc, shift, 0)
            shift *= 2
        pooled = acc[N_META:, :] * (1.0 / w) - e[N_META:, :]
        mixed = jnp.dot(pooled.astype(BF16), poolw_ref[g], preferred_element_type=F32) + poolb_ref[:, lanes]
        mixin_ref[:, RET_WIDTH + g * POOL_CH:RET_WIDTH + (g + 1) * POOL_CH] = (mixed * pools_ref[:, lanes]).astype(BF16)

    proj_ref[:, 0:D_MODEL] = jnp.dot(mixin_ref[...], wout_ref[...], preferred_element_type=F32)

    @pl.when(step >= 1)
    def _():
        scatter_wait(slot)

    keep = hkeep_ref.at[slot]

    def ln1_body(c, _):
        sl = pl.ds(pl.multiple_of(c * CHUNK, CHUNK), CHUNK)
        h1 = _layer_norm(ALPHA * h0_ref[sl, :] + proj_ref[sl, 0:D_MODEL], ln1g_ref[...], ln1b_ref[...], LN_EPS)
        h1_ref[sl, :] = h1
        _store_packed_tokens(keep, c * CHUNK, h1)
        h0b_ref[sl, :] = h1.astype(BF16)
        return 0
    lax.fori_loop(0, n_chunks, ln1_body, 0, unroll=True)

    logits = lax.dot_general(wr_ref[...], h0b_ref[...], (((1,), (1,)), ((), ())), preferred_element_type=F32)
    logits = logits + br_ref[...]
    gl = logits[0:N_GROUPS, :]
    gmax = jnp.max(gl, axis=0, keepdims=True)
    g_p = 1.0 / jnp.sum(jnp.exp(gl - gmax), axis=0, keepdims=True)
    grow = lax.broadcasted_iota(jnp.int32, gl.shape, 0)
    g_idx = jnp.min(jnp.where(gl == gmax, grow, N_GROUPS), axis=0, keepdims=True)
    sel = logits[8:8 + EXPERTS_PER_GROUP, :]
    for g in range(1, N_GROUPS):
        sel = jnp.where(g_idx == g, logits[8 + g * EXPERTS_PER_GROUP:8 + (g + 1) * EXPERTS_PER_GROUP, :], sel)
    erow = lax.broadcasted_iota(jnp.int32, sel.shape, 0)
    m1 = jnp.max(sel, axis=0, keepdims=True)
    i1 = jnp.min(jnp.where(sel == m1, erow, EXPERTS_PER_GROUP), axis=0, keepdims=True)
    sel2 = jnp.where(erow == i1, -jnp.inf, sel)
    m2 = jnp.max(sel2, axis=0, keepdims=True)
    i2 = jnp.min(jnp.where(sel2 == m2, erow, EXPERTS_PER_GROUP), axis=0, keepdims=True)
    e2 = jnp.exp(m2 - m1)
    w1 = 1.0 / (1.0 + e2)
    w2 = e2 / (1.0 + e2)
    pe0 = g_idx * EXPERTS_PER_GROUP + i1
    pe1 = g_idx * EXPERTS_PER_GROUP + i2
    pw_ref[...] = jnp.concatenate([g_p * w1, g_p * w2], axis=0)
    xrow = lax.broadcasted_iota(jnp.int32, (N_EXPERTS, rows), 0)
    oh0 = xrow == pe0
    oh1 = xrow == pe1
    oh = jnp.where(jnp.logical_or(oh0, oh1), 1.0, 0.0)
    carry = carry_ref[...]
    count = carry[:, 0:1]
    prefix = jnp.dot(oh.astype(BF16), tri_ref[...], preferred_element_type=F32) + count

    inv_rows = 1.0 / EXPERT_ROWS
    added = jnp.sum(oh, axis=1, keepdims=True)
    blocks_old = jnp.floor((count + (EXPERT_ROWS - 1)) * inv_rows)
    blocks_new = jnp.floor((count + added + (EXPERT_ROWS - 1)) * inv_rows)
    need = blocks_new - blocks_old
    er = lax.broadcasted_iota(jnp.int32, (N_EXPERTS, N_EXPERTS), 0)
    ec = lax.broadcasted_iota(jnp.int32, (N_EXPERTS, N_EXPERTS), 1)
    earlier = jnp.where(ec < er, 1.0, 0.0)
    need_lanes = jnp.broadcast_to(need, (N_EXPERTS, LANES))
    before = jnp.dot(earlier.astype(BF16), need_lanes.astype(BF16), preferred_element_type=F32)[:, 0:1]
    nalloc = nalloc_ref[0:1, 0:1]
    new_blk = nalloc + before
    cur_blk = curblk_ref[:, 0:1]
    ordinal = jnp.floor(prefix * inv_rows)
    blk = jnp.where(ordinal < blocks_old, cur_blk, new_blk)
    row_in_buf = blk * EXPERT_ROWS + (prefix - ordinal * EXPERT_ROWS)
    dest0 = jnp.sum(jnp.where(oh0, row_in_buf, 0.0), axis=0, keepdims=True)
    dest1 = jnp.sum(jnp.where(oh1, row_in_buf, 0.0), axis=0, keepdims=True)
    dest = jnp.concatenate([dest0, dest1], axis=0).astype(jnp.int32)
    dest_ref[...] = dest
    dvm_ref[...] = dest
    pltpu.make_async_copy(dvm_ref, dsm_ref, idx_sem).start()

    got_new = need > 0.0
    blk_lane = lax.broadcasted_iota(jnp.int32, (N_EXPERTS, BLOCK_TABLE_LANES), 1).astype(F32)
    erow_f = lax.broadcasted_iota(jnp.int32, (N_EXPERTS, BLOCK_TABLE_LANES), 0).astype(F32)
    owner = jnp.max(jnp.where(jnp.logical_and(got_new, blk_lane == new_blk), erow_f, -1.0), axis=0, keepdims=True)
    tab = jnp.where(owner >= 0.0, owner, tab_ref[0:1, :])
    tab_ref[...] = jnp.broadcast_to(tab, tab_ref.shape)
    cur_blk = jnp.where(got_new, new_blk, cur_blk)
    curblk_ref[...] = jnp.broadcast_to(cur_blk, curblk_ref.shape)
    nalloc_ref[...] = jnp.broadcast_to(nalloc + jnp.sum(need, axis=0, keepdims=True), nalloc_ref.shape)
    carry = carry + added
    carry_ref[...] = carry
    cnt_ref[...] = carry.astype(jnp.int32)
    cur_ref[...] = jnp.broadcast_to(cur_blk, cur_ref.shape).astype(jnp.int32)
    table_ref[...] = jnp.broadcast_to(tab, table_ref.shape).astype(jnp.int32)

    @pl.when(step == n_steps - 1)
    def _():
        pltpu.make_async_copy(dvm_ref, dsm_ref, idx_sem).wait()

        def last_rows(t, _):
            for s in range(TOP_K):
                pltpu.make_async_copy(_packed_rows(keep, t), _packed_rows(buf_ref, dsm_ref[s, t]), scat_sem.at[slot]).start()
            return 0
        lax.fori_loop(0, rows, last_rows, 0)

        csm_cnt = pltpu.make_async_copy(cnt_ref, csm_ref.at[0], zsem)
        csm_cur = pltpu.make_async_copy(cur_ref, csm_ref.at[1], zsem)
        csm_cnt.start()
        csm_cur.start()
        csm_cnt.wait()
        csm_cur.wait()

        def tail_fill(e, do_start):
            cnt_e = csm_ref[0, e, 0]
            gap = (EXPERT_ROWS - lax.rem(cnt_e, EXPERT_ROWS)) % EXPERT_ROWS
            first = csm_ref[1, e, 0] * EXPERT_ROWS + (EXPERT_ROWS - gap)
            for bit in range(EXPERT_ROWS.bit_length() - 1):
                run = 1 << bit

                @pl.when((gap >> bit) & 1 == 1)
                def _():
                    cp = pltpu.make_async_copy(_packed_rows(zero_ref, 0, run),
                                               _packed_rows(buf_ref, first + (gap & (run - 1)), run), zsem)
                    cp.start() if do_start else cp.wait()

        def unused_fill(b, do_start):
            cp = pltpu.make_async_copy(zero_ref, _packed_rows(buf_ref, b * EXPERT_ROWS, EXPERT_ROWS), zsem)
            cp.start() if do_start else cp.wait()

        total = jnp.int32(0)
        for e in range(N_EXPERTS):
            total = total + (csm_ref[0, e, 0] + (EXPERT_ROWS - 1)) // EXPERT_ROWS
        for do_start in (True, False):
            lax.fori_loop(0, N_EXPERTS, lambda e, c, d=do_start: (tail_fill(e, d), c)[1], 0)
            lax.fori_loop(total, n_blocks, lambda b, c, d=do_start: (unused_fill(b, d), c)[1], 0)

        scatter_wait(prev)
        scatter_wait(slot)


def _expert_kernel(be_ref, nv_ref, bi_ref, bo_ref, x_ref, wg_ref, wu_ref, wd_ref, y_ref, wgb_ref, wub_ref, wdb_ref):
    i = pl.program_id(0)
    nv = nv_ref[i]
    new_expert = jnp.logical_or(i == 0, be_ref[i] != be_ref[jnp.maximum(i - 1, 0)])

    @pl.when(new_expert)
    def _():
        wgb_ref[...] = wg_ref[...].astype(BF16)
        wub_ref[...] = wu_ref[...].astype(BF16)
        wdb_ref[...] = wd_ref[...].astype(BF16)

    @pl.when(nv == 0)
    def _():
        y_ref[...] = jnp.zeros_like(y_ref)

    @pl.when(nv > 0)
    def _():
        x = _load_packed_tokens(x_ref, 0, EXPERT_ROWS)
        gate = jnp.dot(x, wgb_ref[...], preferred_element_type=F32)
        up = jnp.dot(x, wub_ref[...], preferred_element_type=F32)
        act = (gate * jax.nn.sigmoid(gate) * up).astype(BF16)
        _store_token_tiles(y_ref, 0, jnp.dot(act, wdb_ref[...], preferred_element_type=F32))


def _combine_kernel(dcur_ref, dnext_ref, h_ref, pw_ref, g_ref, b_ref, y_ref, o_ref, yb0, yb1, yb2, yb3, sems):
    part = COMBINE_PART_ROWS
    bufs = (yb0, yb1, yb2, yb3)
    step = pl.program_id(0)
    last = pl.num_programs(0) - 1

    def copy(src_tok, ybuf, t, slot, sem):
        return pltpu.make_async_copy(_token_rows(y_ref, src_tok), _token_rows(ybuf.at[slot], t), sem)

    def issue(dref, col0, ybuf, sem):
        for t0 in range(0, part, INDEX_GROUP):
            src = [[dref[slot, col0 + t0 + k] for slot in range(TOP_K)] for k in range(INDEX_GROUP)]
            for k in range(INDEX_GROUP):
                for slot in range(TOP_K):
                    copy(src[k][slot], ybuf, t0 + k, slot, sem).start(priority=slot % 2)

    def wait(ybuf, sem):
        for slot in range(TOP_K):
            pltpu.make_async_copy(_token_rows(y_ref, 0, part), ybuf.at[slot], sem).wait()

    def finish(ybuf, r0):
        rows = pl.ds(r0, part)
        pw = pw_ref[rows, :]
        y = pw[:, 0:1] * _load_token_tiles(ybuf.at[0], 0, part) + pw[:, 1:2] * _load_token_tiles(ybuf.at[1], 0, part)
        o_ref[rows, :] = _layer_norm(ALPHA * h_ref[rows, :] + y, g_ref[...], b_ref[...], LN_EPS)

    @pl.when(step == 0)
    def _():
        for p in range(2):
            def first(t, _, p=p):
                for slot in range(TOP_K):
                    copy(dcur_ref[slot, p * part + t], bufs[p], t, slot, sems.at[p]).start()
                return 0
            lax.fori_loop(0, part, first, 0)

    for p in range(COMBINE_PHASES):
        ahead = (p + 2) % COMBINE_PHASES
        wait(bufs[p], sems.at[p])
        if p + 2 < COMBINE_PHASES:
            issue(dcur_ref, (p + 2) * part, bufs[ahead], sems.at[ahead])
        else:
            issue(dnext_ref, ahead * part, bufs[ahead], sems.at[ahead])
        finish(bufs[p], p * part)

    @pl.when(step == last)
    def _():
        wait(bufs[0], sems.at[0])
        wait(bufs[1], sems.at[1])


def _tables(seq):
    log_g = jnp.log1p(-jnp.power(2.0, -5.0 - jnp.arange(RET_HEADS, dtype=F32)))
    i = jnp.arange(CHUNK, dtype=F32)
    rel = i[:, None] - i[None, :]
    dmask = jnp.where(rel[None] >= 0, jnp.exp(jnp.maximum(rel, 0.0)[None] * log_g[:, None, None]), 0.0)
    dmask = dmask.reshape(RET_HEADS * CHUNK, CHUNK)
    lg_lane = jnp.repeat(log_g, RET_DK)
    xi = jnp.exp((i + 1)[:, None] * lg_lane[None, :])
    zeta = jnp.exp((CHUNK - 1 - i)[:, None] * lg_lane[None, :])
    decay = jnp.broadcast_to(jnp.exp(CHUNK * lg_lane)[:, None], (RET_QK_WIDTH, RET_WIDTH))
    bd = (jnp.arange(RET_QK_WIDTH)[:, None] // RET_DK == jnp.arange(RET_WIDTH)[None, :] // RET_DV).astype(F32)
    zeta_meta = zeta[CHUNK - N_META:, :]
    half = RET_DK // 2
    inv = ROPE_BASE ** (-jnp.arange(half, dtype=F32) / half)
    pos = jnp.arange(N_META + seq, dtype=F32)
    ang = pos[:, None] * inv[None, :]
    cos = jnp.tile(jnp.cos(ang), (1, 2 * RET_HEADS))
    sin = jnp.tile(jnp.concatenate([-jnp.sin(ang), jnp.sin(ang)], axis=1), (1, RET_HEADS))
    return dict(dmask=dmask, xi=xi, zeta=zeta, decay=decay, bd=bd, zeta_meta=zeta_meta,
                cos_meta=cos[:N_META], sin_meta=sin[:N_META], cos=cos[N_META:], sin=sin[N_META:])


def _full(shape):
    return pl.BlockSpec(shape, lambda *_: (0,) * len(shape))


def kernel(x, meta_tokens, ln_emb_g, ln_emb_b, w_in, pool_w, pool_b, pool_scale, w_out, ln1_g, ln1_b, router_group_w, router_group_b, router_expert_w, router_expert_b, expert_w_gate, expert_w_up, expert_w_down, ln2_g, ln2_b):
    batch, seq, d = x.shape
    assert d == D_MODEL and seq % MIX_ROWS == 0 and MIX_ROWS <= EXPERT_ROWS
    n_tok = batch * seq
    t_blocks = seq // MIX_ROWS
    tb = _tables(seq)

    row = lambda a: a.reshape(1, -1).astype(F32)
    win_b = w_in[0].astype(BF16)
    wout_b = w_out[0].astype(BF16)
    poolw_b = pool_w[0].astype(BF16)
    wr = jnp.zeros((ROUTER_ROWS, D_MODEL), F32)
    wr = wr.at[0:N_GROUPS].set(router_group_w[0].T).at[8:8 + N_EXPERTS].set(router_expert_w[0].T).astype(BF16)
    br = jnp.zeros((ROUTER_ROWS, 1), F32)
    br = br.at[0:N_GROUPS, 0].set(router_group_b[0]).at[8:8 + N_EXPERTS, 0].set(router_expert_b[0])

    s0, tail0 = pl.pallas_call(
        _meta_kernel,
        out_shape=(jax.ShapeDtypeStruct((RET_QK_WIDTH, RET_WIDTH), F32), jax.ShapeDtypeStruct((N_META, POOL_WIDTH), F32)),
        name="meta_prep",
    )(meta_tokens.astype(F32), row(ln_emb_g), row(ln_emb_b), win_b, tb["cos_meta"], tb["sin_meta"], tb["zeta_meta"], tb["bd"])

    tok_spec = pl.BlockSpec((None, MIX_ROWS, D_MODEL), lambda b, j: (b, j, 0))
    pair_spec = pl.BlockSpec((TOP_K, MIX_ROWS), lambda b, j: (0, b * t_blocks + j))
    rope_spec = pl.BlockSpec((MIX_ROWS, RET_QK_WIDTH), lambda b, j: (j, 0))
    n_blocks = (n_tok * TOP_K) // EXPERT_ROWS + N_EXPERTS
    assert n_blocks + 2 <= BLOCK_TABLE_LANES
    tile_block = (EXPERT_ROWS * TOKEN_SUBLANES, LANES)
    packed_block = (EXPERT_ROWS * PACKED_SUBLANES, LANES)
    sorted_shape = (n_blocks * EXPERT_ROWS * TOKEN_SUBLANES, LANES)
    buf_shape = ((n_blocks + 2) * EXPERT_ROWS * PACKED_SUBLANES, LANES)
    stage_block = (MIX_ROWS * PACKED_SUBLANES, LANES)
    h1, dest, pair_w, counts, cur_blk, blk_table, buf = pl.pallas_call(
        _mixer_kernel,
        grid=(batch, t_blocks),
        in_specs=[tok_spec, _full((1, D_MODEL)), _full((1, D_MODEL)), _full((D_MODEL, IN_COLS)), _full((D_MODEL, D_MODEL)),
                  _full((POOL_GROUPS, POOL_CH, POOL_CH)), _full((1, POOL_WIDTH)), _full((1, POOL_WIDTH)),
                  _full((1, D_MODEL)), _full((1, D_MODEL)), rope_spec, rope_spec,
                  _full((RET_HEADS * CHUNK, CHUNK)), _full((CHUNK, RET_QK_WIDTH)), _full((CHUNK, RET_QK_WIDTH)),
                  _full((RET_QK_WIDTH, RET_WIDTH)), _full((RET_QK_WIDTH, RET_WIDTH)),
                  _full((RET_QK_WIDTH, RET_WIDTH)), _full((N_META, POOL_WIDTH)),
                  _full((ROUTER_ROWS, D_MODEL)), _full((ROUTER_ROWS, 1))],
        out_specs=[tok_spec, pair_spec, pair_spec,
                   _full((N_EXPERTS, LANES)), _full((N_EXPERTS, LANES)), _full((8, BLOCK_TABLE_LANES)),
                   pl.BlockSpec(memory_space=pl.ANY)],
        out_shape=[jax.ShapeDtypeStruct((batch, seq, D_MODEL), F32),
                   jax.ShapeDtypeStruct((TOP_K, n_tok), jnp.int32),
                   jax.ShapeDtypeStruct((TOP_K, n_tok), F32),
                   jax.ShapeDtypeStruct((N_EXPERTS, LANES), jnp.int32),
                   jax.ShapeDtypeStruct((N_EXPERTS, LANES), jnp.int32),
                   jax.ShapeDtypeStruct((8, BLOCK_TABLE_LANES), jnp.int32),
                   jax.ShapeDtypeStruct(buf_shape, jnp.uint32)],
        scratch_shapes=[pltpu.VMEM((RET_QK_WIDTH, RET_WIDTH), F32),
                        pltpu.VMEM((N_META, POOL_WIDTH), F32),
                        pltpu.VMEM((MIX_ROWS + N_META, POOL_WIDTH), F32),
                        pltpu.VMEM((MIX_ROWS, IN_COLS), F32),
                        pltpu.VMEM((MIX_ROWS, D_MODEL), F32),
                        pltpu.VMEM((MIX_ROWS, D_MODEL), BF16),
                        pltpu.VMEM((MIX_ROWS, D_MODEL), BF16),
                        pltpu.VMEM((MIX_ROWS, MIX_ROWS), BF16),
                        pltpu.VMEM((N_EXPERTS, LANES), F32),
                        pltpu.VMEM((N_EXPERTS, LANES), F32),
                        pltpu.VMEM((8, LANES), F32),
                        pltpu.VMEM((8, BLOCK_TABLE_LANES), F32),
                        pltpu.VMEM((2,) + stage_block, jnp.uint32),
                        pltpu.VMEM(packed_block, jnp.uint32),
                        pltpu.VMEM((TOP_K, MIX_ROWS), jnp.int32),
                        pltpu.SMEM((TOP_K, MIX_ROWS), jnp.int32),
                        pltpu.SMEM((2, N_EXPERTS, LANES), jnp.int32),
                        pltpu.SemaphoreType.DMA((2,)),
                        pltpu.SemaphoreType.DMA, pltpu.SemaphoreType.DMA],
        compiler_params=pltpu.CompilerParams(dimension_semantics=("arbitrary", "arbitrary"),
                                             vmem_limit_bytes=VMEM_LIMIT),
        name="mixer",
    )(x, row(ln_emb_g), row(ln_emb_b), win_b, wout_b, poolw_b, row(pool_b[0]), row(pool_scale[0]),
      row(ln1_g[0]), row(ln1_b[0]), tb["cos"], tb["sin"], tb["dmask"], tb["xi"], tb["zeta"], tb["decay"], tb["bd"],
      s0, tail0, wr, br)

    del counts, cur_blk
    bidx = jnp.arange(n_blocks, dtype=jnp.int32)
    owner = blk_table[0, :n_blocks]
    key = owner * BLOCK_TABLE_LANES + bidx
    place = jnp.sum(key[None, :] < key[:, None], axis=1)
    order = jnp.sum(jnp.where(place[None, :] == bidx[:, None], bidx[None, :], 0), axis=1).astype(jnp.int32)
    owner_in_order = jnp.sum(jnp.where(place[None, :] == bidx[:, None], owner[None, :], 0), axis=1)
    n_used = jnp.sum(owner < N_EXPERTS)
    used = bidx < n_used
    last_used = jnp.maximum(n_used - 1, 0)
    blk_in = jnp.where(used, order, order[last_used]).astype(jnp.int32)
    blk_e = jnp.where(used, owner_in_order, owner_in_order[last_used]).astype(jnp.int32)
    blk_nv = used.astype(jnp.int32)

    y_sorted = pl.pallas_call(
        _expert_kernel,
        grid_spec=pltpu.PrefetchScalarGridSpec(
            num_scalar_prefetch=4,
            grid=(n_blocks,),
            in_specs=[pl.BlockSpec(packed_block, lambda i, be, nv, bi, bo: (bi[i], 0)),
                      pl.BlockSpec((None, D_MODEL, D_EXPERT), lambda i, be, nv, bi, bo: (be[i], 0, 0)),
                      pl.BlockSpec((None, D_MODEL, D_EXPERT), lambda i, be, nv, bi, bo: (be[i], 0, 0)),
                      pl.BlockSpec((None, D_EXPERT, D_MODEL), lambda i, be, nv, bi, bo: (be[i], 0, 0))],
            out_specs=pl.BlockSpec(tile_block, lambda i, be, nv, bi, bo: (bo[i], 0)),
            scratch_shapes=[pltpu.VMEM((D_MODEL, D_EXPERT), BF16), pltpu.VMEM((D_MODEL, D_EXPERT), BF16),
                            pltpu.VMEM((D_EXPERT, D_MODEL), BF16)],
        ),
        out_shape=jax.ShapeDtypeStruct(sorted_shape, F32),
        compiler_params=pltpu.CompilerParams(dimension_semantics=("arbitrary",), vmem_limit_bytes=VMEM_LIMIT),
        name="experts",
    )(blk_e, blk_nv, blk_in, order, buf, expert_w_gate[0], expert_w_up[0], expert_w_down[0])

    comb_rows = COMBINE_PHASES * COMBINE_PART_ROWS
    comb_steps = n_tok // comb_rows
    assert n_tok % comb_rows == 0
    rows_spec = pl.BlockSpec((comb_rows, D_MODEL), lambda i: (i, 0))
    out = pl.pallas_call(
        _combine_kernel,
        grid=(comb_steps,),
        in_specs=[pl.BlockSpec((TOP_K, comb_rows), lambda i: (0, i), memory_space=pltpu.SMEM),
                  pl.BlockSpec((TOP_K, comb_rows), lambda i: (0, jnp.minimum(i + 1, comb_steps - 1)),
                               memory_space=pltpu.SMEM),
                  rows_spec,
                  pl.BlockSpec((comb_rows, TOP_K), lambda i: (i, 0)),
                  pl.BlockSpec((1, D_MODEL), lambda i: (0, 0)), pl.BlockSpec((1, D_MODEL), lambda i: (0, 0)),
                  pl.BlockSpec(memory_space=pl.ANY)],
        out_specs=rows_spec,
        out_shape=jax.ShapeDtypeStruct((n_tok, D_MODEL), F32),
        scratch_shapes=[pltpu.VMEM((TOP_K, COMBINE_PART_ROWS * TOKEN_SUBLANES, LANES), F32)] * COMBINE_PHASES
                       + [pltpu.SemaphoreType.DMA((COMBINE_PHASES,))],
        compiler_params=pltpu.CompilerParams(dimension_semantics=("arbitrary",), vmem_limit_bytes=VMEM_LIMIT),
        name="combine",
    )(dest, dest, h1.reshape(n_tok, D_MODEL), pair_w.T, row(ln2_g[0]), row(ln2_b[0]), y_sorted)
    return out.reshape(batch, seq, D_MODEL)
```

```python
import jax
import jax.numpy as jnp
from jax import lax
from jax.experimental import pallas as pl
from jax.experimental.pallas import tpu as pltpu

D_MODEL = 1024
DEPTH = 1
N_META = 16
RET_HEADS = 4
RET_WIDTH = D_MODEL // 2
RET_DV = RET_WIDTH // RET_HEADS
RET_DK = RET_DV // 2
RET_QK_WIDTH = RET_HEADS * RET_DK
CHUNK = 128
ROPE_BASE = 10000.0
POOL_WINDOWS = (2, 4, 8, 16)
POOL_GROUPS = len(POOL_WINDOWS)
POOL_WIDTH = D_MODEL // 2
POOL_CH = POOL_WIDTH // POOL_GROUPS
IN_COLS = 2 * RET_QK_WIDTH + 2 * RET_WIDTH + POOL_WIDTH
N_GROUPS = 4
EXPERTS_PER_GROUP = 8
N_EXPERTS = N_GROUPS * EXPERTS_PER_GROUP
D_EXPERT = D_MODEL // 2
TOP_K = 2
LN_EPS = 1e-5
GN_EPS = 1e-6
ALPHA = (2 * DEPTH) ** 0.25

Q0, K0, V0, G0, U0 = 0, RET_QK_WIDTH, 2 * RET_QK_WIDTH, 2 * RET_QK_WIDTH + RET_WIDTH, 2 * RET_QK_WIDTH + 2 * RET_WIDTH

MIX_ROWS = 512
EXPERT_ROWS = 512
ROUTER_ROWS = 40
BLOCK_TABLE_LANES = 384
COMBINE_PHASES = 4
COMBINE_PART_ROWS = 128
INDEX_GROUP = 8
VMEM_LIMIT = 56 * 1024 * 1024
LANES = 128
TOKEN_SUBLANES = D_MODEL // LANES
PACKED_SUBLANES = TOKEN_SUBLANES // 2

F32 = jnp.float32
BF16 = jnp.bfloat16


def _layer_norm(x, g, b, eps):
    mu = jnp.mean(x, axis=-1, keepdims=True)
    xc = x - mu
    var = jnp.mean(xc * xc, axis=-1, keepdims=True)
    return xc * lax.rsqrt(var + eps) * g + b


def _rotary(z, cos, sin_signed, first_half):
    partner = jnp.where(first_half, pltpu.roll(z, RET_QK_WIDTH - RET_DK // 2, 1), pltpu.roll(z, RET_DK // 2, 1))
    return z * cos + partner * sin_signed


def _store_token_tiles(ref, tok0, x):
    n = x.shape[0]
    for s in range(TOKEN_SUBLANES):
        ref[pl.ds(tok0 * TOKEN_SUBLANES + s, n, stride=TOKEN_SUBLANES), :] = x[:, s * LANES:(s + 1) * LANES]


def _load_token_tiles(ref, tok0, n):
    return jnp.concatenate(
        [ref[pl.ds(tok0 * TOKEN_SUBLANES + s, n, stride=TOKEN_SUBLANES), :] for s in range(TOKEN_SUBLANES)], axis=1)


def _token_rows(ref, tok, n=1):
    return ref.at[pl.ds(pl.multiple_of(tok * TOKEN_SUBLANES, TOKEN_SUBLANES), n * TOKEN_SUBLANES), :]


def _store_packed_tokens(ref, tok0, x):
    n, half = x.shape[0], D_MODEL // 2
    lo = lax.bitcast_convert_type(x[:, :half].astype(BF16).astype(F32), jnp.uint32) >> 16
    hi = lax.bitcast_convert_type(x[:, half:].astype(BF16).astype(F32), jnp.uint32) & jnp.uint32(0xFFFF0000)
    words = lo | hi
    for s in range(PACKED_SUBLANES):
        ref[pl.ds(tok0 * PACKED_SUBLANES + s, n, stride=PACKED_SUBLANES), :] = words[:, s * LANES:(s + 1) * LANES]


def _load_packed_tokens(ref, tok0, n):
    words = [ref[pl.ds(tok0 * PACKED_SUBLANES + s, n, stride=PACKED_SUBLANES), :] for s in range(PACKED_SUBLANES)]
    lo = [lax.bitcast_convert_type(w << 16, F32).astype(BF16) for w in words]
    hi = [lax.bitcast_convert_type(w & jnp.uint32(0xFFFF0000), F32).astype(BF16) for w in words]
    return jnp.concatenate(lo + hi, axis=1)


def _packed_rows(ref, tok, n=1):
    return ref.at[pl.ds(pl.multiple_of(tok * PACKED_SUBLANES, PACKED_SUBLANES), n * PACKED_SUBLANES), :]


def _first_half_mask(rows):
    lane = lax.broadcasted_iota(jnp.int32, (rows, RET_QK_WIDTH), 1)
    return (lane % RET_DK) < (RET_DK // 2)


def _meta_kernel(meta_ref, g_ref, b_ref, win_ref, cos_ref, sin_ref, zeta_ref, bd_ref, s0_ref, tail_ref):
    h = _layer_norm(meta_ref[...], g_ref[...], b_ref[...], LN_EPS)
    proj = jnp.dot(h.astype(BF16), win_ref[...], preferred_element_type=F32)
    k = _rotary(proj[:, K0:V0], cos_ref[...], sin_ref[...], _first_half_mask(N_META)) * (RET_DK ** -0.5)
    kz = (k * zeta_ref[...]).astype(BF16)
    v = proj[:, V0:G0].astype(BF16)
    kv = lax.dot_general(kz, v, (((0,), (0,)), ((), ())), preferred_element_type=F32)
    s0_ref[...] = kv * bd_ref[...]
    tail_ref[...] = proj[:, U0:]


def _mixer_kernel(x_ref, lng_ref, lnb_ref, win_ref, wout_ref, poolw_ref, poolb_ref, pools_ref, ln1g_ref, ln1b_ref,
                  cos_ref, sin_ref, dmask_ref, xi_ref, zeta_ref, decay_ref, bd_ref, s0_ref, tail0_ref,
                  wr_ref, br_ref,
                  h1_ref, dest_ref, pw_ref, table_ref, buf_ref,
                  state_ref, tail_ref, uext_ref, proj_ref, h0_ref, h0b_ref, mixin_ref, tri_ref, carry_ref,
                  curblk_ref, nalloc_ref, tab_ref, hkeep_ref, zero_ref, dvm_ref, dsm_ref, cnt_ref, cur_ref, csm_ref,
                  scat_sem, idx_sem, zsem):
    rows = x_ref.shape[0]
    n_chunks = rows // CHUNK
    step = pl.program_id(0) * pl.num_programs(1) + pl.program_id(1)
    n_steps = pl.num_programs(0) * pl.num_programs(1)
    first_step = step == 0
    slot = lax.rem(step, 2)
    prev = 1 - slot
    n_blocks = buf_ref.shape[0] // (EXPERT_ROWS * PACKED_SUBLANES) - 2

    def scatter_wait(b):
        for _ in range(TOP_K):
            pltpu.make_async_copy(hkeep_ref.at[b], _packed_rows(buf_ref, 0, rows), scat_sem.at[b]).wait()

    @pl.when(first_step)
    def _():
        r = lax.broadcasted_iota(jnp.int32, (rows, rows), 0)
        c = lax.broadcasted_iota(jnp.int32, (rows, rows), 1)
        tri_ref[...] = jnp.where(r < c, 1.0, 0.0).astype(BF16)
        carry_ref[...] = jnp.zeros_like(carry_ref)
        curblk_ref[...] = jnp.full(curblk_ref.shape, -1.0, F32)
        nalloc_ref[...] = jnp.zeros_like(nalloc_ref)
        tab_ref[...] = jnp.full(tab_ref.shape, float(N_EXPERTS), F32)
        zero_ref[...] = jnp.zeros_like(zero_ref)
        hkeep_ref[1] = jnp.zeros(hkeep_ref.shape[1:], hkeep_ref.dtype)

        def spare(t, _):
            for s in range(TOP_K):
                dsm_ref[s, t] = (n_blocks + s) * EXPERT_ROWS + t
            return 0
        lax.fori_loop(0, rows, spare, 0)

    @pl.when(pl.program_id(1) == 0)
    def _():
        state_ref[...] = s0_ref[...]
        tail_ref[...] = tail0_ref[...]

    @pl.when(step > 0)
    def _():
        pltpu.make_async_copy(dvm_ref, dsm_ref, idx_sem).wait()

    def scatter_previous(t_lo, t_hi):
        for t0 in range(t_lo, t_hi, INDEX_GROUP):
            dst = [[dsm_ref[s, t0 + k] for s in range(TOP_K)] for k in range(INDEX_GROUP)]
            for k in range(INDEX_GROUP):
                for s in range(TOP_K):
                    pltpu.make_async_copy(_packed_rows(hkeep_ref.at[prev], t0 + k), _packed_rows(buf_ref, dst[k][s]),
                                          scat_sem.at[prev]).start(priority=s % 2)

    def ln_body(c, _):
        sl = pl.ds(pl.multiple_of(c * CHUNK, CHUNK), CHUNK)
        h0 = _layer_norm(x_ref[sl, :], lng_ref[...], lnb_ref[...], LN_EPS)
        h0_ref[sl, :] = h0
        h0b_ref[sl, :] = h0.astype(BF16)
        return 0
    lax.fori_loop(0, n_chunks, ln_body, 0, unroll=True)

    proj_ref[...] = jnp.dot(h0b_ref[...], win_ref[...], preferred_element_type=F32)
    scatter_previous(0, rows)

    first_half = _first_half_mask(CHUNK)
    head_of_lane = lax.broadcasted_iota(jnp.int32, (CHUNK, RET_QK_WIDTH), 1) // RET_DK

    def ret_body(c, _):
        sl = pl.ds(pl.multiple_of(c * CHUNK, CHUNK), CHUNK)
        cos = cos_ref[sl, :]
        sin = sin_ref[sl, :]
        q = _rotary(proj_ref[sl, Q0:K0], cos, sin, first_half)
        k = _rotary(proj_ref[sl, K0:V0], cos, sin, first_half) * (RET_DK ** -0.5)
        qb = q.astype(BF16)
        kb = k.astype(BF16)
        vb = proj_ref[sl, V0:G0].astype(BF16)
        q_heads = jnp.concatenate(
            [jnp.where(head_of_lane == h, qb, jnp.zeros_like(qb)) for h in range(RET_HEADS)], axis=0)
        scores = lax.dot_general(q_heads, kb, (((1,), (1,)), ((), ())), preferred_element_type=F32)
        p = (scores * dmask_ref[...]).astype(BF16)
        inner = jnp.concatenate(
            [jnp.dot(p[h * CHUNK:(h + 1) * CHUNK, :], vb[:, h * RET_DV:(h + 1) * RET_DV],
                     preferred_element_type=F32) for h in range(RET_HEADS)], axis=1)
        state = state_ref[...]
        cross = jnp.dot((q * xi_ref[...]).astype(BF16), state.astype(BF16), preferred_element_type=F32)
        kz = (k * zeta_ref[...]).astype(BF16)
        kv = lax.dot_general(kz, vb, (((0,), (0,)), ((), ())), preferred_element_type=F32)
        state_ref[...] = state * decay_ref[...] + kv * bd_ref[...]
        ret = inner + cross
        gate = proj_ref[sl, G0:U0]
        outs = []
        for h in range(RET_HEADS):
            o = ret[:, h * RET_DV:(h + 1) * RET_DV]
            mu = jnp.mean(o, axis=-1, keepdims=True)
            oc = o - mu
            var = jnp.mean(oc * oc, axis=-1, keepdims=True)
            outs.append(oc * lax.rsqrt(var + GN_EPS))
        gn = jnp.concatenate(outs, axis=1)
        mixin_ref[sl, 0:RET_WIDTH] = (gate * jax.nn.sigmoid(gate) * gn).astype(BF16)
        return 0
    lax.fori_loop(0, n_chunks, ret_body, 0, unroll=True)

    uext_ref[0:N_META, :] = tail_ref[...]
    uext_ref[N_META:, :] = proj_ref[:, U0:]
    tail_ref[...] = uext_ref[rows:, :]
    for g, w in enumerate(POOL_WINDOWS):
        lanes = slice(g * POOL_CH, (g + 1) * POOL_CH)
        e = uext_ref[:, lanes]
        acc = e
        shift = 1
        while shift < w:
            acc = acc + pltpu.roll(acc, shift, 0)
            shift *= 2
        pooled = acc[N_META:, :] * (1.0 / w) - e[N_META:, :]
        mixed = jnp.dot(pooled.astype(BF16), poolw_ref[g], preferred_element_type=F32) + poolb_ref[:, lanes]
        mixin_ref[:, RET_WIDTH + g * POOL_CH:RET_WIDTH + (g + 1) * POOL_CH] = (mixed * pools_ref[:, lanes]).astype(BF16)

    proj_ref[:, 0:D_MODEL] = jnp.dot(mixin_ref[...], wout_ref[...], preferred_element_type=F32)

    @pl.when(step >= 1)
    def _():
        scatter_wait(slot)

    keep = hkeep_ref.at[slot]

    def ln1_body(c, _):
        sl = pl.ds(pl.multiple_of(c * CHUNK, CHUNK), CHUNK)
        h1 = _layer_norm(ALPHA * h0_ref[sl, :] + proj_ref[sl, 0:D_MODEL], ln1g_ref[...], ln1b_ref[...], LN_EPS)
        h1_ref[sl, :] = h1
        _store_packed_tokens(keep, c * CHUNK, h1)
        h0b_ref[sl, :] = h1.astype(BF16)
        return 0
    lax.fori_loop(0, n_chunks, ln1_body, 0, unroll=True)

    logits = lax.dot_general(wr_ref[...], h0b_ref[...], (((1,), (1,)), ((), ())), preferred_element_type=F32)
    logits = logits + br_ref[...]
    gl = logits[0:N_GROUPS, :]
    gmax = jnp.max(gl, axis=0, keepdims=True)
    g_p = 1.0 / jnp.sum(jnp.exp(gl - gmax), axis=0, keepdims=True)
    grow = lax.broadcasted_iota(jnp.int32, gl.shape, 0)
    g_idx = jnp.min(jnp.where(gl == gmax, grow, N_GROUPS), axis=0, keepdims=True)
    sel = logits[8:8 + EXPERTS_PER_GROUP, :]
    for g in range(1, N_GROUPS):
        sel = jnp.where(g_idx == g, logits[8 + g * EXPERTS_PER_GROUP:8 + (g + 1) * EXPERTS_PER_GROUP, :], sel)
    erow = lax.broadcasted_iota(jnp.int32, sel.shape, 0)
    m1 = jnp.max(sel, axis=0, keepdims=True)
    i1 = jnp.min(jnp.where(sel == m1, erow, EXPERTS_PER_GROUP), axis=0, keepdims=True)
    sel2 = jnp.where(erow == i1, -jnp.inf, sel)
    m2 = jnp.max(sel2, axis=0, keepdims=True)
    i2 = jnp.min(jnp.where(sel2 == m2, erow, EXPERTS_PER_GROUP), axis=0, keepdims=True)
    e2 = jnp.exp(m2 - m1)
    w1 = 1.0 / (1.0 + e2)
    w2 = e2 / (1.0 + e2)
    pe0 = g_idx * EXPERTS_PER_GROUP + i1
    pe1 = g_idx * EXPERTS_PER_GROUP + i2
    pw_ref[...] = jnp.concatenate([g_p * w1, g_p * w2], axis=0)
    xrow = lax.broadcasted_iota(jnp.int32, (N_EXPERTS, rows), 0)
    oh0 = xrow == pe0
    oh1 = xrow == pe1
    oh = jnp.where(jnp.logical_or(oh0, oh1), 1.0, 0.0)
    carry = carry_ref[...]
    count = carry[:, 0:1]
    prefix = jnp.dot(oh.astype(BF16), tri_ref[...], preferred_element_type=F32) + count

    inv_rows = 1.0 / EXPERT_ROWS
    added = jnp.sum(oh, axis=1, keepdims=True)
    blocks_old = jnp.floor((count + (EXPERT_ROWS - 1)) * inv_rows)
    blocks_new = jnp.floor((count + added + (EXPERT_ROWS - 1)) * inv_rows)
    need = blocks_new - blocks_old
    er = lax.broadcasted_iota(jnp.int32, (N_EXPERTS, N_EXPERTS), 0)
    ec = lax.broadcasted_iota(jnp.int32, (N_EXPERTS, N_EXPERTS), 1)
    earlier = jnp.where(ec < er, 1.0, 0.0)
    need_lanes = jnp.broadcast_to(need, (N_EXPERTS, LANES))
    before = jnp.dot(earlier.astype(BF16), need_lanes.astype(BF16), preferred_element_type=F32)[:, 0:1]
    nalloc = nalloc_ref[0:1, 0:1]
    new_blk = nalloc + before
    cur_blk = curblk_ref[:, 0:1]
    ordinal = jnp.floor(prefix * inv_rows)
    blk = jnp.where(ordinal < blocks_old, cur_blk, new_blk)
    row_in_buf = blk * EXPERT_ROWS + (prefix - ordinal * EXPERT_ROWS)
    dest0 = jnp.sum(jnp.where(oh0, row_in_buf, 0.0), axis=0, keepdims=True)
    dest1 = jnp.sum(jnp.where(oh1, row_in_buf, 0.0), axis=0, keepdims=True)
    dest = jnp.concatenate([dest0, dest1], axis=0).astype(jnp.int32)
    dest_ref[...] = dest
    dvm_ref[...] = dest
    pltpu.make_async_copy(dvm_ref, dsm_ref, idx_sem).start()

    got_new = need > 0.0
    blk_lane = lax.broadcasted_iota(jnp.int32, (N_EXPERTS, BLOCK_TABLE_LANES), 1).astype(F32)
    erow_f = lax.broadcasted_iota(jnp.int32, (N_EXPERTS, BLOCK_TABLE_LANES), 0).astype(F32)
    owner = jnp.max(jnp.where(jnp.logical_and(got_new, blk_lane == new_blk), erow_f, -1.0), axis=0, keepdims=True)
    tab = jnp.where(owner >= 0.0, owner, tab_ref[0:1, :])
    tab_ref[...] = jnp.broadcast_to(tab, tab_ref.shape)
    cur_blk = jnp.where(got_new, new_blk, cur_blk)
    curblk_ref[...] = jnp.broadcast_to(cur_blk, curblk_ref.shape)
    nalloc_ref[...] = jnp.broadcast_to(nalloc + jnp.sum(need, axis=0, keepdims=True), nalloc_ref.shape)
    carry = carry + added
    carry_ref[...] = carry
    cnt_ref[...] = carry.astype(jnp.int32)
    cur_ref[...] = jnp.broadcast_to(cur_blk, cur_ref.shape).astype(jnp.int32)
    table_ref[...] = jnp.broadcast_to(tab, table_ref.shape).astype(jnp.int32)

    @pl.when(step == n_steps - 1)
    def _():
        pltpu.make_async_copy(dvm_ref, dsm_ref, idx_sem).wait()

        def last_rows(t, _):
            for s in range(TOP_K):
                pltpu.make_async_copy(_packed_rows(keep, t), _packed_rows(buf_ref, dsm_ref[s, t]), scat_sem.at[slot]).start()
            return 0
        lax.fori_loop(0, rows, last_rows, 0)

        csm_cnt = pltpu.make_async_copy(cnt_ref, csm_ref.at[0], zsem)
        csm_cur = pltpu.make_async_copy(cur_ref, csm_ref.at[1], zsem)
        csm_cnt.start()
        csm_cur.start()
        csm_cnt.wait()
        csm_cur.wait()

        def go(copy, do_start):
            if do_start:
                copy.start()
            else:
                copy.wait()

        def tail_fill(do_start):
            def body(e, carry):
                cnt_e = csm_ref[0, e, 0]
                gap = (EXPERT_ROWS - lax.rem(cnt_e, EXPERT_ROWS)) % EXPERT_ROWS
                first = csm_ref[1, e, 0] * EXPERT_ROWS + (EXPERT_ROWS - gap)
                for bit in range(EXPERT_ROWS.bit_length() - 1):
                    run = 1 << bit

                    @pl.when((gap >> bit) & 1 == 1)
                    def _():
                        go(pltpu.make_async_copy(_packed_rows(zero_ref, 0, run),
                                                 _packed_rows(buf_ref, first + (gap & (run - 1)), run), zsem), do_start)
                return carry
            return body

        def unused_fill(do_start):
            def body(b, carry):
                go(pltpu.make_async_copy(zero_ref, _packed_rows(buf_ref, b * EXPERT_ROWS, EXPERT_ROWS), zsem), do_start)
                return carry
            return body

        handed_out = jnp.int32(0)
        for e in range(N_EXPERTS):
            handed_out = handed_out + (csm_ref[0, e, 0] + (EXPERT_ROWS - 1)) // EXPERT_ROWS
        for do_start in (True, False):
            lax.fori_loop(0, N_EXPERTS, tail_fill(do_start), 0)
            lax.fori_loop(handed_out, n_blocks, unused_fill(do_start), 0)

        scatter_wait(prev)
        scatter_wait(slot)


def _expert_kernel(be_ref, nv_ref, bi_ref, bo_ref, x_ref, wg_ref, wu_ref, wd_ref, y_ref, wgb_ref, wub_ref, wdb_ref):
    i = pl.program_id(0)
    nv = nv_ref[i]
    new_expert = jnp.logical_or(i == 0, be_ref[i] != be_ref[jnp.maximum(i - 1, 0)])

    @pl.when(new_expert)
    def _():
        wgb_ref[...] = wg_ref[...].astype(BF16)
        wub_ref[...] = wu_ref[...].astype(BF16)
        wdb_ref[...] = wd_ref[...].astype(BF16)

    @pl.when(nv == 0)
    def _():
        y_ref[...] = jnp.zeros_like(y_ref)

    @pl.when(nv > 0)
    def _():
        x = _load_packed_tokens(x_ref, 0, EXPERT_ROWS)
        gate = jnp.dot(x, wgb_ref[...], preferred_element_type=F32)
        up = jnp.dot(x, wub_ref[...], preferred_element_type=F32)
        act = (gate * jax.nn.sigmoid(gate) * up).astype(BF16)
        _store_token_tiles(y_ref, 0, jnp.dot(act, wdb_ref[...], preferred_element_type=F32))


def _combine_kernel(dcur_ref, dnext_ref, h_ref, pw_ref, g_ref, b_ref, y_ref, o_ref, yb0, yb1, yb2, yb3, sems):
    part = COMBINE_PART_ROWS
    bufs = (yb0, yb1, yb2, yb3)
    step = pl.program_id(0)
    last = pl.num_programs(0) - 1

    def copy(src_tok, ybuf, t, slot, sem):
        return pltpu.make_async_copy(_token_rows(y_ref, src_tok), _token_rows(ybuf.at[slot], t), sem)

    def issue(dref, col0, ybuf, sem):
        for t0 in range(0, part, INDEX_GROUP):
            src = [[dref[slot, col0 + t0 + k] for slot in range(TOP_K)] for k in range(INDEX_GROUP)]
            for k in range(INDEX_GROUP):
                for slot in range(TOP_K):
                    copy(src[k][slot], ybuf, t0 + k, slot, sem).start(priority=slot % 2)

    def wait(ybuf, sem):
        for slot in range(TOP_K):
            pltpu.make_async_copy(_token_rows(y_ref, 0, part), ybuf.at[slot], sem).wait()

    def finish(ybuf, r0):
        rows = pl.ds(r0, part)
        pw = pw_ref[rows, :]
        y = pw[:, 0:1] * _load_token_tiles(ybuf.at[0], 0, part) + pw[:, 1:2] * _load_token_tiles(ybuf.at[1], 0, part)
        o_ref[rows, :] = _layer_norm(ALPHA * h_ref[rows, :] + y, g_ref[...], b_ref[...], LN_EPS)

    @pl.when(step == 0)
    def _():
        for p in range(2):
            def first(t, _, p=p):
                for slot in range(TOP_K):
                    copy(dcur_ref[slot, p * part + t], bufs[p], t, slot, sems.at[p]).start()
                return 0
            lax.fori_loop(0, part, first, 0)

    for p in range(COMBINE_PHASES):
        ahead = (p + 2) % COMBINE_PHASES
        wait(bufs[p], sems.at[p])
        if p + 2 < COMBINE_PHASES:
            issue(dcur_ref, (p + 2) * part, bufs[ahead], sems.at[ahead])
        else:
            issue(dnext_ref, ahead * part, bufs[ahead], sems.at[ahead])
        finish(bufs[p], p * part)

    @pl.when(step == last)
    def _():
        wait(bufs[0], sems.at[0])
        wait(bufs[1], sems.at[1])


def _tables(seq):
    log_g = jnp.log1p(-jnp.power(2.0, -5.0 - jnp.arange(RET_HEADS, dtype=F32)))
    i = jnp.arange(CHUNK, dtype=F32)
    rel = i[:, None] - i[None, :]
    dmask = jnp.where(rel[None] >= 0, jnp.exp(jnp.maximum(rel, 0.0)[None] * log_g[:, None, None]), 0.0)
    dmask = dmask.reshape(RET_HEADS * CHUNK, CHUNK)
    lg_lane = jnp.repeat(log_g, RET_DK)
    xi = jnp.exp((i + 1)[:, None] * lg_lane[None, :])
    zeta = jnp.exp((CHUNK - 1 - i)[:, None] * lg_lane[None, :])
    decay = jnp.broadcast_to(jnp.exp(CHUNK * lg_lane)[:, None], (RET_QK_WIDTH, RET_WIDTH))
    bd = (jnp.arange(RET_QK_WIDTH)[:, None] // RET_DK == jnp.arange(RET_WIDTH)[None, :] // RET_DV).astype(F32)
    zeta_meta = zeta[CHUNK - N_META:, :]
    half = RET_DK // 2
    inv = ROPE_BASE ** (-jnp.arange(half, dtype=F32) / half)
    pos = jnp.arange(N_META + seq, dtype=F32)
    ang = pos[:, None] * inv[None, :]
    cos = jnp.tile(jnp.cos(ang), (1, 2 * RET_HEADS))
    sin = jnp.tile(jnp.concatenate([-jnp.sin(ang), jnp.sin(ang)], axis=1), (1, RET_HEADS))
    return dict(dmask=dmask, xi=xi, zeta=zeta, decay=decay, bd=bd, zeta_meta=zeta_meta,
                cos_meta=cos[:N_META], sin_meta=sin[:N_META], cos=cos[N_META:], sin=sin[N_META:])


def _full(shape):
    return pl.BlockSpec(shape, lambda *_: (0,) * len(shape))


def kernel(x, meta_tokens, ln_emb_g, ln_emb_b, w_in, pool_w, pool_b, pool_scale, w_out, ln1_g, ln1_b, router_group_w, router_group_b, router_expert_w, router_expert_b, expert_w_gate, expert_w_up, expert_w_down, ln2_g, ln2_b):
    batch, seq, d = x.shape
    assert d == D_MODEL and seq % MIX_ROWS == 0 and MIX_ROWS <= EXPERT_ROWS
    n_tok = batch * seq
    t_blocks = seq // MIX_ROWS
    tb = _tables(seq)

    row = lambda a: a.reshape(1, -1).astype(F32)
    win_b = w_in[0].astype(BF16)
    wout_b = w_out[0].astype(BF16)
    poolw_b = pool_w[0].astype(BF16)
    wr = jnp.zeros((ROUTER_ROWS, D_MODEL), F32)
    wr = wr.at[0:N_GROUPS].set(router_group_w[0].T).at[8:8 + N_EXPERTS].set(router_expert_w[0].T).astype(BF16)
    br = jnp.zeros((ROUTER_ROWS, 1), F32)
    br = br.at[0:N_GROUPS, 0].set(router_group_b[0]).at[8:8 + N_EXPERTS, 0].set(router_expert_b[0])

    s0, tail0 = pl.pallas_call(
        _meta_kernel,
        out_shape=(jax.ShapeDtypeStruct((RET_QK_WIDTH, RET_WIDTH), F32), jax.ShapeDtypeStruct((N_META, POOL_WIDTH), F32)),
        name="meta_prep",
    )(meta_tokens.astype(F32), row(ln_emb_g), row(ln_emb_b), win_b, tb["cos_meta"], tb["sin_meta"], tb["zeta_meta"], tb["bd"])

    tok_spec = pl.BlockSpec((None, MIX_ROWS, D_MODEL), lambda b, j: (b, j, 0))
    pair_spec = pl.BlockSpec((TOP_K, MIX_ROWS), lambda b, j: (0, b * t_blocks + j))
    rope_spec = pl.BlockSpec((MIX_ROWS, RET_QK_WIDTH), lambda b, j: (j, 0))
    n_blocks = (n_tok * TOP_K) // EXPERT_ROWS + N_EXPERTS
    assert n_blocks + 2 <= BLOCK_TABLE_LANES
    tile_block = (EXPERT_ROWS * TOKEN_SUBLANES, LANES)
    packed_block = (EXPERT_ROWS * PACKED_SUBLANES, LANES)
    sorted_shape = (n_blocks * EXPERT_ROWS * TOKEN_SUBLANES, LANES)
    buf_shape = ((n_blocks + 2) * EXPERT_ROWS * PACKED_SUBLANES, LANES)
    stage_block = (MIX_ROWS * PACKED_SUBLANES, LANES)
    h1, dest, pair_w, blk_table, buf = pl.pallas_call(
        _mixer_kernel,
        grid=(batch, t_blocks),
        in_specs=[tok_spec, _full((1, D_MODEL)), _full((1, D_MODEL)), _full((D_MODEL, IN_COLS)), _full((D_MODEL, D_MODEL)),
                  _full((POOL_GROUPS, POOL_CH, POOL_CH)), _full((1, POOL_WIDTH)), _full((1, POOL_WIDTH)),
                  _full((1, D_MODEL)), _full((1, D_MODEL)), rope_spec, rope_spec,
                  _full((RET_HEADS * CHUNK, CHUNK)), _full((CHUNK, RET_QK_WIDTH)), _full((CHUNK, RET_QK_WIDTH)),
                  _full((RET_QK_WIDTH, RET_WIDTH)), _full((RET_QK_WIDTH, RET_WIDTH)),
                  _full((RET_QK_WIDTH, RET_WIDTH)), _full((N_META, POOL_WIDTH)),
                  _full((ROUTER_ROWS, D_MODEL)), _full((ROUTER_ROWS, 1))],
        out_specs=[tok_spec, pair_spec, pair_spec,
                   _full((8, BLOCK_TABLE_LANES)),
                   pl.BlockSpec(memory_space=pl.ANY)],
        out_shape=[jax.ShapeDtypeStruct((batch, seq, D_MODEL), F32),
                   jax.ShapeDtypeStruct((TOP_K, n_tok), jnp.int32),
                   jax.ShapeDtypeStruct((TOP_K, n_tok), F32),
                   jax.ShapeDtypeStruct((8, BLOCK_TABLE_LANES), jnp.int32),
                   jax.ShapeDtypeStruct(buf_shape, jnp.uint32)],
        scratch_shapes=[pltpu.VMEM((RET_QK_WIDTH, RET_WIDTH), F32),
                        pltpu.VMEM((N_META, POOL_WIDTH), F32),
                        pltpu.VMEM((MIX_ROWS + N_META, POOL_WIDTH), F32),
                        pltpu.VMEM((MIX_ROWS, IN_COLS), F32),
                        pltpu.VMEM((MIX_ROWS, D_MODEL), F32),
                        pltpu.VMEM((MIX_ROWS, D_MODEL), BF16),
                        pltpu.VMEM((MIX_ROWS, D_MODEL), BF16),
                        pltpu.VMEM((MIX_ROWS, MIX_ROWS), BF16),
                        pltpu.VMEM((N_EXPERTS, LANES), F32),
                        pltpu.VMEM((N_EXPERTS, LANES), F32),
                        pltpu.VMEM((8, LANES), F32),
                        pltpu.VMEM((8, BLOCK_TABLE_LANES), F32),
                        pltpu.VMEM((2,) + stage_block, jnp.uint32),
                        pltpu.VMEM(packed_block, jnp.uint32),
                        pltpu.VMEM((TOP_K, MIX_ROWS), jnp.int32),
                        pltpu.SMEM((TOP_K, MIX_ROWS), jnp.int32),
                        pltpu.VMEM((N_EXPERTS, LANES), jnp.int32),
                        pltpu.VMEM((N_EXPERTS, LANES), jnp.int32),
                        pltpu.SMEM((2, N_EXPERTS, LANES), jnp.int32),
                        pltpu.SemaphoreType.DMA((2,)),
                        pltpu.SemaphoreType.DMA, pltpu.SemaphoreType.DMA],
        compiler_params=pltpu.CompilerParams(dimension_semantics=("arbitrary", "arbitrary"),
                                             vmem_limit_bytes=VMEM_LIMIT),
        name="mixer",
    )(x, row(ln_emb_g), row(ln_emb_b), win_b, wout_b, poolw_b, row(pool_b[0]), row(pool_scale[0]),
      row(ln1_g[0]), row(ln1_b[0]), tb["cos"], tb["sin"], tb["dmask"], tb["xi"], tb["zeta"], tb["decay"], tb["bd"],
      s0, tail0, wr, br)

    bidx = jnp.arange(n_blocks, dtype=jnp.int32)
    owner = blk_table[0, :n_blocks]
    key = owner * BLOCK_TABLE_LANES + bidx
    place = jnp.sum(key[None, :] < key[:, None], axis=1)
    order = jnp.sum(jnp.where(place[None, :] == bidx[:, None], bidx[None, :], 0), axis=1).astype(jnp.int32)
    owner_in_order = jnp.sum(jnp.where(place[None, :] == bidx[:, None], owner[None, :], 0), axis=1)
    n_used = jnp.sum(owner < N_EXPERTS)
    used = bidx < n_used
    last_used = jnp.maximum(n_used - 1, 0)
    blk_in = jnp.where(used, order, order[last_used]).astype(jnp.int32)
    blk_e = jnp.where(used, owner_in_order, owner_in_order[last_used]).astype(jnp.int32)
    blk_nv = used.astype(jnp.int32)

    y_sorted = pl.pallas_call(
        _expert_kernel,
        grid_spec=pltpu.PrefetchScalarGridSpec(
            num_scalar_prefetch=4,
            grid=(n_blocks,),
            in_specs=[pl.BlockSpec(packed_block, lambda i, be, nv, bi, bo: (bi[i], 0)),
                      pl.BlockSpec((None, D_MODEL, D_EXPERT), lambda i, be, nv, bi, bo: (be[i], 0, 0)),
                      pl.BlockSpec((None, D_MODEL, D_EXPERT), lambda i, be, nv, bi, bo: (be[i], 0, 0)),
                      pl.BlockSpec((None, D_EXPERT, D_MODEL), lambda i, be, nv, bi, bo: (be[i], 0, 0))],
            out_specs=pl.BlockSpec(tile_block, lambda i, be, nv, bi, bo: (bo[i], 0)),
            scratch_shapes=[pltpu.VMEM((D_MODEL, D_EXPERT), BF16), pltpu.VMEM((D_MODEL, D_EXPERT), BF16),
                            pltpu.VMEM((D_EXPERT, D_MODEL), BF16)],
        ),
        out_shape=jax.ShapeDtypeStruct(sorted_shape, F32),
        compiler_params=pltpu.CompilerParams(dimension_semantics=("arbitrary",), vmem_limit_bytes=VMEM_LIMIT),
        name="experts",
    )(blk_e, blk_nv, blk_in, order, buf, expert_w_gate[0], expert_w_up[0], expert_w_down[0])

    comb_rows = COMBINE_PHASES * COMBINE_PART_ROWS
    comb_steps = n_tok // comb_rows
    assert n_tok % comb_rows == 0
    rows_spec = pl.BlockSpec((comb_rows, D_MODEL), lambda i: (i, 0))
    out = pl.pallas_call(
        _combine_kernel,
        grid=(comb_steps,),
        in_specs=[pl.BlockSpec((TOP_K, comb_rows), lambda i: (0, i), memory_space=pltpu.SMEM),
                  pl.BlockSpec((TOP_K, comb_rows), lambda i: (0, jnp.minimum(i + 1, comb_steps - 1)),
                               memory_space=pltpu.SMEM),
                  rows_spec,
                  pl.BlockSpec((comb_rows, TOP_K), lambda i: (i, 0)),
                  pl.BlockSpec((1, D_MODEL), lambda i: (0, 0)), pl.BlockSpec((1, D_MODEL), lambda i: (0, 0)),
                  pl.BlockSpec(memory_space=pl.ANY)],
        out_specs=rows_spec,
        out_shape=jax.ShapeDtypeStruct((n_tok, D_MODEL), F32),
        scratch_shapes=[pltpu.VMEM((TOP_K, COMBINE_PART_ROWS * TOKEN_SUBLANES, LANES), F32)] * COMBINE_PHASES
                       + [pltpu.SemaphoreType.DMA((COMBINE_PHASES,))],
        compiler_params=pltpu.CompilerParams(dimension_semantics=("arbitrary",), vmem_limit_bytes=VMEM_LIMIT),
        name="combine",
    )(dest, dest, h1.reshape(n_tok, D_MODEL), pair_w.T, row(ln2_g[0]), row(ln2_b[0]), y_sorted)
    return out.reshape(batch, seq, D_MODEL)
```

```python
import jax
import jax.numpy as jnp
from jax import lax
from jax.experimental import pallas as pl
from jax.experimental.pallas import tpu as pltpu

D_MODEL = 1024
DEPTH = 1
N_META = 16
RET_HEADS = 4
RET_WIDTH = D_MODEL // 2
RET_DV = RET_WIDTH // RET_HEADS
RET_DK = RET_DV // 2
RET_QK_WIDTH = RET_HEADS * RET_DK
CHUNK = 128
ROPE_BASE = 10000.0
POOL_WINDOWS = (2, 4, 8, 16)
POOL_GROUPS = len(POOL_WINDOWS)
POOL_WIDTH = D_MODEL // 2
POOL_CH = POOL_WIDTH // POOL_GROUPS
IN_COLS = 2 * RET_QK_WIDTH + 2 * RET_WIDTH + POOL_WIDTH
N_GROUPS = 4
EXPERTS_PER_GROUP = 8
N_EXPERTS = N_GROUPS * EXPERTS_PER_GROUP
D_EXPERT = D_MODEL // 2
TOP_K = 2
LN_EPS = 1e-5
GN_EPS = 1e-6
ALPHA = (2 * DEPTH) ** 0.25

Q0, K0, V0, G0, U0 = 0, RET_QK_WIDTH, 2 * RET_QK_WIDTH, 2 * RET_QK_WIDTH + RET_WIDTH, 2 * RET_QK_WIDTH + 2 * RET_WIDTH

MIX_ROWS = 512
EXPERT_ROWS = 512
ROUTER_ROWS = 40
BLOCK_TABLE_LANES = 384
COMBINE_PHASES = 4
COMBINE_PART_ROWS = 128
INDEX_GROUP = 8
VMEM_LIMIT = 56 * 1024 * 1024
LANES = 128
TOKEN_SUBLANES = D_MODEL // LANES
PACKED_SUBLANES = TOKEN_SUBLANES // 2

F32 = jnp.float32
BF16 = jnp.bfloat16


def _layer_norm(x, g, b, eps):
    mu = jnp.mean(x, axis=-1, keepdims=True)
    xc = x - mu
    var = jnp.mean(xc * xc, axis=-1, keepdims=True)
    return xc * lax.rsqrt(var + eps) * g + b


def _rotary(z, cos, sin_signed, first_half):
    partner = jnp.where(first_half, pltpu.roll(z, RET_QK_WIDTH - RET_DK // 2, 1), pltpu.roll(z, RET_DK // 2, 1))
    return z * cos + partner * sin_signed


def _store_token_tiles(ref, tok0, x):
    n = x.shape[0]
    for s in range(TOKEN_SUBLANES):
        ref[pl.ds(tok0 * TOKEN_SUBLANES + s, n, stride=TOKEN_SUBLANES), :] = x[:, s * LANES:(s + 1) * LANES]


def _load_token_tiles(ref, tok0, n):
    return jnp.concatenate(
        [ref[pl.ds(tok0 * TOKEN_SUBLANES + s, n, stride=TOKEN_SUBLANES), :] for s in range(TOKEN_SUBLANES)], axis=1)


def _token_rows(ref, tok, n=1):
    return ref.at[pl.ds(pl.multiple_of(tok * TOKEN_SUBLANES, TOKEN_SUBLANES), n * TOKEN_SUBLANES), :]


def _store_packed_tokens(ref, tok0, x):
    n, half = x.shape[0], D_MODEL // 2
    lo = lax.bitcast_convert_type(x[:, :half].astype(BF16).astype(F32), jnp.uint32) >> 16
    hi = lax.bitcast_convert_type(x[:, half:].astype(BF16).astype(F32), jnp.uint32) & jnp.uint32(0xFFFF0000)
    words = lo | hi
    for s in range(PACKED_SUBLANES):
        ref[pl.ds(tok0 * PACKED_SUBLANES + s, n, stride=PACKED_SUBLANES), :] = words[:, s * LANES:(s + 1) * LANES]


def _load_packed_tokens(ref, tok0, n):
    words = [ref[pl.ds(tok0 * PACKED_SUBLANES + s, n, stride=PACKED_SUBLANES), :] for s in range(PACKED_SUBLANES)]
    lo = [lax.bitcast_convert_type(w << 16, F32).astype(BF16) for w in words]
    hi = [lax.bitcast_convert_type(w & jnp.uint32(0xFFFF0000), F32).astype(BF16) for w in words]
    return jnp.concatenate(lo + hi, axis=1)


def _packed_rows(ref, tok, n=1):
    return ref.at[pl.ds(pl.multiple_of(tok * PACKED_SUBLANES, PACKED_SUBLANES), n * PACKED_SUBLANES), :]


def _first_half_mask(rows):
    lane = lax.broadcasted_iota(jnp.int32, (rows, RET_QK_WIDTH), 1)
    return (lane % RET_DK) < (RET_DK // 2)


def _meta_kernel(meta_ref, g_ref, b_ref, win_ref, cos_ref, sin_ref, zeta_ref, bd_ref, s0_ref, tail_ref):
    h = _layer_norm(meta_ref[...], g_ref[...], b_ref[...], LN_EPS)
    proj = jnp.dot(h.astype(BF16), win_ref[...], preferred_element_type=F32)
    k = _rotary(proj[:, K0:V0], cos_ref[...], sin_ref[...], _first_half_mask(N_META)) * (RET_DK ** -0.5)
    kz = (k * zeta_ref[...]).astype(BF16)
    v = proj[:, V0:G0].astype(BF16)
    kv = lax.dot_general(kz, v, (((0,), (0,)), ((), ())), preferred_element_type=F32)
    s0_ref[...] = kv * bd_ref[...]
    tail_ref[...] = proj[:, U0:]


def _mixer_kernel(x_ref, lng_ref, lnb_ref, win_ref, wout_ref, poolw_ref, poolb_ref, pools_ref, ln1g_ref, ln1b_ref,
                  cos_ref, sin_ref, dmask_ref, xi_ref, zeta_ref, decay_ref, bd_ref, s0_ref, tail0_ref,
                  wr_ref, br_ref,
                  h1_ref, dest_ref, pw_ref, table_ref, buf_ref,
                  state_ref, tail_ref, uext_ref, proj_ref, h0_ref, h0b_ref, mixin_ref, tri_ref, carry_ref,
                  curblk_ref, nalloc_ref, tab_ref, hkeep_ref, zero_ref, dvm_ref, dsm_ref, cnt_ref, cur_ref, csm_ref,
                  scat_sem, idx_sem, zsem):
    rows = x_ref.shape[0]
    n_chunks = rows // CHUNK
    step = pl.program_id(0) * pl.num_programs(1) + pl.program_id(1)
    n_steps = pl.num_programs(0) * pl.num_programs(1)
    first_step = step == 0
    slot = lax.rem(step, 2)
    prev = 1 - slot
    n_blocks = buf_ref.shape[0] // (EXPERT_ROWS * PACKED_SUBLANES) - 2

    def scatter_wait(b):
        for _ in range(TOP_K):
            pltpu.make_async_copy(hkeep_ref.at[b], _packed_rows(buf_ref, 0, rows), scat_sem.at[b]).wait()

    @pl.when(first_step)
    def _():
        r = lax.broadcasted_iota(jnp.int32, (rows, rows), 0)
        c = lax.broadcasted_iota(jnp.int32, (rows, rows), 1)
        tri_ref[...] = jnp.where(r < c, 1.0, 0.0).astype(BF16)
        carry_ref[...] = jnp.zeros_like(carry_ref)
        curblk_ref[...] = jnp.full(curblk_ref.shape, -1.0, F32)
        nalloc_ref[...] = jnp.zeros_like(nalloc_ref)
        tab_ref[...] = jnp.full(tab_ref.shape, float(N_EXPERTS), F32)
        zero_ref[...] = jnp.zeros_like(zero_ref)
        hkeep_ref[1] = jnp.zeros(hkeep_ref.shape[1:], hkeep_ref.dtype)

        def spare(t, _):
            for s in range(TOP_K):
                dsm_ref[s, t] = (n_blocks + s) * EXPERT_ROWS + t
            return 0
        lax.fori_loop(0, rows, spare, 0)

    @pl.when(pl.program_id(1) == 0)
    def _():
        state_ref[...] = s0_ref[...]
        tail_ref[...] = tail0_ref[...]

    @pl.when(step > 0)
    def _():
        pltpu.make_async_copy(dvm_ref, dsm_ref, idx_sem).wait()

    def scatter_previous(t_lo, t_hi):
        for t0 in range(t_lo, t_hi, INDEX_GROUP):
            dst = [[dsm_ref[s, t0 + k] for s in range(TOP_K)] for k in range(INDEX_GROUP)]
            for k in range(INDEX_GROUP):
                for s in range(TOP_K):
                    pltpu.make_async_copy(_packed_rows(hkeep_ref.at[prev], t0 + k), _packed_rows(buf_ref, dst[k][s]),
                                          scat_sem.at[prev]).start(priority=s % 2)

    def ln_body(c, _):
        sl = pl.ds(pl.multiple_of(c * CHUNK, CHUNK), CHUNK)
        h0 = _layer_norm(x_ref[sl, :], lng_ref[...], lnb_ref[...], LN_EPS)
        h0_ref[sl, :] = h0
        h0b_ref[sl, :] = h0.astype(BF16)
        return 0
    lax.fori_loop(0, n_chunks, ln_body, 0, unroll=True)

    proj_ref[...] = jnp.dot(h0b_ref[...], win_ref[...], preferred_element_type=F32)
    scatter_previous(0, rows)

    first_half = _first_half_mask(CHUNK)
    head_of_lane = lax.broadcasted_iota(jnp.int32, (CHUNK, RET_QK_WIDTH), 1) // RET_DK

    def ret_body(c, _):
        sl = pl.ds(pl.multiple_of(c * CHUNK, CHUNK), CHUNK)
        cos = cos_ref[sl, :]
        sin = sin_ref[sl, :]
        q = _rotary(proj_ref[sl, Q0:K0], cos, sin, first_half)
        k = _rotary(proj_ref[sl, K0:V0], cos, sin, first_half) * (RET_DK ** -0.5)
        qb = q.astype(BF16)
        kb = k.astype(BF16)
        vb = proj_ref[sl, V0:G0].astype(BF16)
        q_heads = jnp.concatenate(
            [jnp.where(head_of_lane == h, qb, jnp.zeros_like(qb)) for h in range(RET_HEADS)], axis=0)
        scores = lax.dot_general(q_heads, kb, (((1,), (1,)), ((), ())), preferred_element_type=F32)
        p = (scores * dmask_ref[...]).astype(BF16)
        inner = jnp.concatenate(
            [jnp.dot(p[h * CHUNK:(h + 1) * CHUNK, :], vb[:, h * RET_DV:(h + 1) * RET_DV],
                     preferred_element_type=F32) for h in range(RET_HEADS)], axis=1)
        state = state_ref[...]
        cross = jnp.dot((q * xi_ref[...]).astype(BF16), state.astype(BF16), preferred_element_type=F32)
        kz = (k * zeta_ref[...]).astype(BF16)
        kv = lax.dot_general(kz, vb, (((0,), (0,)), ((), ())), preferred_element_type=F32)
        state_ref[...] = state * decay_ref[...] + kv * bd_ref[...]
        ret = inner + cross
        gate = proj_ref[sl, G0:U0]
        outs = []
        for h in range(RET_HEADS):
            o = ret[:, h * RET_DV:(h + 1) * RET_DV]
            mu = jnp.mean(o, axis=-1, keepdims=True)
            oc = o - mu
            var = jnp.mean(oc * oc, axis=-1, keepdims=True)
            outs.append(oc * lax.rsqrt(var + GN_EPS))
        gn = jnp.concatenate(outs, axis=1)
        mixin_ref[sl, 0:RET_WIDTH] = (gate * jax.nn.sigmoid(gate) * gn).astype(BF16)
        return 0
    lax.fori_loop(0, n_chunks, ret_body, 0, unroll=True)

    uext_ref[0:N_META, :] = tail_ref[...]
    uext_ref[N_META:, :] = proj_ref[:, U0:]
    tail_ref[...] = uext_ref[rows:, :]
    for g, w in enumerate(POOL_WINDOWS):
        lanes = slice(g * POOL_CH, (g + 1) * POOL_CH)
        e = uext_ref[:, lanes]
        acc = e
        shift = 1
        while shift < w:
            acc = acc + pltpu.roll(acc, shift, 0)
            shift *= 2
        pooled = acc[N_META:, :] * (1.0 / w) - e[N_META:, :]
        mixed = jnp.dot(pooled.astype(BF16), poolw_ref[g], preferred_element_type=F32) + poolb_ref[:, lanes]
        mixin_ref[:, RET_WIDTH + g * POOL_CH:RET_WIDTH + (g + 1) * POOL_CH] = (mixed * pools_ref[:, lanes]).astype(BF16)

    proj_ref[:, 0:D_MODEL] = jnp.dot(mixin_ref[...], wout_ref[...], preferred_element_type=F32)

    @pl.when(step >= 1)
    def _():
        scatter_wait(slot)

    keep = hkeep_ref.at[slot]

    def ln1_body(c, _):
        sl = pl.ds(pl.multiple_of(c * CHUNK, CHUNK), CHUNK)
        h1 = _layer_norm(ALPHA * h0_ref[sl, :] + proj_ref[sl, 0:D_MODEL], ln1g_ref[...], ln1b_ref[...], LN_EPS)
        h1_ref[sl, :] = h1
        _store_packed_tokens(keep, c * CHUNK, h1)
        h0b_ref[sl, :] = h1.astype(BF16)
        return 0
    lax.fori_loop(0, n_chunks, ln1_body, 0, unroll=True)

    logits = lax.dot_general(wr_ref[...], h0b_ref[...], (((1,), (1,)), ((), ())), preferred_element_type=F32)
    logits = logits + br_ref[...]
    gl = logits[0:N_GROUPS, :]
    gmax = jnp.max(gl, axis=0, keepdims=True)
    g_p = 1.0 / jnp.sum(jnp.exp(gl - gmax), axis=0, keepdims=True)
    grow = lax.broadcasted_iota(jnp.int32, gl.shape, 0)
    g_idx = jnp.min(jnp.where(gl == gmax, grow, N_GROUPS), axis=0, keepdims=True)
    sel = logits[8:8 + EXPERTS_PER_GROUP, :]
    for g in range(1, N_GROUPS):
        sel = jnp.where(g_idx == g, logits[8 + g * EXPERTS_PER_GROUP:8 + (g + 1) * EXPERTS_PER_GROUP, :], sel)
    erow = lax.broadcasted_iota(jnp.int32, sel.shape, 0)
    m1 = jnp.max(sel, axis=0, keepdims=True)
    i1 = jnp.min(jnp.where(sel == m1, erow, EXPERTS_PER_GROUP), axis=0, keepdims=True)
    sel2 = jnp.where(erow == i1, -jnp.inf, sel)
    m2 = jnp.max(sel2, axis=0, keepdims=True)
    i2 = jnp.min(jnp.where(sel2 == m2, erow, EXPERTS_PER_GROUP), axis=0, keepdims=True)
    e2 = jnp.exp(m2 - m1)
    w1 = 1.0 / (1.0 + e2)
    w2 = e2 / (1.0 + e2)
    pe0 = g_idx * EXPERTS_PER_GROUP + i1
    pe1 = g_idx * EXPERTS_PER_GROUP + i2
    pw_ref[...] = jnp.concatenate([g_p * w1, g_p * w2], axis=0)
    xrow = lax.broadcasted_iota(jnp.int32, (N_EXPERTS, rows), 0)
    oh0 = xrow == pe0
    oh1 = xrow == pe1
    oh = jnp.where(jnp.logical_or(oh0, oh1), 1.0, 0.0)
    carry = carry_ref[...]
    count = carry[:, 0:1]
    prefix = jnp.dot(oh.astype(BF16), tri_ref[...], preferred_element_type=F32) + count

    inv_rows = 1.0 / EXPERT_ROWS
    added = jnp.sum(oh, axis=1, keepdims=True)
    blocks_old = jnp.floor((count + (EXPERT_ROWS - 1)) * inv_rows)
    blocks_new = jnp.floor((count + added + (EXPERT_ROWS - 1)) * inv_rows)
    need = blocks_new - blocks_old
    er = lax.broadcasted_iota(jnp.int32, (N_EXPERTS, N_EXPERTS), 0)
    ec = lax.broadcasted_iota(jnp.int32, (N_EXPERTS, N_EXPERTS), 1)
    earlier = jnp.where(ec < er, 1.0, 0.0)
    need_lanes = jnp.broadcast_to(need, (N_EXPERTS, LANES))
    before = jnp.dot(earlier.astype(BF16), need_lanes.astype(BF16), preferred_element_type=F32)[:, 0:1]
    nalloc = nalloc_ref[0:1, 0:1]
    new_blk = nalloc + before
    cur_blk = curblk_ref[:, 0:1]
    ordinal = jnp.floor(prefix * inv_rows)
    blk = jnp.where(ordinal < blocks_old, cur_blk, new_blk)
    row_in_buf = blk * EXPERT_ROWS + (prefix - ordinal * EXPERT_ROWS)
    dest0 = jnp.sum(jnp.where(oh0, row_in_buf, 0.0), axis=0, keepdims=True)
    dest1 = jnp.sum(jnp.where(oh1, row_in_buf, 0.0), axis=0, keepdims=True)
    dest = jnp.concatenate([dest0, dest1], axis=0).astype(jnp.int32)
    dest_ref[...] = dest
    dvm_ref[...] = dest
    pltpu.make_async_copy(dvm_ref, dsm_ref, idx_sem).start()

    got_new = need > 0.0
    blk_lane = lax.broadcasted_iota(jnp.int32, (N_EXPERTS, BLOCK_TABLE_LANES), 1).astype(F32)
    erow_f = lax.broadcasted_iota(jnp.int32, (N_EXPERTS, BLOCK_TABLE_LANES), 0).astype(F32)
    owner = jnp.max(jnp.where(jnp.logical_and(got_new, blk_lane == new_blk), erow_f, -1.0), axis=0, keepdims=True)
    tab = jnp.where(owner >= 0.0, owner, tab_ref[0:1, :])
    tab_ref[...] = jnp.broadcast_to(tab, tab_ref.shape)
    cur_blk = jnp.where(got_new, new_blk, cur_blk)
    curblk_ref[...] = jnp.broadcast_to(cur_blk, curblk_ref.shape)
    nalloc_ref[...] = jnp.broadcast_to(nalloc + jnp.sum(need, axis=0, keepdims=True), nalloc_ref.shape)
    carry = carry + added
    carry_ref[...] = carry
    cnt_ref[...] = carry.astype(jnp.int32)
    cur_ref[...] = jnp.broadcast_to(cur_blk, cur_ref.shape).astype(jnp.int32)
    table_ref[...] = jnp.broadcast_to(tab, table_ref.shape).astype(jnp.int32)

    @pl.when(step == n_steps - 1)
    def _():
        pltpu.make_async_copy(dvm_ref, dsm_ref, idx_sem).wait()

        def last_rows(t, _):
            for s in range(TOP_K):
                pltpu.make_async_copy(_packed_rows(keep, t), _packed_rows(buf_ref, dsm_ref[s, t]), scat_sem.at[slot]).start()
            return 0
        lax.fori_loop(0, rows, last_rows, 0)

        csm_cnt = pltpu.make_async_copy(cnt_ref, csm_ref.at[0], zsem)
        csm_cur = pltpu.make_async_copy(cur_ref, csm_ref.at[1], zsem)
        csm_cnt.start()
        csm_cur.start()
        csm_cnt.wait()
        csm_cur.wait()

        def go(copy, do_start):
            if do_start:
                copy.start()
            else:
                copy.wait()

        def tail_fill(do_start):
            def body(e, carry):
                cnt_e = csm_ref[0, e, 0]
                gap = (EXPERT_ROWS - lax.rem(cnt_e, EXPERT_ROWS)) % EXPERT_ROWS
                first = csm_ref[1, e, 0] * EXPERT_ROWS + (EXPERT_ROWS - gap)
                for bit in range(EXPERT_ROWS.bit_length() - 1):
                    run = 1 << bit

                    @pl.when((gap >> bit) & 1 == 1)
                    def _():
                        go(pltpu.make_async_copy(_packed_rows(zero_ref, 0, run),
                                                 _packed_rows(buf_ref, first + (gap & (run - 1)), run), zsem), do_start)
                return carry
            return body

        def unused_fill(do_start):
            def body(b, carry):
                go(pltpu.make_async_copy(zero_ref, _packed_rows(buf_ref, b * EXPERT_ROWS, EXPERT_ROWS), zsem), do_start)
                return carry
            return body

        handed_out = jnp.int32(0)
        for e in range(N_EXPERTS):
            handed_out = handed_out + (csm_ref[0, e, 0] + (EXPERT_ROWS - 1)) // EXPERT_ROWS
        for do_start in (True, False):
            lax.fori_loop(0, N_EXPERTS, tail_fill(do_start), 0)
            lax.fori_loop(handed_out, n_blocks, unused_fill(do_start), 0)

        scatter_wait(prev)
        scatter_wait(slot)


def _expert_kernel(first_ref, nblk_ref, order_ref, wg_ref, wu_ref, wd_ref, x_ref, y_ref,
                   wgb_ref, wub_ref, wdb_ref, xbuf_ref, ybuf_ref, xsem, ysem):
    e = pl.program_id(0)
    nb = nblk_ref[e]
    first = first_ref[e]

    def x_copy(k, b):
        return pltpu.make_async_copy(_packed_rows(x_ref, order_ref[first + k] * EXPERT_ROWS, EXPERT_ROWS),
                                     xbuf_ref.at[b], xsem.at[b])

    def y_copy(k, b):
        return pltpu.make_async_copy(ybuf_ref.at[b], _token_rows(y_ref, order_ref[first + k] * EXPERT_ROWS, EXPERT_ROWS),
                                     ysem.at[b])

    @pl.when(nb > 0)
    def _():
        x_copy(0, 0).start()
        wgb_ref[...] = wg_ref[...].astype(BF16)
        wub_ref[...] = wu_ref[...].astype(BF16)
        wdb_ref[...] = wd_ref[...].astype(BF16)

        def block(k, carry):
            b = lax.rem(k, 2)
            x_copy(k, b).wait()

            @pl.when(k + 1 < nb)
            def _():
                x_copy(k + 1, 1 - b).start()

            @pl.when(k >= 2)
            def _():
                y_copy(k - 2, b).wait()

            x = _load_packed_tokens(xbuf_ref.at[b], 0, EXPERT_ROWS)
            gate = jnp.dot(x, wgb_ref[...], preferred_element_type=F32)
            up = jnp.dot(x, wub_ref[...], preferred_element_type=F32)
            act = (gate * jax.nn.sigmoid(gate) * up).astype(BF16)
            _store_token_tiles(ybuf_ref.at[b], 0, jnp.dot(act, wdb_ref[...], preferred_element_type=F32))
            y_copy(k, b).start()
            return carry
        lax.fori_loop(0, nb, block, 0)

        @pl.when(nb >= 2)
        def _():
            y_copy(nb - 2, lax.rem(nb, 2)).wait()
        y_copy(nb - 1, lax.rem(nb + 1, 2)).wait()

    @pl.when(e == pl.num_programs(0) - 1)
    def _():
        n_used = first + nb
        n_blocks = order_ref.shape[0]
        ybuf_ref[0] = jnp.zeros(ybuf_ref.shape[1:], F32)

        def zero_block(do_start):
            def body(i, carry):
                cp = pltpu.make_async_copy(ybuf_ref.at[0], _token_rows(y_ref, order_ref[i] * EXPERT_ROWS, EXPERT_ROWS),
                                           ysem.at[0])
                if do_start:
                    cp.start()
                else:
                    cp.wait()
                return carry
            return body
        lax.fori_loop(n_used, n_blocks, zero_block(True), 0)
        lax.fori_loop(n_used, n_blocks, zero_block(False), 0)


def _combine_kernel(dcur_ref, dnext_ref, h_ref, pw_ref, g_ref, b_ref, y_ref, o_ref, yb0, yb1, yb2, yb3, sems):
    part = COMBINE_PART_ROWS
    bufs = (yb0, yb1, yb2, yb3)
    step = pl.program_id(0)
    last = pl.num_programs(0) - 1

    def copy(src_tok, ybuf, t, slot, sem):
        return pltpu.make_async_copy(_token_rows(y_ref, src_tok), _token_rows(ybuf.at[slot], t), sem)

    def issue(dref, col0, ybuf, sem):
        for t0 in range(0, part, INDEX_GROUP):
            src = [[dref[slot, col0 + t0 + k] for slot in range(TOP_K)] for k in range(INDEX_GROUP)]
            for k in range(INDEX_GROUP):
                for slot in range(TOP_K):
                    copy(src[k][slot], ybuf, t0 + k, slot, sem).start(priority=slot % 2)

    def wait(ybuf, sem):
        for slot in range(TOP_K):
            pltpu.make_async_copy(_token_rows(y_ref, 0, part), ybuf.at[slot], sem).wait()

    def finish(ybuf, r0):
        rows = pl.ds(r0, part)
        pw = pw_ref[rows, :]
        y = pw[:, 0:1] * _load_token_tiles(ybuf.at[0], 0, part) + pw[:, 1:2] * _load_token_tiles(ybuf.at[1], 0, part)
        o_ref[rows, :] = _layer_norm(ALPHA * h_ref[rows, :] + y, g_ref[...], b_ref[...], LN_EPS)

    @pl.when(step == 0)
    def _():
        for p in range(2):
            def first(t, _, p=p):
                for slot in range(TOP_K):
                    copy(dcur_ref[slot, p * part + t], bufs[p], t, slot, sems.at[p]).start()
                return 0
            lax.fori_loop(0, part, first, 0)

    for p in range(COMBINE_PHASES):
        ahead = (p + 2) % COMBINE_PHASES
        wait(bufs[p], sems.at[p])
        if p + 2 < COMBINE_PHASES:
            issue(dcur_ref, (p + 2) * part, bufs[ahead], sems.at[ahead])
        else:
            issue(dnext_ref, ahead * part, bufs[ahead], sems.at[ahead])
        finish(bufs[p], p * part)

    @pl.when(step == last)
    def _():
        wait(bufs[0], sems.at[0])
        wait(bufs[1], sems.at[1])


def _tables(seq):
    log_g = jnp.log1p(-jnp.power(2.0, -5.0 - jnp.arange(RET_HEADS, dtype=F32)))
    i = jnp.arange(CHUNK, dtype=F32)
    rel = i[:, None] - i[None, :]
    dmask = jnp.where(rel[None] >= 0, jnp.exp(jnp.maximum(rel, 0.0)[None] * log_g[:, None, None]), 0.0)
    dmask = dmask.reshape(RET_HEADS * CHUNK, CHUNK)
    lg_lane = jnp.repeat(log_g, RET_DK)
    xi = jnp.exp((i + 1)[:, None] * lg_lane[None, :])
    zeta = jnp.exp((CHUNK - 1 - i)[:, None] * lg_lane[None, :])
    decay = jnp.broadcast_to(jnp.exp(CHUNK * lg_lane)[:, None], (RET_QK_WIDTH, RET_WIDTH))
    bd = (jnp.arange(RET_QK_WIDTH)[:, None] // RET_DK == jnp.arange(RET_WIDTH)[None, :] // RET_DV).astype(F32)
    zeta_meta = zeta[CHUNK - N_META:, :]
    half = RET_DK // 2
    inv = ROPE_BASE ** (-jnp.arange(half, dtype=F32) / half)
    pos = jnp.arange(N_META + seq, dtype=F32)
    ang = pos[:, None] * inv[None, :]
    cos = jnp.tile(jnp.cos(ang), (1, 2 * RET_HEADS))
    sin = jnp.tile(jnp.concatenate([-jnp.sin(ang), jnp.sin(ang)], axis=1), (1, RET_HEADS))
    return dict(dmask=dmask, xi=xi, zeta=zeta, decay=decay, bd=bd, zeta_meta=zeta_meta,
                cos_meta=cos[:N_META], sin_meta=sin[:N_META], cos=cos[N_META:], sin=sin[N_META:])


def _full(shape):
    return pl.BlockSpec(shape, lambda *_: (0,) * len(shape))


def kernel(x, meta_tokens, ln_emb_g, ln_emb_b, w_in, pool_w, pool_b, pool_scale, w_out, ln1_g, ln1_b, router_group_w, router_group_b, router_expert_w, router_expert_b, expert_w_gate, expert_w_up, expert_w_down, ln2_g, ln2_b):
    batch, seq, d = x.shape
    assert d == D_MODEL and seq % MIX_ROWS == 0 and MIX_ROWS <= EXPERT_ROWS
    n_tok = batch * seq
    t_blocks = seq // MIX_ROWS
    tb = _tables(seq)

    row = lambda a: a.reshape(1, -1).astype(F32)
    win_b = w_in[0].astype(BF16)
    wout_b = w_out[0].astype(BF16)
    poolw_b = pool_w[0].astype(BF16)
    wr = jnp.zeros((ROUTER_ROWS, D_MODEL), F32)
    wr = wr.at[0:N_GROUPS].set(router_group_w[0].T).at[8:8 + N_EXPERTS].set(router_expert_w[0].T).astype(BF16)
    br = jnp.zeros((ROUTER_ROWS, 1), F32)
    br = br.at[0:N_GROUPS, 0].set(router_group_b[0]).at[8:8 + N_EXPERTS, 0].set(router_expert_b[0])

    s0, tail0 = pl.pallas_call(
        _meta_kernel,
        out_shape=(jax.ShapeDtypeStruct((RET_QK_WIDTH, RET_WIDTH), F32), jax.ShapeDtypeStruct((N_META, POOL_WIDTH), F32)),
        name="meta_prep",
    )(meta_tokens.astype(F32), row(ln_emb_g), row(ln_emb_b), win_b, tb["cos_meta"], tb["sin_meta"], tb["zeta_meta"], tb["bd"])

    tok_spec = pl.BlockSpec((None, MIX_ROWS, D_MODEL), lambda b, j: (b, j, 0))
    pair_spec = pl.BlockSpec((TOP_K, MIX_ROWS), lambda b, j: (0, b * t_blocks + j))
    rope_spec = pl.BlockSpec((MIX_ROWS, RET_QK_WIDTH), lambda b, j: (j, 0))
    n_blocks = (n_tok * TOP_K) // EXPERT_ROWS + N_EXPERTS
    assert n_blocks + 2 <= BLOCK_TABLE_LANES
    tile_block = (EXPERT_ROWS * TOKEN_SUBLANES, LANES)
    packed_block = (EXPERT_ROWS * PACKED_SUBLANES, LANES)
    sorted_shape = (n_blocks * EXPERT_ROWS * TOKEN_SUBLANES, LANES)
    buf_shape = ((n_blocks + 2) * EXPERT_ROWS * PACKED_SUBLANES, LANES)
    stage_block = (MIX_ROWS * PACKED_SUBLANES, LANES)
    h1, dest, pair_w, blk_table, buf = pl.pallas_call(
        _mixer_kernel,
        grid=(batch, t_blocks),
        in_specs=[tok_spec, _full((1, D_MODEL)), _full((1, D_MODEL)), _full((D_MODEL, IN_COLS)), _full((D_MODEL, D_MODEL)),
                  _full((POOL_GROUPS, POOL_CH, POOL_CH)), _full((1, POOL_WIDTH)), _full((1, POOL_WIDTH)),
                  _full((1, D_MODEL)), _full((1, D_MODEL)), rope_spec, rope_spec,
                  _full((RET_HEADS * CHUNK, CHUNK)), _full((CHUNK, RET_QK_WIDTH)), _full((CHUNK, RET_QK_WIDTH)),
                  _full((RET_QK_WIDTH, RET_WIDTH)), _full((RET_QK_WIDTH, RET_WIDTH)),
                  _full((RET_QK_WIDTH, RET_WIDTH)), _full((N_META, POOL_WIDTH)),
                  _full((ROUTER_ROWS, D_MODEL)), _full((ROUTER_ROWS, 1))],
        out_specs=[tok_spec, pair_spec, pair_spec,
                   _full((8, BLOCK_TABLE_LANES)),
                   pl.BlockSpec(memory_space=pl.ANY)],
        out_shape=[jax.ShapeDtypeStruct((batch, seq, D_MODEL), F32),
                   jax.ShapeDtypeStruct((TOP_K, n_tok), jnp.int32),
                   jax.ShapeDtypeStruct((TOP_K, n_tok), F32),
                   jax.ShapeDtypeStruct((8, BLOCK_TABLE_LANES), jnp.int32),
                   jax.ShapeDtypeStruct(buf_shape, jnp.uint32)],
        scratch_shapes=[pltpu.VMEM((RET_QK_WIDTH, RET_WIDTH), F32),
                        pltpu.VMEM((N_META, POOL_WIDTH), F32),
                        pltpu.VMEM((MIX_ROWS + N_META, POOL_WIDTH), F32),
                        pltpu.VMEM((MIX_ROWS, IN_COLS), F32),
                        pltpu.VMEM((MIX_ROWS, D_MODEL), F32),
                        pltpu.VMEM((MIX_ROWS, D_MODEL), BF16),
                        pltpu.VMEM((MIX_ROWS, D_MODEL), BF16),
                        pltpu.VMEM((MIX_ROWS, MIX_ROWS), BF16),
                        pltpu.VMEM((N_EXPERTS, LANES), F32),
                        pltpu.VMEM((N_EXPERTS, LANES), F32),
                        pltpu.VMEM((8, LANES), F32),
                        pltpu.VMEM((8, BLOCK_TABLE_LANES), F32),
                        pltpu.VMEM((2,) + stage_block, jnp.uint32),
                        pltpu.VMEM(packed_block, jnp.uint32),
                        pltpu.VMEM((TOP_K, MIX_ROWS), jnp.int32),
                        pltpu.SMEM((TOP_K, MIX_ROWS), jnp.int32),
                        pltpu.VMEM((N_EXPERTS, LANES), jnp.int32),
                        pltpu.VMEM((N_EXPERTS, LANES), jnp.int32),
                        pltpu.SMEM((2, N_EXPERTS, LANES), jnp.int32),
                        pltpu.SemaphoreType.DMA((2,)),
                        pltpu.SemaphoreType.DMA, pltpu.SemaphoreType.DMA],
        compiler_params=pltpu.CompilerParams(dimension_semantics=("arbitrary", "arbitrary"),
                                             vmem_limit_bytes=VMEM_LIMIT),
        name="mixer",
    )(x, row(ln_emb_g), row(ln_emb_b), win_b, wout_b, poolw_b, row(pool_b[0]), row(pool_scale[0]),
      row(ln1_g[0]), row(ln1_b[0]), tb["cos"], tb["sin"], tb["dmask"], tb["xi"], tb["zeta"], tb["decay"], tb["bd"],
      s0, tail0, wr, br)

    bidx = jnp.arange(n_blocks, dtype=jnp.int32)
    owner = blk_table[0, :n_blocks]
    key = owner * BLOCK_TABLE_LANES + bidx
    place = jnp.sum(key[None, :] < key[:, None], axis=1)
    order = jnp.sum(jnp.where(place[None, :] == bidx[:, None], bidx[None, :], 0), axis=1).astype(jnp.int32)
    blocks_of = jnp.sum(owner[None, :] == jnp.arange(N_EXPERTS, dtype=jnp.int32)[:, None], axis=1).astype(jnp.int32)
    first_of = (jnp.cumsum(blocks_of) - blocks_of).astype(jnp.int32)

    y_sorted = pl.pallas_call(
        _expert_kernel,
        grid_spec=pltpu.PrefetchScalarGridSpec(
            num_scalar_prefetch=3,
            grid=(N_EXPERTS,),
            in_specs=[pl.BlockSpec((None, D_MODEL, D_EXPERT), lambda e, f, n, o: (e, 0, 0)),
                      pl.BlockSpec((None, D_MODEL, D_EXPERT), lambda e, f, n, o: (e, 0, 0)),
                      pl.BlockSpec((None, D_EXPERT, D_MODEL), lambda e, f, n, o: (e, 0, 0)),
                      pl.BlockSpec(memory_space=pl.ANY)],
            out_specs=pl.BlockSpec(memory_space=pl.ANY),
            scratch_shapes=[pltpu.VMEM((D_MODEL, D_EXPERT), BF16), pltpu.VMEM((D_MODEL, D_EXPERT), BF16),
                            pltpu.VMEM((D_EXPERT, D_MODEL), BF16),
                            pltpu.VMEM((2,) + packed_block, jnp.uint32), pltpu.VMEM((2,) + tile_block, F32),
                            pltpu.SemaphoreType.DMA((2,)), pltpu.SemaphoreType.DMA((2,))],
        ),
        out_shape=jax.ShapeDtypeStruct(sorted_shape, F32),
        compiler_params=pltpu.CompilerParams(dimension_semantics=("arbitrary",), vmem_limit_bytes=VMEM_LIMIT),
        name="experts",
    )(first_of, blocks_of, order, expert_w_gate[0], expert_w_up[0], expert_w_down[0], buf)

    comb_rows = COMBINE_PHASES * COMBINE_PART_ROWS
    comb_steps = n_tok // comb_rows
    assert n_tok % comb_rows == 0
    rows_spec = pl.BlockSpec((comb_rows, D_MODEL), lambda i: (i, 0))
    out = pl.pallas_call(
        _combine_kernel,
        grid=(comb_steps,),
        in_specs=[pl.BlockSpec((TOP_K, comb_rows), lambda i: (0, i), memory_space=pltpu.SMEM),
                  pl.BlockSpec((TOP_K, comb_rows), lambda i: (0, jnp.minimum(i + 1, comb_steps - 1)),
                               memory_space=pltpu.SMEM),
                  rows_spec,
                  pl.BlockSpec((comb_rows, TOP_K), lambda i: (i, 0)),
                  pl.BlockSpec((1, D_MODEL), lambda i: (0, 0)), pl.BlockSpec((1, D_MODEL), lambda i: (0, 0)),
                  pl.BlockSpec(memory_space=pl.ANY)],
        out_specs=rows_spec,
        out_shape=jax.ShapeDtypeStruct((n_tok, D_MODEL), F32),
        scratch_shapes=[pltpu.VMEM((TOP_K, COMBINE_PART_ROWS * TOKEN_SUBLANES, LANES), F32)] * COMBINE_PHASES
                       + [pltpu.SemaphoreType.DMA((COMBINE_PHASES,))],
        compiler_params=pltpu.CompilerParams(dimension_semantics=("arbitrary",), vmem_limit_bytes=VMEM_LIMIT),
        name="combine",
    )(dest, dest, h1.reshape(n_tok, D_MODEL), pair_w.T, row(ln2_g[0]), row(ln2_b[0]), y_sorted)
    return out.reshape(batch, seq, D_MODEL)
```

```python
import jax
import jax.numpy as jnp
from jax import lax
from jax.experimental import pallas as pl
from jax.experimental.pallas import tpu as pltpu

D_MODEL = 1024
DEPTH = 1
N_META = 16
RET_HEADS = 4
RET_WIDTH = D_MODEL // 2
RET_DV = RET_WIDTH // RET_HEADS
RET_DK = RET_DV // 2
RET_QK_WIDTH = RET_HEADS * RET_DK
CHUNK = 128
ROPE_BASE = 10000.0
POOL_WINDOWS = (2, 4, 8, 16)
POOL_GROUPS = len(POOL_WINDOWS)
POOL_WIDTH = D_MODEL // 2
POOL_CH = POOL_WIDTH // POOL_GROUPS
IN_COLS = 2 * RET_QK_WIDTH + 2 * RET_WIDTH + POOL_WIDTH
N_GROUPS = 4
EXPERTS_PER_GROUP = 8
N_EXPERTS = N_GROUPS * EXPERTS_PER_GROUP
D_EXPERT = D_MODEL // 2
TOP_K = 2
LN_EPS = 1e-5
GN_EPS = 1e-6
ALPHA = (2 * DEPTH) ** 0.25

Q0, K0, V0, G0, U0 = 0, RET_QK_WIDTH, 2 * RET_QK_WIDTH, 2 * RET_QK_WIDTH + RET_WIDTH, 2 * RET_QK_WIDTH + 2 * RET_WIDTH

MIX_ROWS = 512
EXPERT_ROWS = 512
ROUTER_ROWS = 40
BLOCK_TABLE_LANES = 384
COMBINE_PHASES = 4
COMBINE_PART_ROWS = 128
INDEX_GROUP = 8
VMEM_LIMIT = 56 * 1024 * 1024
LANES = 128
TOKEN_SUBLANES = D_MODEL // LANES
PACKED_SUBLANES = TOKEN_SUBLANES // 2

F32 = jnp.float32
BF16 = jnp.bfloat16


def _layer_norm(x, g, b, eps):
    mu = jnp.mean(x, axis=-1, keepdims=True)
    xc = x - mu
    var = jnp.mean(xc * xc, axis=-1, keepdims=True)
    return xc * lax.rsqrt(var + eps) * g + b


def _rotary(z, cos, sin_signed, first_half):
    partner = jnp.where(first_half, pltpu.roll(z, RET_QK_WIDTH - RET_DK // 2, 1), pltpu.roll(z, RET_DK // 2, 1))
    return z * cos + partner * sin_signed


def _store_token_tiles(ref, tok0, x):
    n = x.shape[0]
    for s in range(TOKEN_SUBLANES):
        ref[pl.ds(tok0 * TOKEN_SUBLANES + s, n, stride=TOKEN_SUBLANES), :] = x[:, s * LANES:(s + 1) * LANES]


def _load_token_tiles(ref, tok0, n):
    return jnp.concatenate(
        [ref[pl.ds(tok0 * TOKEN_SUBLANES + s, n, stride=TOKEN_SUBLANES), :] for s in range(TOKEN_SUBLANES)], axis=1)


def _token_rows(ref, tok, n=1):
    return ref.at[pl.ds(pl.multiple_of(tok * TOKEN_SUBLANES, TOKEN_SUBLANES), n * TOKEN_SUBLANES), :]


def _store_packed_tokens(ref, tok0, x):
    n, half = x.shape[0], D_MODEL // 2
    lo = lax.bitcast_convert_type(x[:, :half].astype(BF16).astype(F32), jnp.uint32) >> 16
    hi = lax.bitcast_convert_type(x[:, half:].astype(BF16).astype(F32), jnp.uint32) & jnp.uint32(0xFFFF0000)
    words = lo | hi
    for s in range(PACKED_SUBLANES):
        ref[pl.ds(tok0 * PACKED_SUBLANES + s, n, stride=PACKED_SUBLANES), :] = words[:, s * LANES:(s + 1) * LANES]


def _load_packed_tokens(ref, tok0, n):
    words = [ref[pl.ds(tok0 * PACKED_SUBLANES + s, n, stride=PACKED_SUBLANES), :] for s in range(PACKED_SUBLANES)]
    lo = [lax.bitcast_convert_type(w << 16, F32).astype(BF16) for w in words]
    hi = [lax.bitcast_convert_type(w & jnp.uint32(0xFFFF0000), F32).astype(BF16) for w in words]
    return jnp.concatenate(lo + hi, axis=1)


def _packed_rows(ref, tok, n=1):
    return ref.at[pl.ds(pl.multiple_of(tok * PACKED_SUBLANES, PACKED_SUBLANES), n * PACKED_SUBLANES), :]


def _first_half_mask(rows):
    lane = lax.broadcasted_iota(jnp.int32, (rows, RET_QK_WIDTH), 1)
    return (lane % RET_DK) < (RET_DK // 2)


def _meta_kernel(meta_ref, g_ref, b_ref, win_ref, cos_ref, sin_ref, zeta_ref, bd_ref, s0_ref, tail_ref):
    h = _layer_norm(meta_ref[...], g_ref[...], b_ref[...], LN_EPS)
    proj = jnp.dot(h.astype(BF16), win_ref[...], preferred_element_type=F32)
    k = _rotary(proj[:, K0:V0], cos_ref[...], sin_ref[...], _first_half_mask(N_META)) * (RET_DK ** -0.5)
    kz = (k * zeta_ref[...]).astype(BF16)
    v = proj[:, V0:G0].astype(BF16)
    kv = lax.dot_general(kz, v, (((0,), (0,)), ((), ())), preferred_element_type=F32)
    s0_ref[...] = kv * bd_ref[...]
    tail_ref[...] = proj[:, U0:]


def _mixer_kernel(x_ref, lng_ref, lnb_ref, win_ref, wout_ref, poolw_ref, poolb_ref, pools_ref, ln1g_ref, ln1b_ref,
                  cos_ref, sin_ref, dmask_ref, xi_ref, zeta_ref, decay_ref, bd_ref, s0_ref, tail0_ref,
                  wr_ref, br_ref,
                  h1_ref, dest_ref, pw_ref, table_ref, buf_ref,
                  state_ref, tail_ref, uext_ref, proj_ref, h0_ref, h0b_ref, mixin_ref, tri_ref, carry_ref,
                  curblk_ref, nalloc_ref, tab_ref, hkeep_ref, zero_ref, dvm_ref, dsm_ref, cnt_ref, cur_ref, csm_ref,
                  scat_sem, idx_sem, zsem):
    rows = x_ref.shape[0]
    n_chunks = rows // CHUNK
    step = pl.program_id(0) * pl.num_programs(1) + pl.program_id(1)
    n_steps = pl.num_programs(0) * pl.num_programs(1)
    first_step = step == 0
    slot = lax.rem(step, 2)
    prev = 1 - slot
    n_blocks = buf_ref.shape[0] // (EXPERT_ROWS * PACKED_SUBLANES) - 2

    def scatter_wait(b):
        for _ in range(TOP_K):
            pltpu.make_async_copy(hkeep_ref.at[b], _packed_rows(buf_ref, 0, rows), scat_sem.at[b]).wait()

    @pl.when(first_step)
    def _():
        r = lax.broadcasted_iota(jnp.int32, (rows, rows), 0)
        c = lax.broadcasted_iota(jnp.int32, (rows, rows), 1)
        tri_ref[...] = jnp.where(r < c, 1.0, 0.0).astype(BF16)
        carry_ref[...] = jnp.zeros_like(carry_ref)
        curblk_ref[...] = jnp.full(curblk_ref.shape, -1.0, F32)
        nalloc_ref[...] = jnp.zeros_like(nalloc_ref)
        tab_ref[...] = jnp.full(tab_ref.shape, float(N_EXPERTS), F32)
        zero_ref[...] = jnp.zeros_like(zero_ref)
        hkeep_ref[1] = jnp.zeros(hkeep_ref.shape[1:], hkeep_ref.dtype)

        def spare(t, _):
            for s in range(TOP_K):
                dsm_ref[s, t] = (n_blocks + s) * EXPERT_ROWS + t
            return 0
        lax.fori_loop(0, rows, spare, 0)

    @pl.when(pl.program_id(1) == 0)
    def _():
        state_ref[...] = s0_ref[...]
        tail_ref[...] = tail0_ref[...]

    @pl.when(step > 0)
    def _():
        pltpu.make_async_copy(dvm_ref, dsm_ref, idx_sem).wait()

    def scatter_previous(t_lo, t_hi):
        for t0 in range(t_lo, t_hi, INDEX_GROUP):
            dst = [[dsm_ref[s, t0 + k] for s in range(TOP_K)] for k in range(INDEX_GROUP)]
            for k in range(INDEX_GROUP):
                for s in range(TOP_K):
                    pltpu.make_async_copy(_packed_rows(hkeep_ref.at[prev], t0 + k), _packed_rows(buf_ref, dst[k][s]),
                                          scat_sem.at[prev]).start(priority=s % 2)

    def ln_body(c, _):
        sl = pl.ds(pl.multiple_of(c * CHUNK, CHUNK), CHUNK)
        h0 = _layer_norm(x_ref[sl, :], lng_ref[...], lnb_ref[...], LN_EPS)
        h0_ref[sl, :] = h0
        h0b_ref[sl, :] = h0.astype(BF16)
        return 0
    lax.fori_loop(0, n_chunks, ln_body, 0, unroll=True)

    proj_ref[...] = jnp.dot(h0b_ref[...], win_ref[...], preferred_element_type=F32)
    scatter_previous(0, rows)

    first_half = _first_half_mask(CHUNK)
    head_of_lane = lax.broadcasted_iota(jnp.int32, (CHUNK, RET_QK_WIDTH), 1) // RET_DK

    def ret_body(c, _):
        sl = pl.ds(pl.multiple_of(c * CHUNK, CHUNK), CHUNK)
        cos = cos_ref[sl, :]
        sin = sin_ref[sl, :]
        q = _rotary(proj_ref[sl, Q0:K0], cos, sin, first_half)
        k = _rotary(proj_ref[sl, K0:V0], cos, sin, first_half) * (RET_DK ** -0.5)
        qb = q.astype(BF16)
        kb = k.astype(BF16)
        vb = proj_ref[sl, V0:G0].astype(BF16)
        q_heads = jnp.concatenate(
            [jnp.where(head_of_lane == h, qb, jnp.zeros_like(qb)) for h in range(RET_HEADS)], axis=0)
        scores = lax.dot_general(q_heads, kb, (((1,), (1,)), ((), ())), preferred_element_type=F32)
        p = (scores * dmask_ref[...]).astype(BF16)
        inner = jnp.concatenate(
            [jnp.dot(p[h * CHUNK:(h + 1) * CHUNK, :], vb[:, h * RET_DV:(h + 1) * RET_DV],
                     preferred_element_type=F32) for h in range(RET_HEADS)], axis=1)
        state = state_ref[...]
        cross = jnp.dot((q * xi_ref[...]).astype(BF16), state.astype(BF16), preferred_element_type=F32)
        kz = (k * zeta_ref[...]).astype(BF16)
        kv = lax.dot_general(kz, vb, (((0,), (0,)), ((), ())), preferred_element_type=F32)
        state_ref[...] = state * decay_ref[...] + kv * bd_ref[...]
        ret = inner + cross
        gate = proj_ref[sl, G0:U0]
        outs = []
        for h in range(RET_HEADS):
            o = ret[:, h * RET_DV:(h + 1) * RET_DV]
            mu = jnp.mean(o, axis=-1, keepdims=True)
            oc = o - mu
            var = jnp.mean(oc * oc, axis=-1, keepdims=True)
            outs.append(oc * lax.rsqrt(var + GN_EPS))
        gn = jnp.concatenate(outs, axis=1)
        mixin_ref[sl, 0:RET_WIDTH] = (gate * jax.nn.sigmoid(gate) * gn).astype(BF16)
        return 0
    lax.fori_loop(0, n_chunks, ret_body, 0, unroll=True)

    uext_ref[0:N_META, :] = tail_ref[...]
    uext_ref[N_META:, :] = proj_ref[:, U0:]
    tail_ref[...] = uext_ref[rows:, :]
    for g, w in enumerate(POOL_WINDOWS):
        lanes = slice(g * POOL_CH, (g + 1) * POOL_CH)
        e = uext_ref[:, lanes]
        acc = e
        shift = 1
        while shift < w:
            acc = acc + pltpu.roll(acc, shift, 0)
            shift *= 2
        pooled = acc[N_META:, :] * (1.0 / w) - e[N_META:, :]
        mixed = jnp.dot(pooled.astype(BF16), poolw_ref[g], preferred_element_type=F32) + poolb_ref[:, lanes]
        mixin_ref[:, RET_WIDTH + g * POOL_CH:RET_WIDTH + (g + 1) * POOL_CH] = (mixed * pools_ref[:, lanes]).astype(BF16)

    proj_ref[:, 0:D_MODEL] = jnp.dot(mixin_ref[...], wout_ref[...], preferred_element_type=F32)

    @pl.when(step >= 1)
    def _():
        scatter_wait(slot)

    keep = hkeep_ref.at[slot]

    def ln1_body(c, _):
        sl = pl.ds(pl.multiple_of(c * CHUNK, CHUNK), CHUNK)
        h1 = _layer_norm(ALPHA * h0_ref[sl, :] + proj_ref[sl, 0:D_MODEL], ln1g_ref[...], ln1b_ref[...], LN_EPS)
        h1_ref[sl, :] = h1
        _store_packed_tokens(keep, c * CHUNK, h1)
        h0b_ref[sl, :] = h1.astype(BF16)
        return 0
    lax.fori_loop(0, n_chunks, ln1_body, 0, unroll=True)

    logits = lax.dot_general(wr_ref[...], h0b_ref[...], (((1,), (1,)), ((), ())), preferred_element_type=F32)
    logits = logits + br_ref[...]
    gl = logits[0:N_GROUPS, :]
    gmax = jnp.max(gl, axis=0, keepdims=True)
    g_p = 1.0 / jnp.sum(jnp.exp(gl - gmax), axis=0, keepdims=True)
    grow = lax.broadcasted_iota(jnp.int32, gl.shape, 0)
    g_idx = jnp.min(jnp.where(gl == gmax, grow, N_GROUPS), axis=0, keepdims=True)
    sel = logits[8:8 + EXPERTS_PER_GROUP, :]
    for g in range(1, N_GROUPS):
        sel = jnp.where(g_idx == g, logits[8 + g * EXPERTS_PER_GROUP:8 + (g + 1) * EXPERTS_PER_GROUP, :], sel)
    erow = lax.broadcasted_iota(jnp.int32, sel.shape, 0)
    m1 = jnp.max(sel, axis=0, keepdims=True)
    i1 = jnp.min(jnp.where(sel == m1, erow, EXPERTS_PER_GROUP), axis=0, keepdims=True)
    sel2 = jnp.where(erow == i1, -jnp.inf, sel)
    m2 = jnp.max(sel2, axis=0, keepdims=True)
    i2 = jnp.min(jnp.where(sel2 == m2, erow, EXPERTS_PER_GROUP), axis=0, keepdims=True)
    e2 = jnp.exp(m2 - m1)
    w1 = 1.0 / (1.0 + e2)
    w2 = e2 / (1.0 + e2)
    pe0 = g_idx * EXPERTS_PER_GROUP + i1
    pe1 = g_idx * EXPERTS_PER_GROUP + i2
    pw_ref[...] = jnp.concatenate([g_p * w1, g_p * w2], axis=0)
    xrow = lax.broadcasted_iota(jnp.int32, (N_EXPERTS, rows), 0)
    oh0 = xrow == pe0
    oh1 = xrow == pe1
    oh = jnp.where(jnp.logical_or(oh0, oh1), 1.0, 0.0)
    carry = carry_ref[...]
    count = carry[:, 0:1]
    prefix = jnp.dot(oh.astype(BF16), tri_ref[...], preferred_element_type=F32) + count

    inv_rows = 1.0 / EXPERT_ROWS
    added = jnp.sum(oh, axis=1, keepdims=True)
    blocks_old = jnp.floor((count + (EXPERT_ROWS - 1)) * inv_rows)
    blocks_new = jnp.floor((count + added + (EXPERT_ROWS - 1)) * inv_rows)
    need = blocks_new - blocks_old
    er = lax.broadcasted_iota(jnp.int32, (N_EXPERTS, N_EXPERTS), 0)
    ec = lax.broadcasted_iota(jnp.int32, (N_EXPERTS, N_EXPERTS), 1)
    earlier = jnp.where(ec < er, 1.0, 0.0)
    need_lanes = jnp.broadcast_to(need, (N_EXPERTS, LANES))
    before = jnp.dot(earlier.astype(BF16), need_lanes.astype(BF16), preferred_element_type=F32)[:, 0:1]
    nalloc = nalloc_ref[0:1, 0:1]
    new_blk = nalloc + before
    cur_blk = curblk_ref[:, 0:1]
    ordinal = jnp.floor(prefix * inv_rows)
    blk = jnp.where(ordinal < blocks_old, cur_blk, new_blk)
    row_in_buf = blk * EXPERT_ROWS + (prefix - ordinal * EXPERT_ROWS)
    dest0 = jnp.sum(jnp.where(oh0, row_in_buf, 0.0), axis=0, keepdims=True)
    dest1 = jnp.sum(jnp.where(oh1, row_in_buf, 0.0), axis=0, keepdims=True)
    dest = jnp.concatenate([dest0, dest1], axis=0).astype(jnp.int32)
    dest_ref[...] = dest
    dvm_ref[...] = dest
    pltpu.make_async_copy(dvm_ref, dsm_ref, idx_sem).start()

    got_new = need > 0.0
    blk_lane = lax.broadcasted_iota(jnp.int32, (N_EXPERTS, BLOCK_TABLE_LANES), 1).astype(F32)
    erow_f = lax.broadcasted_iota(jnp.int32, (N_EXPERTS, BLOCK_TABLE_LANES), 0).astype(F32)
    owner = jnp.max(jnp.where(jnp.logical_and(got_new, blk_lane == new_blk), erow_f, -1.0), axis=0, keepdims=True)
    tab = jnp.where(owner >= 0.0, owner, tab_ref[0:1, :])
    tab_ref[...] = jnp.broadcast_to(tab, tab_ref.shape)
    cur_blk = jnp.where(got_new, new_blk, cur_blk)
    curblk_ref[...] = jnp.broadcast_to(cur_blk, curblk_ref.shape)
    nalloc_ref[...] = jnp.broadcast_to(nalloc + jnp.sum(need, axis=0, keepdims=True), nalloc_ref.shape)
    carry = carry + added
    carry_ref[...] = carry
    cnt_ref[...] = carry.astype(jnp.int32)
    cur_ref[...] = jnp.broadcast_to(cur_blk, cur_ref.shape).astype(jnp.int32)
    table_ref[...] = jnp.broadcast_to(tab, table_ref.shape).astype(jnp.int32)

    @pl.when(step == n_steps - 1)
    def _():
        pltpu.make_async_copy(dvm_ref, dsm_ref, idx_sem).wait()

        def last_rows(t, _):
            for s in range(TOP_K):
                pltpu.make_async_copy(_packed_rows(keep, t), _packed_rows(buf_ref, dsm_ref[s, t]), scat_sem.at[slot]).start()
            return 0
        lax.fori_loop(0, rows, last_rows, 0)

        csm_cnt = pltpu.make_async_copy(cnt_ref, csm_ref.at[0], zsem)
        csm_cur = pltpu.make_async_copy(cur_ref, csm_ref.at[1], zsem)
        csm_cnt.start()
        csm_cur.start()
        csm_cnt.wait()
        csm_cur.wait()

        def go(copy, do_start):
            if do_start:
                copy.start()
            else:
                copy.wait()

        def tail_fill(do_start):
            def body(e, carry):
                cnt_e = csm_ref[0, e, 0]
                gap = (EXPERT_ROWS - lax.rem(cnt_e, EXPERT_ROWS)) % EXPERT_ROWS
                first = csm_ref[1, e, 0] * EXPERT_ROWS + (EXPERT_ROWS - gap)
                for bit in range(EXPERT_ROWS.bit_length() - 1):
                    run = 1 << bit

                    @pl.when((gap >> bit) & 1 == 1)
                    def _():
                        go(pltpu.make_async_copy(_packed_rows(zero_ref, 0, run),
                                                 _packed_rows(buf_ref, first + (gap & (run - 1)), run), zsem), do_start)
                return carry
            return body

        def unused_fill(do_start):
            def body(b, carry):
                go(pltpu.make_async_copy(zero_ref, _packed_rows(buf_ref, b * EXPERT_ROWS, EXPERT_ROWS), zsem), do_start)
                return carry
            return body

        handed_out = jnp.int32(0)
        for e in range(N_EXPERTS):
            handed_out = handed_out + (csm_ref[0, e, 0] + (EXPERT_ROWS - 1)) // EXPERT_ROWS
        for do_start in (True, False):
            lax.fori_loop(0, N_EXPERTS, tail_fill(do_start), 0)
            lax.fori_loop(handed_out, n_blocks, unused_fill(do_start), 0)

        scatter_wait(prev)
        scatter_wait(slot)


def _expert_kernel(be_ref, nv_ref, bi_ref, bo_ref, x_ref, wg_ref, wu_ref, wd_ref, y_ref, wgb_ref, wub_ref, wdb_ref):
    i = pl.program_id(0)
    nv = nv_ref[i]
    new_expert = jnp.logical_or(i == 0, be_ref[i] != be_ref[jnp.maximum(i - 1, 0)])

    @pl.when(new_expert)
    def _():
        wgb_ref[...] = wg_ref[...].astype(BF16)
        wub_ref[...] = wu_ref[...].astype(BF16)
        wdb_ref[...] = wd_ref[...].astype(BF16)

    @pl.when(nv == 0)
    def _():
        y_ref[...] = jnp.zeros_like(y_ref)

    @pl.when(nv > 0)
    def _():
        x = _load_packed_tokens(x_ref, 0, EXPERT_ROWS)
        gate = jnp.dot(x, wgb_ref[...], preferred_element_type=F32)
        up = jnp.dot(x, wub_ref[...], preferred_element_type=F32)
        act = (gate * jax.nn.sigmoid(gate) * up).astype(BF16)
        _store_token_tiles(y_ref, 0, jnp.dot(act, wdb_ref[...], preferred_element_type=F32))


def _combine_kernel(dcur_ref, dnext_ref, h_ref, pw_ref, g_ref, b_ref, y_ref, o_ref, yb0, yb1, yb2, yb3, sems):
    part = COMBINE_PART_ROWS
    bufs = (yb0, yb1, yb2, yb3)
    step = pl.program_id(0)
    last = pl.num_programs(0) - 1

    def copy(src_tok, ybuf, t, slot, sem):
        return pltpu.make_async_copy(_token_rows(y_ref, src_tok), _token_rows(ybuf.at[slot], t), sem)

    def issue(dref, col0, ybuf, sem):
        for t0 in range(0, part, INDEX_GROUP):
            src = [[dref[slot, col0 + t0 + k] for slot in range(TOP_K)] for k in range(INDEX_GROUP)]
            for k in range(INDEX_GROUP):
                for slot in range(TOP_K):
                    copy(src[k][slot], ybuf, t0 + k, slot, sem).start(priority=1)

    def wait(ybuf, sem):
        for slot in range(TOP_K):
            pltpu.make_async_copy(_token_rows(y_ref, 0, part), ybuf.at[slot], sem).wait()

    def finish(ybuf, r0):
        rows = pl.ds(r0, part)
        pw = pw_ref[rows, :]
        y = pw[:, 0:1] * _load_token_tiles(ybuf.at[0], 0, part) + pw[:, 1:2] * _load_token_tiles(ybuf.at[1], 0, part)
        o_ref[rows, :] = _layer_norm(ALPHA * h_ref[rows, :] + y, g_ref[...], b_ref[...], LN_EPS)

    @pl.when(step == 0)
    def _():
        for p in range(2):
            def first(t, _, p=p):
                for slot in range(TOP_K):
                    copy(dcur_ref[slot, p * part + t], bufs[p], t, slot, sems.at[p]).start()
                return 0
            lax.fori_loop(0, part, first, 0)

    for p in range(COMBINE_PHASES):
        ahead = (p + 2) % COMBINE_PHASES
        wait(bufs[p], sems.at[p])
        if p + 2 < COMBINE_PHASES:
            issue(dcur_ref, (p + 2) * part, bufs[ahead], sems.at[ahead])
        else:
            issue(dnext_ref, ahead * part, bufs[ahead], sems.at[ahead])
        finish(bufs[p], p * part)

    @pl.when(step == last)
    def _():
        wait(bufs[0], sems.at[0])
        wait(bufs[1], sems.at[1])


def _tables(seq):
    log_g = jnp.log1p(-jnp.power(2.0, -5.0 - jnp.arange(RET_HEADS, dtype=F32)))
    i = jnp.arange(CHUNK, dtype=F32)
    rel = i[:, None] - i[None, :]
    dmask = jnp.where(rel[None] >= 0, jnp.exp(jnp.maximum(rel, 0.0)[None] * log_g[:, None, None]), 0.0)
    dmask = dmask.reshape(RET_HEADS * CHUNK, CHUNK)
    lg_lane = jnp.repeat(log_g, RET_DK)
    xi = jnp.exp((i + 1)[:, None] * lg_lane[None, :])
    zeta = jnp.exp((CHUNK - 1 - i)[:, None] * lg_lane[None, :])
    decay = jnp.broadcast_to(jnp.exp(CHUNK * lg_lane)[:, None], (RET_QK_WIDTH, RET_WIDTH))
    bd = (jnp.arange(RET_QK_WIDTH)[:, None] // RET_DK == jnp.arange(RET_WIDTH)[None, :] // RET_DV).astype(F32)
    zeta_meta = zeta[CHUNK - N_META:, :]
    half = RET_DK // 2
    inv = ROPE_BASE ** (-jnp.arange(half, dtype=F32) / half)
    pos = jnp.arange(N_META + seq, dtype=F32)
    ang = pos[:, None] * inv[None, :]
    cos = jnp.tile(jnp.cos(ang), (1, 2 * RET_HEADS))
    sin = jnp.tile(jnp.concatenate([-jnp.sin(ang), jnp.sin(ang)], axis=1), (1, RET_HEADS))
    return dict(dmask=dmask, xi=xi, zeta=zeta, decay=decay, bd=bd, zeta_meta=zeta_meta,
                cos_meta=cos[:N_META], sin_meta=sin[:N_META], cos=cos[N_META:], sin=sin[N_META:])


def _full(shape):
    return pl.BlockSpec(shape, lambda *_: (0,) * len(shape))


def kernel(x, meta_tokens, ln_emb_g, ln_emb_b, w_in, pool_w, pool_b, pool_scale, w_out, ln1_g, ln1_b, router_group_w, router_group_b, router_expert_w, router_expert_b, expert_w_gate, expert_w_up, expert_w_down, ln2_g, ln2_b):
    batch, seq, d = x.shape
    assert d == D_MODEL and seq % MIX_ROWS == 0 and MIX_ROWS <= EXPERT_ROWS
    n_tok = batch * seq
    t_blocks = seq // MIX_ROWS
    tb = _tables(seq)

    row = lambda a: a.reshape(1, -1).astype(F32)
    win_b = w_in[0].astype(BF16)
    wout_b = w_out[0].astype(BF16)
    poolw_b = pool_w[0].astype(BF16)
    wr = jnp.zeros((ROUTER_ROWS, D_MODEL), F32)
    wr = wr.at[0:N_GROUPS].set(router_group_w[0].T).at[8:8 + N_EXPERTS].set(router_expert_w[0].T).astype(BF16)
    br = jnp.zeros((ROUTER_ROWS, 1), F32)
    br = br.at[0:N_GROUPS, 0].set(router_group_b[0]).at[8:8 + N_EXPERTS, 0].set(router_expert_b[0])

    s0, tail0 = pl.pallas_call(
        _meta_kernel,
        out_shape=(jax.ShapeDtypeStruct((RET_QK_WIDTH, RET_WIDTH), F32), jax.ShapeDtypeStruct((N_META, POOL_WIDTH), F32)),
        name="meta_prep",
    )(meta_tokens.astype(F32), row(ln_emb_g), row(ln_emb_b), win_b, tb["cos_meta"], tb["sin_meta"], tb["zeta_meta"], tb["bd"])

    tok_spec = pl.BlockSpec((None, MIX_ROWS, D_MODEL), lambda b, j: (b, j, 0))
    pair_spec = pl.BlockSpec((TOP_K, MIX_ROWS), lambda b, j: (0, b * t_blocks + j))
    rope_spec = pl.BlockSpec((MIX_ROWS, RET_QK_WIDTH), lambda b, j: (j, 0))
    n_blocks = (n_tok * TOP_K) // EXPERT_ROWS + N_EXPERTS
    assert n_blocks + 2 <= BLOCK_TABLE_LANES
    tile_block = (EXPERT_ROWS * TOKEN_SUBLANES, LANES)
    packed_block = (EXPERT_ROWS * PACKED_SUBLANES, LANES)
    sorted_shape = (n_blocks * EXPERT_ROWS * TOKEN_SUBLANES, LANES)
    buf_shape = ((n_blocks + 2) * EXPERT_ROWS * PACKED_SUBLANES, LANES)
    stage_block = (MIX_ROWS * PACKED_SUBLANES, LANES)
    h1, dest, pair_w, blk_table, buf = pl.pallas_call(
        _mixer_kernel,
        grid=(batch, t_blocks),
        in_specs=[tok_spec, _full((1, D_MODEL)), _full((1, D_MODEL)), _full((D_MODEL, IN_COLS)), _full((D_MODEL, D_MODEL)),
                  _full((POOL_GROUPS, POOL_CH, POOL_CH)), _full((1, POOL_WIDTH)), _full((1, POOL_WIDTH)),
                  _full((1, D_MODEL)), _full((1, D_MODEL)), rope_spec, rope_spec,
                  _full((RET_HEADS * CHUNK, CHUNK)), _full((CHUNK, RET_QK_WIDTH)), _full((CHUNK, RET_QK_WIDTH)),
                  _full((RET_QK_WIDTH, RET_WIDTH)), _full((RET_QK_WIDTH, RET_WIDTH)),
                  _full((RET_QK_WIDTH, RET_WIDTH)), _full((N_META, POOL_WIDTH)),
                  _full((ROUTER_ROWS, D_MODEL)), _full((ROUTER_ROWS, 1))],
        out_specs=[tok_spec, pair_spec, pair_spec,
                   _full((8, BLOCK_TABLE_LANES)),
                   pl.BlockSpec(memory_space=pl.ANY)],
        out_shape=[jax.ShapeDtypeStruct((batch, seq, D_MODEL), F32),
                   jax.ShapeDtypeStruct((TOP_K, n_tok), jnp.int32),
                   jax.ShapeDtypeStruct((TOP_K, n_tok), F32),
                   jax.ShapeDtypeStruct((8, BLOCK_TABLE_LANES), jnp.int32),
                   jax.ShapeDtypeStruct(buf_shape, jnp.uint32)],
        scratch_shapes=[pltpu.VMEM((RET_QK_WIDTH, RET_WIDTH), F32),
                        pltpu.VMEM((N_META, POOL_WIDTH), F32),
                        pltpu.VMEM((MIX_ROWS + N_META, POOL_WIDTH), F32),
                        pltpu.VMEM((MIX_ROWS, IN_COLS), F32),
                        pltpu.VMEM((MIX_ROWS, D_MODEL), F32),
                        pltpu.VMEM((MIX_ROWS, D_MODEL), BF16),
                        pltpu.VMEM((MIX_ROWS, D_MODEL), BF16),
                        pltpu.VMEM((MIX_ROWS, MIX_ROWS), BF16),
                        pltpu.VMEM((N_EXPERTS, LANES), F32),
                        pltpu.VMEM((N_EXPERTS, LANES), F32),
                        pltpu.VMEM((8, LANES), F32),
                        pltpu.VMEM((8, BLOCK_TABLE_LANES), F32),
                        pltpu.VMEM((2,) + stage_block, jnp.uint32),
                        pltpu.VMEM(packed_block, jnp.uint32),
                        pltpu.VMEM((TOP_K, MIX_ROWS), jnp.int32),
                        pltpu.SMEM((TOP_K, MIX_ROWS), jnp.int32),
                        pltpu.VMEM((N_EXPERTS, LANES), jnp.int32),
                        pltpu.VMEM((N_EXPERTS, LANES), jnp.int32),
                        pltpu.SMEM((2, N_EXPERTS, LANES), jnp.int32),
                        pltpu.SemaphoreType.DMA((2,)),
                        pltpu.SemaphoreType.DMA, pltpu.SemaphoreType.DMA],
        compiler_params=pltpu.CompilerParams(dimension_semantics=("arbitrary", "arbitrary"),
                                             vmem_limit_bytes=VMEM_LIMIT),
        name="mixer",
    )(x, row(ln_emb_g), row(ln_emb_b), win_b, wout_b, poolw_b, row(pool_b[0]), row(pool_scale[0]),
      row(ln1_g[0]), row(ln1_b[0]), tb["cos"], tb["sin"], tb["dmask"], tb["xi"], tb["zeta"], tb["decay"], tb["bd"],
      s0, tail0, wr, br)

    bidx = jnp.arange(n_blocks, dtype=jnp.int32)
    owner = blk_table[0, :n_blocks]
    key = owner * BLOCK_TABLE_LANES + bidx
    place = jnp.sum(key[None, :] < key[:, None], axis=1)
    order = jnp.sum(jnp.where(place[None, :] == bidx[:, None], bidx[None, :], 0), axis=1).astype(jnp.int32)
    owner_in_order = jnp.sum(jnp.where(place[None, :] == bidx[:, None], owner[None, :], 0), axis=1)
    n_used = jnp.sum(owner < N_EXPERTS)
    used = bidx < n_used
    last_used = jnp.maximum(n_used - 1, 0)
    blk_in = jnp.where(used, order, order[last_used]).astype(jnp.int32)
    blk_e = jnp.where(used, owner_in_order, owner_in_order[last_used]).astype(jnp.int32)
    blk_nv = used.astype(jnp.int32)

    y_sorted = pl.pallas_call(
        _expert_kernel,
        grid_spec=pltpu.PrefetchScalarGridSpec(
            num_scalar_prefetch=4,
            grid=(n_blocks,),
            in_specs=[pl.BlockSpec(packed_block, lambda i, be, nv, bi, bo: (bi[i], 0)),
                      pl.BlockSpec((None, D_MODEL, D_EXPERT), lambda i, be, nv, bi, bo: (be[i], 0, 0)),
                      pl.BlockSpec((None, D_MODEL, D_EXPERT), lambda i, be, nv, bi, bo: (be[i], 0, 0)),
                      pl.BlockSpec((None, D_EXPERT, D_MODEL), lambda i, be, nv, bi, bo: (be[i], 0, 0))],
            out_specs=pl.BlockSpec(tile_block, lambda i, be, nv, bi, bo: (bo[i], 0)),
            scratch_shapes=[pltpu.VMEM((D_MODEL, D_EXPERT), BF16), pltpu.VMEM((D_MODEL, D_EXPERT), BF16),
                            pltpu.VMEM((D_EXPERT, D_MODEL), BF16)],
        ),
        out_shape=jax.ShapeDtypeStruct(sorted_shape, F32),
        compiler_params=pltpu.CompilerParams(dimension_semantics=("arbitrary",), vmem_limit_bytes=VMEM_LIMIT),
        name="experts",
    )(blk_e, blk_nv, blk_in, order, buf, expert_w_gate[0], expert_w_up[0], expert_w_down[0])

    comb_rows = COMBINE_PHASES * COMBINE_PART_ROWS
    comb_steps = n_tok // comb_rows
    assert n_tok % comb_rows == 0
    rows_spec = pl.BlockSpec((comb_rows, D_MODEL), lambda i: (i, 0))
    out = pl.pallas_call(
        _combine_kernel,
        grid=(comb_steps,),
        in_specs=[pl.BlockSpec((TOP_K, comb_rows), lambda i: (0, i), memory_space=pltpu.SMEM),
                  pl.BlockSpec((TOP_K, comb_rows), lambda i: (0, jnp.minimum(i + 1, comb_steps - 1)),
                               memory_space=pltpu.SMEM),
                  rows_spec,
                  pl.BlockSpec((comb_rows, TOP_K), lambda i: (i, 0)),
                  pl.BlockSpec((1, D_MODEL), lambda i: (0, 0)), pl.BlockSpec((1, D_MODEL), lambda i: (0, 0)),
                  pl.BlockSpec(memory_space=pl.ANY)],
        out_specs=rows_spec,
        out_shape=jax.ShapeDtypeStruct((n_tok, D_MODEL), F32),
        scratch_shapes=[pltpu.VMEM((TOP_K, COMBINE_PART_ROWS * TOKEN_SUBLANES, LANES), F32)] * COMBINE_PHASES
                       + [pltpu.SemaphoreType.DMA((COMBINE_PHASES,))],
        compiler_params=pltpu.CompilerParams(dimension_semantics=("arbitrary",), vmem_limit_bytes=VMEM_LIMIT),
        name="combine",
    )(dest, dest, h1.reshape(n_tok, D_MODEL), pair_w.T, row(ln2_g[0]), row(ln2_b[0]), y_sorted)
    return out.reshape(batch, seq, D_MODEL)
```

```python
import jax
import jax.numpy as jnp
from jax import lax
from jax.experimental import pallas as pl
from jax.experimental.pallas import tpu as pltpu

D_MODEL = 1024
DEPTH = 1
N_META = 16
RET_HEADS = 4
RET_WIDTH = D_MODEL // 2
RET_DV = RET_WIDTH // RET_HEADS
RET_DK = RET_DV // 2
RET_QK_WIDTH = RET_HEADS * RET_DK
CHUNK = 128
ROPE_BASE = 10000.0
POOL_WINDOWS = (2, 4, 8, 16)
POOL_GROUPS = len(POOL_WINDOWS)
POOL_WIDTH = D_MODEL // 2
POOL_CH = POOL_WIDTH // POOL_GROUPS
IN_COLS = 2 * RET_QK_WIDTH + 2 * RET_WIDTH + POOL_WIDTH
N_GROUPS = 4
EXPERTS_PER_GROUP = 8
N_EXPERTS = N_GROUPS * EXPERTS_PER_GROUP
D_EXPERT = D_MODEL // 2
TOP_K = 2
LN_EPS = 1e-5
GN_EPS = 1e-6
ALPHA = (2 * DEPTH) ** 0.25

Q0, K0, V0, G0, U0 = 0, RET_QK_WIDTH, 2 * RET_QK_WIDTH, 2 * RET_QK_WIDTH + RET_WIDTH, 2 * RET_QK_WIDTH + 2 * RET_WIDTH

MIX_ROWS = 512
EXPERT_ROWS = 512
ROUTER_ROWS = 40
BLOCK_TABLE_LANES = 384
COMBINE_PHASES = 4
COMBINE_PART_ROWS = 128
INDEX_GROUP = 8
VMEM_LIMIT = 56 * 1024 * 1024
LANES = 128
TOKEN_SUBLANES = D_MODEL // LANES
PACKED_SUBLANES = TOKEN_SUBLANES // 2

F32 = jnp.float32
BF16 = jnp.bfloat16


def _layer_norm(x, g, b, eps):
    mu = jnp.mean(x, axis=-1, keepdims=True)
    xc = x - mu
    var = jnp.mean(xc * xc, axis=-1, keepdims=True)
    return xc * lax.rsqrt(var + eps) * g + b


def _rotary(z, cos, sin_signed, first_half):
    partner = jnp.where(first_half, pltpu.roll(z, RET_QK_WIDTH - RET_DK // 2, 1), pltpu.roll(z, RET_DK // 2, 1))
    return z * cos + partner * sin_signed


def _store_token_tiles(ref, tok0, x):
    n = x.shape[0]
    for s in range(TOKEN_SUBLANES):
        ref[pl.ds(tok0 * TOKEN_SUBLANES + s, n, stride=TOKEN_SUBLANES), :] = x[:, s * LANES:(s + 1) * LANES]


def _load_token_tiles(ref, tok0, n):
    return jnp.concatenate(
        [ref[pl.ds(tok0 * TOKEN_SUBLANES + s, n, stride=TOKEN_SUBLANES), :] for s in range(TOKEN_SUBLANES)], axis=1)


def _token_rows(ref, tok, n=1):
    return ref.at[pl.ds(pl.multiple_of(tok * TOKEN_SUBLANES, TOKEN_SUBLANES), n * TOKEN_SUBLANES), :]


def _store_packed_tokens(ref, tok0, x):
    n, half = x.shape[0], D_MODEL // 2
    lo = lax.bitcast_convert_type(x[:, :half].astype(BF16).astype(F32), jnp.uint32) >> 16
    hi = lax.bitcast_convert_type(x[:, half:].astype(BF16).astype(F32), jnp.uint32) & jnp.uint32(0xFFFF0000)
    words = lo | hi
    for s in range(PACKED_SUBLANES):
        ref[pl.ds(tok0 * PACKED_SUBLANES + s, n, stride=PACKED_SUBLANES), :] = words[:, s * LANES:(s + 1) * LANES]


def _load_packed_tokens(ref, tok0, n):
    words = [ref[pl.ds(tok0 * PACKED_SUBLANES + s, n, stride=PACKED_SUBLANES), :] for s in range(PACKED_SUBLANES)]
    lo = [lax.bitcast_convert_type(w << 16, F32).astype(BF16) for w in words]
    hi = [lax.bitcast_convert_type(w & jnp.uint32(0xFFFF0000), F32).astype(BF16) for w in words]
    return jnp.concatenate(lo + hi, axis=1)


def _packed_rows(ref, tok, n=1):
    return ref.at[pl.ds(pl.multiple_of(tok * PACKED_SUBLANES, PACKED_SUBLANES), n * PACKED_SUBLANES), :]


def _first_half_mask(rows):
    lane = lax.broadcasted_iota(jnp.int32, (rows, RET_QK_WIDTH), 1)
    return (lane % RET_DK) < (RET_DK // 2)


def _meta_kernel(meta_ref, g_ref, b_ref, win_ref, cos_ref, sin_ref, zeta_ref, bd_ref, s0_ref, tail_ref):
    h = _layer_norm(meta_ref[...], g_ref[...], b_ref[...], LN_EPS)
    proj = jnp.dot(h.astype(BF16), win_ref[...], preferred_element_type=F32)
    k = _rotary(proj[:, K0:V0], cos_ref[...], sin_ref[...], _first_half_mask(N_META)) * (RET_DK ** -0.5)
    kz = (k * zeta_ref[...]).astype(BF16)
    v = proj[:, V0:G0].astype(BF16)
    kv = lax.dot_general(kz, v, (((0,), (0,)), ((), ())), preferred_element_type=F32)
    s0_ref[...] = kv * bd_ref[...]
    tail_ref[...] = proj[:, U0:]


def _mixer_kernel(x_ref, lng_ref, lnb_ref, win_ref, wout_ref, poolw_ref, poolb_ref, pools_ref, ln1g_ref, ln1b_ref,
                  cos_ref, sin_ref, dmask_ref, xi_ref, zeta_ref, decay_ref, bd_ref, s0_ref, tail0_ref,
                  wr_ref, br_ref,
                  h1_ref, dest_ref, pw_ref, table_ref, buf_ref,
                  state_ref, tail_ref, uext_ref, proj_ref, h0_ref, h0b_ref, mixin_ref, tri_ref, carry_ref,
                  curblk_ref, nalloc_ref, tab_ref, hkeep_ref, zero_ref, dvm_ref, dsm_ref, cnt_ref, cur_ref, csm_ref,
                  scat_sem, idx_sem, zsem):
    rows = x_ref.shape[0]
    n_chunks = rows // CHUNK
    step = pl.program_id(0) * pl.num_programs(1) + pl.program_id(1)
    n_steps = pl.num_programs(0) * pl.num_programs(1)
    first_step = step == 0
    slot = lax.rem(step, 2)
    prev = 1 - slot
    n_blocks = buf_ref.shape[0] // (EXPERT_ROWS * PACKED_SUBLANES) - 2

    def scatter_wait(b):
        for _ in range(TOP_K):
            pltpu.make_async_copy(hkeep_ref.at[b], _packed_rows(buf_ref, 0, rows), scat_sem.at[b]).wait()

    @pl.when(first_step)
    def _():
        r = lax.broadcasted_iota(jnp.int32, (rows, rows), 0)
        c = lax.broadcasted_iota(jnp.int32, (rows, rows), 1)
        tri_ref[...] = jnp.where(r < c, 1.0, 0.0).astype(BF16)
        carry_ref[...] = jnp.zeros_like(carry_ref)
        curblk_ref[...] = jnp.full(curblk_ref.shape, -1.0, F32)
        nalloc_ref[...] = jnp.zeros_like(nalloc_ref)
        tab_ref[...] = jnp.full(tab_ref.shape, float(N_EXPERTS), F32)
        zero_ref[...] = jnp.zeros_like(zero_ref)
        hkeep_ref[1] = jnp.zeros(hkeep_ref.shape[1:], hkeep_ref.dtype)

        def spare(t, _):
            for s in range(TOP_K):
                dsm_ref[s, t] = (n_blocks + s) * EXPERT_ROWS + t
            return 0
        lax.fori_loop(0, rows, spare, 0)

    @pl.when(pl.program_id(1) == 0)
    def _():
        state_ref[...] = s0_ref[...]
        tail_ref[...] = tail0_ref[...]

    @pl.when(step > 0)
    def _():
        pltpu.make_async_copy(dvm_ref, dsm_ref, idx_sem).wait()

    def scatter_previous(t_lo, t_hi):
        for t0 in range(t_lo, t_hi, INDEX_GROUP):
            dst = [[dsm_ref[s, t0 + k] for s in range(TOP_K)] for k in range(INDEX_GROUP)]
            for k in range(INDEX_GROUP):
                for s in range(TOP_K):
                    pltpu.make_async_copy(_packed_rows(hkeep_ref.at[prev], t0 + k), _packed_rows(buf_ref, dst[k][s]),
                                          scat_sem.at[prev]).start(priority=s % 2)

    def ln_body(c, _):
        sl = pl.ds(pl.multiple_of(c * CHUNK, CHUNK), CHUNK)
        h0 = _layer_norm(x_ref[sl, :], lng_ref[...], lnb_ref[...], LN_EPS)
        h0_ref[sl, :] = h0
        h0b_ref[sl, :] = h0.astype(BF16)
        return 0
    lax.fori_loop(0, n_chunks, ln_body, 0, unroll=True)

    proj_ref[...] = jnp.dot(h0b_ref[...], win_ref[...], preferred_element_type=F32)
    scatter_previous(0, rows)

    first_half = _first_half_mask(CHUNK)
    head_of_lane = lax.broadcasted_iota(jnp.int32, (CHUNK, RET_QK_WIDTH), 1) // RET_DK

    def ret_body(c, _):
        sl = pl.ds(pl.multiple_of(c * CHUNK, CHUNK), CHUNK)
        cos = cos_ref[sl, :]
        sin = sin_ref[sl, :]
        q = _rotary(proj_ref[sl, Q0:K0], cos, sin, first_half)
        k = _rotary(proj_ref[sl, K0:V0], cos, sin, first_half) * (RET_DK ** -0.5)
        qb = q.astype(BF16)
        kb = k.astype(BF16)
        vb = proj_ref[sl, V0:G0].astype(BF16)
        q_heads = jnp.concatenate(
            [jnp.where(head_of_lane == h, qb, jnp.zeros_like(qb)) for h in range(RET_HEADS)], axis=0)
        scores = lax.dot_general(q_heads, kb, (((1,), (1,)), ((), ())), preferred_element_type=F32)
        p = (scores * dmask_ref[...]).astype(BF16)
        inner = jnp.concatenate(
            [jnp.dot(p[h * CHUNK:(h + 1) * CHUNK, :], vb[:, h * RET_DV:(h + 1) * RET_DV],
                     preferred_element_type=F32) for h in range(RET_HEADS)], axis=1)
        state = state_ref[...]
        cross = jnp.dot((q * xi_ref[...]).astype(BF16), state.astype(BF16), preferred_element_type=F32)
        kz = (k * zeta_ref[...]).astype(BF16)
        kv = lax.dot_general(kz, vb, (((0,), (0,)), ((), ())), preferred_element_type=F32)
        state_ref[...] = state * decay_ref[...] + kv * bd_ref[...]
        ret = inner + cross
        gate = proj_ref[sl, G0:U0]
        outs = []
        for h in range(RET_HEADS):
            o = ret[:, h * RET_DV:(h + 1) * RET_DV]
            mu = jnp.mean(o, axis=-1, keepdims=True)
            oc = o - mu
            var = jnp.mean(oc * oc, axis=-1, keepdims=True)
            outs.append(oc * lax.rsqrt(var + GN_EPS))
        gn = jnp.concatenate(outs, axis=1)
        mixin_ref[sl, 0:RET_WIDTH] = (gate * jax.nn.sigmoid(gate) * gn).astype(BF16)
        return 0
    lax.fori_loop(0, n_chunks, ret_body, 0, unroll=True)

    uext_ref[0:N_META, :] = tail_ref[...]
    uext_ref[N_META:, :] = proj_ref[:, U0:]
    tail_ref[...] = uext_ref[rows:, :]
    for g, w in enumerate(POOL_WINDOWS):
        lanes = slice(g * POOL_CH, (g + 1) * POOL_CH)
        e = uext_ref[:, lanes]
        acc = e
        shift = 1
        while shift < w:
            acc = acc + pltpu.roll(acc, shift, 0)
            shift *= 2
        pooled = acc[N_META:, :] * (1.0 / w) - e[N_META:, :]
        mixed = jnp.dot(pooled.astype(BF16), poolw_ref[g], preferred_element_type=F32) + poolb_ref[:, lanes]
        mixin_ref[:, RET_WIDTH + g * POOL_CH:RET_WIDTH + (g + 1) * POOL_CH] = (mixed * pools_ref[:, lanes]).astype(BF16)

    proj_ref[:, 0:D_MODEL] = jnp.dot(mixin_ref[...], wout_ref[...], preferred_element_type=F32)

    @pl.when(step >= 1)
    def _():
        scatter_wait(slot)

    keep = hkeep_ref.at[slot]

    def ln1_body(c, _):
        sl = pl.ds(pl.multiple_of(c * CHUNK, CHUNK), CHUNK)
        h1 = _layer_norm(ALPHA * h0_ref[sl, :] + proj_ref[sl, 0:D_MODEL], ln1g_ref[...], ln1b_ref[...], LN_EPS)
        h1_ref[sl, :] = h1
        _store_packed_tokens(keep, c * CHUNK, h1)
        h0b_ref[sl, :] = h1.astype(BF16)
        return 0
    lax.fori_loop(0, n_chunks, ln1_body, 0, unroll=True)

    logits = lax.dot_general(wr_ref[...], h0b_ref[...], (((1,), (1,)), ((), ())), preferred_element_type=F32)
    logits = logits + br_ref[...]
    gl = logits[0:N_GROUPS, :]
    gmax = jnp.max(gl, axis=0, keepdims=True)
    g_p = 1.0 / jnp.sum(jnp.exp(gl - gmax), axis=0, keepdims=True)
    grow = lax.broadcasted_iota(jnp.int32, gl.shape, 0)
    g_idx = jnp.min(jnp.where(gl == gmax, grow, N_GROUPS), axis=0, keepdims=True)
    sel = logits[8:8 + EXPERTS_PER_GROUP, :]
    for g in range(1, N_GROUPS):
        sel = jnp.where(g_idx == g, logits[8 + g * EXPERTS_PER_GROUP:8 + (g + 1) * EXPERTS_PER_GROUP, :], sel)
    erow = lax.broadcasted_iota(jnp.int32, sel.shape, 0)
    m1 = jnp.max(sel, axis=0, keepdims=True)
    i1 = jnp.min(jnp.where(sel == m1, erow, EXPERTS_PER_GROUP), axis=0, keepdims=True)
    sel2 = jnp.where(erow == i1, -jnp.inf, sel)
    m2 = jnp.max(sel2, axis=0, keepdims=True)
    i2 = jnp.min(jnp.where(sel2 == m2, erow, EXPERTS_PER_GROUP), axis=0, keepdims=True)
    e2 = jnp.exp(m2 - m1)
    w1 = 1.0 / (1.0 + e2)
    w2 = e2 / (1.0 + e2)
    pe0 = g_idx * EXPERTS_PER_GROUP + i1
    pe1 = g_idx * EXPERTS_PER_GROUP + i2
    pw_ref[...] = jnp.concatenate([g_p * w1, g_p * w2], axis=0)
    xrow = lax.broadcasted_iota(jnp.int32, (N_EXPERTS, rows), 0)
    oh0 = xrow == pe0
    oh1 = xrow == pe1
    oh = jnp.where(jnp.logical_or(oh0, oh1), 1.0, 0.0)
    carry = carry_ref[...]
    count = carry[:, 0:1]
    prefix = jnp.dot(oh.astype(BF16), tri_ref[...], preferred_element_type=F32) + count

    inv_rows = 1.0 / EXPERT_ROWS
    added = jnp.sum(oh, axis=1, keepdims=True)
    blocks_old = jnp.floor((count + (EXPERT_ROWS - 1)) * inv_rows)
    blocks_new = jnp.floor((count + added + (EXPERT_ROWS - 1)) * inv_rows)
    need = blocks_new - blocks_old
    er = lax.broadcasted_iota(jnp.int32, (N_EXPERTS, N_EXPERTS), 0)
    ec = lax.broadcasted_iota(jnp.int32, (N_EXPERTS, N_EXPERTS), 1)
    earlier = jnp.where(ec < er, 1.0, 0.0)
    need_lanes = jnp.broadcast_to(need, (N_EXPERTS, LANES))
    before = jnp.dot(earlier.astype(BF16), need_lanes.astype(BF16), preferred_element_type=F32)[:, 0:1]
    nalloc = nalloc_ref[0:1, 0:1]
    new_blk = nalloc + before
    cur_blk = curblk_ref[:, 0:1]
    ordinal = jnp.floor(prefix * inv_rows)
    blk = jnp.where(ordinal < blocks_old, cur_blk, new_blk)
    row_in_buf = blk * EXPERT_ROWS + (prefix - ordinal * EXPERT_ROWS)
    dest0 = jnp.sum(jnp.where(oh0, row_in_buf, 0.0), axis=0, keepdims=True)
    dest1 = jnp.sum(jnp.where(oh1, row_in_buf, 0.0), axis=0, keepdims=True)
    dest = jnp.concatenate([dest0, dest1], axis=0).astype(jnp.int32)
    dest_ref[...] = dest
    dvm_ref[...] = dest
    pltpu.make_async_copy(dvm_ref, dsm_ref, idx_sem).start()

    got_new = need > 0.0
    blk_lane = lax.broadcasted_iota(jnp.int32, (N_EXPERTS, BLOCK_TABLE_LANES), 1).astype(F32)
    erow_f = lax.broadcasted_iota(jnp.int32, (N_EXPERTS, BLOCK_TABLE_LANES), 0).astype(F32)
    owner = jnp.max(jnp.where(jnp.logical_and(got_new, blk_lane == new_blk), erow_f, -1.0), axis=0, keepdims=True)
    tab = jnp.where(owner >= 0.0, owner, tab_ref[0:1, :])
    tab_ref[...] = jnp.broadcast_to(tab, tab_ref.shape)
    cur_blk = jnp.where(got_new, new_blk, cur_blk)
    curblk_ref[...] = jnp.broadcast_to(cur_blk, curblk_ref.shape)
    nalloc_ref[...] = jnp.broadcast_to(nalloc + jnp.sum(need, axis=0, keepdims=True), nalloc_ref.shape)
    carry = carry + added
    carry_ref[...] = carry
    cnt_ref[...] = carry.astype(jnp.int32)
    cur_ref[...] = jnp.broadcast_to(cur_blk, cur_ref.shape).astype(jnp.int32)
    table_ref[...] = jnp.broadcast_to(tab, table_ref.shape).astype(jnp.int32)

    @pl.when(step == n_steps - 1)
    def _():
        pltpu.make_async_copy(dvm_ref, dsm_ref, idx_sem).wait()

        def last_rows(t, _):
            for s in range(TOP_K):
                pltpu.make_async_copy(_packed_rows(keep, t), _packed_rows(buf_ref, dsm_ref[s, t]), scat_sem.at[slot]).start()
            return 0
        lax.fori_loop(0, rows, last_rows, 0)

        csm_cnt = pltpu.make_async_copy(cnt_ref, csm_ref.at[0], zsem)
        csm_cur = pltpu.make_async_copy(cur_ref, csm_ref.at[1], zsem)
        csm_cnt.start()
        csm_cur.start()
        csm_cnt.wait()
        csm_cur.wait()

        def go(copy, do_start):
            if do_start:
                copy.start()
            else:
                copy.wait()

        def tail_fill(do_start):
            def body(e, carry):
                cnt_e = csm_ref[0, e, 0]
                gap = (EXPERT_ROWS - lax.rem(cnt_e, EXPERT_ROWS)) % EXPERT_ROWS
                first = csm_ref[1, e, 0] * EXPERT_ROWS + (EXPERT_ROWS - gap)
                for bit in range(EXPERT_ROWS.bit_length() - 1):
                    run = 1 << bit

                    @pl.when((gap >> bit) & 1 == 1)
                    def _():
                        go(pltpu.make_async_copy(_packed_rows(zero_ref, 0, run),
                                                 _packed_rows(buf_ref, first + (gap & (run - 1)), run), zsem), do_start)
                return carry
            return body

        def unused_fill(do_start):
            def body(b, carry):
                go(pltpu.make_async_copy(zero_ref, _packed_rows(buf_ref, b * EXPERT_ROWS, EXPERT_ROWS), zsem), do_start)
                return carry
            return body

        handed_out = jnp.int32(0)
        for e in range(N_EXPERTS):
            handed_out = handed_out + (csm_ref[0, e, 0] + (EXPERT_ROWS - 1)) // EXPERT_ROWS
        for do_start in (True, False):
            lax.fori_loop(0, N_EXPERTS, tail_fill(do_start), 0)
            lax.fori_loop(handed_out, n_blocks, unused_fill(do_start), 0)

        scatter_wait(prev)
        scatter_wait(slot)


def _expert_kernel(be_ref, nv_ref, bi_ref, bo_ref, nxt_ref, x_ref, wg_ref, wu_ref, wd_ref, y_ref,
                   wgb_ref, wub_ref, wdb_ref, wgf_ref, wuf_ref, wdf_ref, wsem):
    i = pl.program_id(0)
    nv = nv_ref[i]
    expert = be_ref[i]
    new_expert = jnp.logical_or(i == 0, expert != be_ref[jnp.maximum(i - 1, 0)])

    def weight_copies(e):
        return (pltpu.make_async_copy(wg_ref.at[e], wgf_ref, wsem.at[0]),
                pltpu.make_async_copy(wu_ref.at[e], wuf_ref, wsem.at[1]),
                pltpu.make_async_copy(wd_ref.at[e], wdf_ref, wsem.at[2]))

    @pl.when(new_expert)
    def _():
        @pl.when(i == 0)
        def _():
            for c in weight_copies(expert):
                c.start()
        for c in weight_copies(expert):
            c.wait()
        wgb_ref[...] = wgf_ref[...].astype(BF16)
        wub_ref[...] = wuf_ref[...].astype(BF16)
        wdb_ref[...] = wdf_ref[...].astype(BF16)
        following = nxt_ref[i]

        @pl.when(following >= 0)
        def _():
            for c in weight_copies(following):
                c.start()

    @pl.when(nv == 0)
    def _():
        y_ref[...] = jnp.zeros_like(y_ref)

    @pl.when(nv > 0)
    def _():
        x = _load_packed_tokens(x_ref, 0, EXPERT_ROWS)
        gate = jnp.dot(x, wgb_ref[...], preferred_element_type=F32)
        up = jnp.dot(x, wub_ref[...], preferred_element_type=F32)
        act = (gate * jax.nn.sigmoid(gate) * up).astype(BF16)
        _store_token_tiles(y_ref, 0, jnp.dot(act, wdb_ref[...], preferred_element_type=F32))


def _combine_kernel(dcur_ref, dnext_ref, h_ref, pw_ref, g_ref, b_ref, y_ref, o_ref, yb0, yb1, yb2, yb3, sems):
    part = COMBINE_PART_ROWS
    bufs = (yb0, yb1, yb2, yb3)
    step = pl.program_id(0)
    last = pl.num_programs(0) - 1

    def copy(src_tok, ybuf, t, slot, sem):
        return pltpu.make_async_copy(_token_rows(y_ref, src_tok), _token_rows(ybuf.at[slot], t), sem)

    def issue(dref, col0, ybuf, sem):
        for t0 in range(0, part, INDEX_GROUP):
            src = [[dref[slot, col0 + t0 + k] for slot in range(TOP_K)] for k in range(INDEX_GROUP)]
            for k in range(INDEX_GROUP):
                for slot in range(TOP_K):
                    copy(src[k][slot], ybuf, t0 + k, slot, sem).start(priority=slot % 2)

    def wait(ybuf, sem):
        for slot in range(TOP_K):
            pltpu.make_async_copy(_token_rows(y_ref, 0, part), ybuf.at[slot], sem).wait()

    def finish(ybuf, r0):
        rows = pl.ds(r0, part)
        pw = pw_ref[rows, :]
        y = pw[:, 0:1] * _load_token_tiles(ybuf.at[0], 0, part) + pw[:, 1:2] * _load_token_tiles(ybuf.at[1], 0, part)
        o_ref[rows, :] = _layer_norm(ALPHA * h_ref[rows, :] + y, g_ref[...], b_ref[...], LN_EPS)

    @pl.when(step == 0)
    def _():
        for p in range(2):
            def first(t, _, p=p):
                for slot in range(TOP_K):
                    copy(dcur_ref[slot, p * part + t], bufs[p], t, slot, sems.at[p]).start()
                return 0
            lax.fori_loop(0, part, first, 0)

    for p in range(COMBINE_PHASES):
        ahead = (p + 2) % COMBINE_PHASES
        wait(bufs[p], sems.at[p])
        if p + 2 < COMBINE_PHASES:
            issue(dcur_ref, (p + 2) * part, bufs[ahead], sems.at[ahead])
        else:
            issue(dnext_ref, ahead * part, bufs[ahead], sems.at[ahead])
        finish(bufs[p], p * part)

    @pl.when(step == last)
    def _():
        wait(bufs[0], sems.at[0])
        wait(bufs[1], sems.at[1])


def _tables(seq):
    log_g = jnp.log1p(-jnp.power(2.0, -5.0 - jnp.arange(RET_HEADS, dtype=F32)))
    i = jnp.arange(CHUNK, dtype=F32)
    rel = i[:, None] - i[None, :]
    dmask = jnp.where(rel[None] >= 0, jnp.exp(jnp.maximum(rel, 0.0)[None] * log_g[:, None, None]), 0.0)
    dmask = dmask.reshape(RET_HEADS * CHUNK, CHUNK)
    lg_lane = jnp.repeat(log_g, RET_DK)
    xi = jnp.exp((i + 1)[:, None] * lg_lane[None, :])
    zeta = jnp.exp((CHUNK - 1 - i)[:, None] * lg_lane[None, :])
    decay = jnp.broadcast_to(jnp.exp(CHUNK * lg_lane)[:, None], (RET_QK_WIDTH, RET_WIDTH))
    bd = (jnp.arange(RET_QK_WIDTH)[:, None] // RET_DK == jnp.arange(RET_WIDTH)[None, :] // RET_DV).astype(F32)
    zeta_meta = zeta[CHUNK - N_META:, :]
    half = RET_DK // 2
    inv = ROPE_BASE ** (-jnp.arange(half, dtype=F32) / half)
    pos = jnp.arange(N_META + seq, dtype=F32)
    ang = pos[:, None] * inv[None, :]
    cos = jnp.tile(jnp.cos(ang), (1, 2 * RET_HEADS))
    sin = jnp.tile(jnp.concatenate([-jnp.sin(ang), jnp.sin(ang)], axis=1), (1, RET_HEADS))
    return dict(dmask=dmask, xi=xi, zeta=zeta, decay=decay, bd=bd, zeta_meta=zeta_meta,
                cos_meta=cos[:N_META], sin_meta=sin[:N_META], cos=cos[N_META:], sin=sin[N_META:])


def _full(shape):
    return pl.BlockSpec(shape, lambda *_: (0,) * len(shape))


def kernel(x, meta_tokens, ln_emb_g, ln_emb_b, w_in, pool_w, pool_b, pool_scale, w_out, ln1_g, ln1_b, router_group_w, router_group_b, router_expert_w, router_expert_b, expert_w_gate, expert_w_up, expert_w_down, ln2_g, ln2_b):
    batch, seq, d = x.shape
    assert d == D_MODEL and seq % MIX_ROWS == 0 and MIX_ROWS <= EXPERT_ROWS
    n_tok = batch * seq
    t_blocks = seq // MIX_ROWS
    tb = _tables(seq)

    row = lambda a: a.reshape(1, -1).astype(F32)
    win_b = w_in[0].astype(BF16)
    wout_b = w_out[0].astype(BF16)
    poolw_b = pool_w[0].astype(BF16)
    wr = jnp.zeros((ROUTER_ROWS, D_MODEL), F32)
    wr = wr.at[0:N_GROUPS].set(router_group_w[0].T).at[8:8 + N_EXPERTS].set(router_expert_w[0].T).astype(BF16)
    br = jnp.zeros((ROUTER_ROWS, 1), F32)
    br = br.at[0:N_GROUPS, 0].set(router_group_b[0]).at[8:8 + N_EXPERTS, 0].set(router_expert_b[0])

    s0, tail0 = pl.pallas_call(
        _meta_kernel,
        out_shape=(jax.ShapeDtypeStruct((RET_QK_WIDTH, RET_WIDTH), F32), jax.ShapeDtypeStruct((N_META, POOL_WIDTH), F32)),
        name="meta_prep",
    )(meta_tokens.astype(F32), row(ln_emb_g), row(ln_emb_b), win_b, tb["cos_meta"], tb["sin_meta"], tb["zeta_meta"], tb["bd"])

    tok_spec = pl.BlockSpec((None, MIX_ROWS, D_MODEL), lambda b, j: (b, j, 0))
    pair_spec = pl.BlockSpec((TOP_K, MIX_ROWS), lambda b, j: (0, b * t_blocks + j))
    rope_spec = pl.BlockSpec((MIX_ROWS, RET_QK_WIDTH), lambda b, j: (j, 0))
    n_blocks = (n_tok * TOP_K) // EXPERT_ROWS + N_EXPERTS
    assert n_blocks + 2 <= BLOCK_TABLE_LANES
    tile_block = (EXPERT_ROWS * TOKEN_SUBLANES, LANES)
    packed_block = (EXPERT_ROWS * PACKED_SUBLANES, LANES)
    sorted_shape = (n_blocks * EXPERT_ROWS * TOKEN_SUBLANES, LANES)
    buf_shape = ((n_blocks + 2) * EXPERT_ROWS * PACKED_SUBLANES, LANES)
    stage_block = (MIX_ROWS * PACKED_SUBLANES, LANES)
    h1, dest, pair_w, blk_table, buf = pl.pallas_call(
        _mixer_kernel,
        grid=(batch, t_blocks),
        in_specs=[tok_spec, _full((1, D_MODEL)), _full((1, D_MODEL)), _full((D_MODEL, IN_COLS)), _full((D_MODEL, D_MODEL)),
                  _full((POOL_GROUPS, POOL_CH, POOL_CH)), _full((1, POOL_WIDTH)), _full((1, POOL_WIDTH)),
                  _full((1, D_MODEL)), _full((1, D_MODEL)), rope_spec, rope_spec,
                  _full((RET_HEADS * CHUNK, CHUNK)), _full((CHUNK, RET_QK_WIDTH)), _full((CHUNK, RET_QK_WIDTH)),
                  _full((RET_QK_WIDTH, RET_WIDTH)), _full((RET_QK_WIDTH, RET_WIDTH)),
                  _full((RET_QK_WIDTH, RET_WIDTH)), _full((N_META, POOL_WIDTH)),
                  _full((ROUTER_ROWS, D_MODEL)), _full((ROUTER_ROWS, 1))],
        out_specs=[tok_spec, pair_spec, pair_spec,
                   _full((8, BLOCK_TABLE_LANES)),
                   pl.BlockSpec(memory_space=pl.ANY)],
        out_shape=[jax.ShapeDtypeStruct((batch, seq, D_MODEL), F32),
                   jax.ShapeDtypeStruct((TOP_K, n_tok), jnp.int32),
                   jax.ShapeDtypeStruct((TOP_K, n_tok), F32),
                   jax.ShapeDtypeStruct((8, BLOCK_TABLE_LANES), jnp.int32),
                   jax.ShapeDtypeStruct(buf_shape, jnp.uint32)],
        scratch_shapes=[pltpu.VMEM((RET_QK_WIDTH, RET_WIDTH), F32),
                        pltpu.VMEM((N_META, POOL_WIDTH), F32),
                        pltpu.VMEM((MIX_ROWS + N_META, POOL_WIDTH), F32),
                        pltpu.VMEM((MIX_ROWS, IN_COLS), F32),
                        pltpu.VMEM((MIX_ROWS, D_MODEL), F32),
                        pltpu.VMEM((MIX_ROWS, D_MODEL), BF16),
                        pltpu.VMEM((MIX_ROWS, D_MODEL), BF16),
                        pltpu.VMEM((MIX_ROWS, MIX_ROWS), BF16),
                        pltpu.VMEM((N_EXPERTS, LANES), F32),
                        pltpu.VMEM((N_EXPERTS, LANES), F32),
                        pltpu.VMEM((8, LANES), F32),
                        pltpu.VMEM((8, BLOCK_TABLE_LANES), F32),
                        pltpu.VMEM((2,) + stage_block, jnp.uint32),
                        pltpu.VMEM(packed_block, jnp.uint32),
                        pltpu.VMEM((TOP_K, MIX_ROWS), jnp.int32),
                        pltpu.SMEM((TOP_K, MIX_ROWS), jnp.int32),
                        pltpu.VMEM((N_EXPERTS, LANES), jnp.int32),
                        pltpu.VMEM((N_EXPERTS, LANES), jnp.int32),
                        pltpu.SMEM((2, N_EXPERTS, LANES), jnp.int32),
                        pltpu.SemaphoreType.DMA((2,)),
                        pltpu.SemaphoreType.DMA, pltpu.SemaphoreType.DMA],
        compiler_params=pltpu.CompilerParams(dimension_semantics=("arbitrary", "arbitrary"),
                                             vmem_limit_bytes=VMEM_LIMIT),
        name="mixer",
    )(x, row(ln_emb_g), row(ln_emb_b), win_b, wout_b, poolw_b, row(pool_b[0]), row(pool_scale[0]),
      row(ln1_g[0]), row(ln1_b[0]), tb["cos"], tb["sin"], tb["dmask"], tb["xi"], tb["zeta"], tb["decay"], tb["bd"],
      s0, tail0, wr, br)

    bidx = jnp.arange(n_blocks, dtype=jnp.int32)
    owner = blk_table[0, :n_blocks]
    key = owner * BLOCK_TABLE_LANES + bidx
    place = jnp.sum(key[None, :] < key[:, None], axis=1)
    order = jnp.sum(jnp.where(place[None, :] == bidx[:, None], bidx[None, :], 0), axis=1).astype(jnp.int32)
    owner_in_order = jnp.sum(jnp.where(place[None, :] == bidx[:, None], owner[None, :], 0), axis=1)
    n_used = jnp.sum(owner < N_EXPERTS)
    used = bidx < n_used
    last_used = jnp.maximum(n_used - 1, 0)
    blk_in = jnp.where(used, order, order[last_used]).astype(jnp.int32)
    blk_e = jnp.where(used, owner_in_order, owner_in_order[last_used]).astype(jnp.int32)
    blk_nv = used.astype(jnp.int32)
    later = jnp.logical_and(owner_in_order[None, :] > blk_e[:, None], used[None, :])
    blk_next = jnp.min(jnp.where(later, owner_in_order[None, :], N_EXPERTS), axis=1)
    blk_next = jnp.where(blk_next < N_EXPERTS, blk_next, -1).astype(jnp.int32)

    y_sorted = pl.pallas_call(
        _expert_kernel,
        grid_spec=pltpu.PrefetchScalarGridSpec(
            num_scalar_prefetch=5,
            grid=(n_blocks,),
            in_specs=[pl.BlockSpec(packed_block, lambda i, be, nv, bi, bo, nx: (bi[i], 0)),
                      pl.BlockSpec(memory_space=pl.ANY), pl.BlockSpec(memory_space=pl.ANY),
                      pl.BlockSpec(memory_space=pl.ANY)],
            out_specs=pl.BlockSpec(tile_block, lambda i, be, nv, bi, bo, nx: (bo[i], 0)),
            scratch_shapes=[pltpu.VMEM((D_MODEL, D_EXPERT), BF16), pltpu.VMEM((D_MODEL, D_EXPERT), BF16),
                            pltpu.VMEM((D_EXPERT, D_MODEL), BF16),
                            pltpu.VMEM((D_MODEL, D_EXPERT), F32), pltpu.VMEM((D_MODEL, D_EXPERT), F32),
                            pltpu.VMEM((D_EXPERT, D_MODEL), F32), pltpu.SemaphoreType.DMA((3,))],
        ),
        out_shape=jax.ShapeDtypeStruct(sorted_shape, F32),
        compiler_params=pltpu.CompilerParams(dimension_semantics=("arbitrary",), vmem_limit_bytes=VMEM_LIMIT),
        name="experts",
    )(blk_e, blk_nv, blk_in, order, blk_next, buf, expert_w_gate[0], expert_w_up[0], expert_w_down[0])

    comb_rows = COMBINE_PHASES * COMBINE_PART_ROWS
    comb_steps = n_tok // comb_rows
    assert n_tok % comb_rows == 0
    rows_spec = pl.BlockSpec((comb_rows, D_MODEL), lambda i: (i, 0))
    out = pl.pallas_call(
        _combine_kernel,
        grid=(comb_steps,),
        in_specs=[pl.BlockSpec((TOP_K, comb_rows), lambda i: (0, i), memory_space=pltpu.SMEM),
                  pl.BlockSpec((TOP_K, comb_rows), lambda i: (0, jnp.minimum(i + 1, comb_steps - 1)),
                               memory_space=pltpu.SMEM),
                  rows_spec,
                  pl.BlockSpec((comb_rows, TOP_K), lambda i: (i, 0)),
                  pl.BlockSpec((1, D_MODEL), lambda i: (0, 0)), pl.BlockSpec((1, D_MODEL), lambda i: (0, 0)),
                  pl.BlockSpec(memory_space=pl.ANY)],
        out_specs=rows_spec,
        out_shape=jax.ShapeDtypeStruct((n_tok, D_MODEL), F32),
        scratch_shapes=[pltpu.VMEM((TOP_K, COMBINE_PART_ROWS * TOKEN_SUBLANES, LANES), F32)] * COMBINE_PHASES
                       + [pltpu.SemaphoreType.DMA((COMBINE_PHASES,))],
        compiler_params=pltpu.CompilerParams(dimension_semantics=("arbitrary",), vmem_limit_bytes=VMEM_LIMIT),
        name="combine",
    )(dest, dest, h1.reshape(n_tok, D_MODEL), pair_w.T, row(ln2_g[0]), row(ln2_b[0]), y_sorted)
    return out.reshape(batch, seq, D_MODEL)
```

```python
import jax
import jax.numpy as jnp
from jax import lax
from jax.experimental import pallas as pl
from jax.experimental.pallas import tpu as pltpu

D_MODEL = 1024
DEPTH = 1
N_META = 16
RET_HEADS = 4
RET_WIDTH = D_MODEL // 2
RET_DV = RET_WIDTH // RET_HEADS
RET_DK = RET_DV // 2
RET_QK_WIDTH = RET_HEADS * RET_DK
CHUNK = 128
ROPE_BASE = 10000.0
POOL_WINDOWS = (2, 4, 8, 16)
POOL_GROUPS = len(POOL_WINDOWS)
POOL_WIDTH = D_MODEL // 2
POOL_CH = POOL_WIDTH // POOL_GROUPS
IN_COLS = 2 * RET_QK_WIDTH + 2 * RET_WIDTH + POOL_WIDTH
N_GROUPS = 4
EXPERTS_PER_GROUP = 8
N_EXPERTS = N_GROUPS * EXPERTS_PER_GROUP
D_EXPERT = D_MODEL // 2
TOP_K = 2
LN_EPS = 1e-5
GN_EPS = 1e-6
ALPHA = (2 * DEPTH) ** 0.25

Q0, K0, V0, G0, U0 = 0, RET_QK_WIDTH, 2 * RET_QK_WIDTH, 2 * RET_QK_WIDTH + RET_WIDTH, 2 * RET_QK_WIDTH + 2 * RET_WIDTH

MIX_ROWS = 512
EXPERT_ROWS = 512
ROUTER_ROWS = 40
BLOCK_TABLE_LANES = 384
COMBINE_PHASES = 4
COMBINE_PART_ROWS = 128
INDEX_GROUP = 8
VMEM_LIMIT = 56 * 1024 * 1024
LANES = 128
PACKED_SUBLANES = D_MODEL // (2 * LANES)

F32 = jnp.float32
BF16 = jnp.bfloat16


def _layer_norm(x, g, b, eps):
    mu = jnp.mean(x, axis=-1, keepdims=True)
    xc = x - mu
    var = jnp.mean(xc * xc, axis=-1, keepdims=True)
    return xc * lax.rsqrt(var + eps) * g + b


def _rotary(z, cos, sin_signed, first_half):
    partner = jnp.where(first_half, pltpu.roll(z, RET_QK_WIDTH - RET_DK // 2, 1), pltpu.roll(z, RET_DK // 2, 1))
    return z * cos + partner * sin_signed


def _store_packed_tokens(ref, tok0, x):
    n, half = x.shape[0], D_MODEL // 2
    lo = lax.bitcast_convert_type(x[:, :half].astype(BF16).astype(F32), jnp.uint32) >> 16
    hi = lax.bitcast_convert_type(x[:, half:].astype(BF16).astype(F32), jnp.uint32) & jnp.uint32(0xFFFF0000)
    words = lo | hi
    for s in range(PACKED_SUBLANES):
        ref[pl.ds(tok0 * PACKED_SUBLANES + s, n, stride=PACKED_SUBLANES), :] = words[:, s * LANES:(s + 1) * LANES]


def _load_packed_tokens(ref, tok0, n, dtype=BF16):
    words = [ref[pl.ds(tok0 * PACKED_SUBLANES + s, n, stride=PACKED_SUBLANES), :] for s in range(PACKED_SUBLANES)]
    lo = [lax.bitcast_convert_type(w << 16, F32).astype(dtype) for w in words]
    hi = [lax.bitcast_convert_type(w & jnp.uint32(0xFFFF0000), F32).astype(dtype) for w in words]
    return jnp.concatenate(lo + hi, axis=1)


def _packed_rows(ref, tok, n=1):
    return ref.at[pl.ds(pl.multiple_of(tok * PACKED_SUBLANES, PACKED_SUBLANES), n * PACKED_SUBLANES), :]


def _first_half_mask(rows):
    lane = lax.broadcasted_iota(jnp.int32, (rows, RET_QK_WIDTH), 1)
    return (lane % RET_DK) < (RET_DK // 2)


def _meta_kernel(meta_ref, g_ref, b_ref, win_ref, cos_ref, sin_ref, zeta_ref, bd_ref, s0_ref, tail_ref):
    h = _layer_norm(meta_ref[...], g_ref[...], b_ref[...], LN_EPS)
    proj = jnp.dot(h.astype(BF16), win_ref[...], preferred_element_type=F32)
    k = _rotary(proj[:, K0:V0], cos_ref[...], sin_ref[...], _first_half_mask(N_META)) * (RET_DK ** -0.5)
    kz = (k * zeta_ref[...]).astype(BF16)
    v = proj[:, V0:G0].astype(BF16)
    kv = lax.dot_general(kz, v, (((0,), (0,)), ((), ())), preferred_element_type=F32)
    s0_ref[...] = kv * bd_ref[...]
    tail_ref[...] = proj[:, U0:]


def _mixer_kernel(x_ref, lng_ref, lnb_ref, win_ref, wout_ref, poolw_ref, poolb_ref, pools_ref, ln1g_ref, ln1b_ref,
                  cos_ref, sin_ref, dmask_ref, xi_ref, zeta_ref, decay_ref, bd_ref, s0_ref, tail0_ref,
                  wr_ref, br_ref,
                  h1_ref, dest_ref, pw_ref, table_ref, buf_ref,
                  state_ref, tail_ref, uext_ref, proj_ref, h0_ref, h0b_ref, mixin_ref, tri_ref, carry_ref,
                  curblk_ref, nalloc_ref, tab_ref, hkeep_ref, zero_ref, dvm_ref, dsm_ref, cnt_ref, cur_ref, csm_ref,
                  scat_sem, idx_sem, zsem):
    rows = x_ref.shape[0]
    n_chunks = rows // CHUNK
    step = pl.program_id(0) * pl.num_programs(1) + pl.program_id(1)
    n_steps = pl.num_programs(0) * pl.num_programs(1)
    first_step = step == 0
    slot = lax.rem(step, 2)
    prev = 1 - slot
    n_blocks = buf_ref.shape[0] // (EXPERT_ROWS * PACKED_SUBLANES) - 2

    def scatter_wait(b):
        for _ in range(TOP_K):
            pltpu.make_async_copy(hkeep_ref.at[b], _packed_rows(buf_ref, 0, rows), scat_sem.at[b]).wait()

    @pl.when(first_step)
    def _():
        r = lax.broadcasted_iota(jnp.int32, (rows, rows), 0)
        c = lax.broadcasted_iota(jnp.int32, (rows, rows), 1)
        tri_ref[...] = jnp.where(r < c, 1.0, 0.0).astype(BF16)
        carry_ref[...] = jnp.zeros_like(carry_ref)
        curblk_ref[...] = jnp.full(curblk_ref.shape, -1.0, F32)
        nalloc_ref[...] = jnp.zeros_like(nalloc_ref)
        tab_ref[...] = jnp.full(tab_ref.shape, float(N_EXPERTS), F32)
        zero_ref[...] = jnp.zeros_like(zero_ref)
        hkeep_ref[1] = jnp.zeros(hkeep_ref.shape[1:], hkeep_ref.dtype)

        def spare(t, _):
            for s in range(TOP_K):
                dsm_ref[s, t] = (n_blocks + s) * EXPERT_ROWS + t
            return 0
        lax.fori_loop(0, rows, spare, 0)

    @pl.when(pl.program_id(1) == 0)
    def _():
        state_ref[...] = s0_ref[...]
        tail_ref[...] = tail0_ref[...]

    @pl.when(step > 0)
    def _():
        pltpu.make_async_copy(dvm_ref, dsm_ref, idx_sem).wait()

    def scatter_previous(t_lo, t_hi):
        for t0 in range(t_lo, t_hi, INDEX_GROUP):
            dst = [[dsm_ref[s, t0 + k] for s in range(TOP_K)] for k in range(INDEX_GROUP)]
            for k in range(INDEX_GROUP):
                for s in range(TOP_K):
                    pltpu.make_async_copy(_packed_rows(hkeep_ref.at[prev], t0 + k), _packed_rows(buf_ref, dst[k][s]),
                                          scat_sem.at[prev]).start(priority=s % 2)

    def ln_body(c, _):
        sl = pl.ds(pl.multiple_of(c * CHUNK, CHUNK), CHUNK)
        h0 = _layer_norm(x_ref[sl, :], lng_ref[...], lnb_ref[...], LN_EPS)
        h0_ref[sl, :] = h0
        h0b_ref[sl, :] = h0.astype(BF16)
        return 0
    lax.fori_loop(0, n_chunks, ln_body, 0, unroll=True)

    proj_ref[...] = jnp.dot(h0b_ref[...], win_ref[...], preferred_element_type=F32)
    scatter_previous(0, rows)

    first_half = _first_half_mask(CHUNK)
    head_of_lane = lax.broadcasted_iota(jnp.int32, (CHUNK, RET_QK_WIDTH), 1) // RET_DK

    def ret_body(c, _):
        sl = pl.ds(pl.multiple_of(c * CHUNK, CHUNK), CHUNK)
        cos = cos_ref[sl, :]
        sin = sin_ref[sl, :]
        q = _rotary(proj_ref[sl, Q0:K0], cos, sin, first_half)
        k = _rotary(proj_ref[sl, K0:V0], cos, sin, first_half) * (RET_DK ** -0.5)
        qb = q.astype(BF16)
        kb = k.astype(BF16)
        vb = proj_ref[sl, V0:G0].astype(BF16)
        q_heads = jnp.concatenate(
            [jnp.where(head_of_lane == h, qb, jnp.zeros_like(qb)) for h in range(RET_HEADS)], axis=0)
        scores = lax.dot_general(q_heads, kb, (((1,), (1,)), ((), ())), preferred_element_type=F32)
        p = (scores * dmask_ref[...]).astype(BF16)
        inner = jnp.concatenate(
            [jnp.dot(p[h * CHUNK:(h + 1) * CHUNK, :], vb[:, h * RET_DV:(h + 1) * RET_DV],
                     preferred_element_type=F32) for h in range(RET_HEADS)], axis=1)
        state = state_ref[...]
        cross = jnp.dot((q * xi_ref[...]).astype(BF16), state.astype(BF16), preferred_element_type=F32)
        kz = (k * zeta_ref[...]).astype(BF16)
        kv = lax.dot_general(kz, vb, (((0,), (0,)), ((), ())), preferred_element_type=F32)
        state_ref[...] = state * decay_ref[...] + kv * bd_ref[...]
        ret = inner + cross
        gate = proj_ref[sl, G0:U0]
        outs = []
        for h in range(RET_HEADS):
            o = ret[:, h * RET_DV:(h + 1) * RET_DV]
            mu = jnp.mean(o, axis=-1, keepdims=True)
            oc = o - mu
            var = jnp.mean(oc * oc, axis=-1, keepdims=True)
            outs.append(oc * lax.rsqrt(var + GN_EPS))
        gn = jnp.concatenate(outs, axis=1)
        mixin_ref[sl, 0:RET_WIDTH] = (gate * jax.nn.sigmoid(gate) * gn).astype(BF16)
        return 0
    lax.fori_loop(0, n_chunks, ret_body, 0, unroll=True)

    uext_ref[0:N_META, :] = tail_ref[...]
    uext_ref[N_META:, :] = proj_ref[:, U0:]
    tail_ref[...] = uext_ref[rows:, :]
    for g, w in enumerate(POOL_WINDOWS):
        lanes = slice(g * POOL_CH, (g + 1) * POOL_CH)
        e = uext_ref[:, lanes]
        acc = e
        shift = 1
        while shift < w:
            acc = acc + pltpu.roll(acc, shift, 0)
            shift *= 2
        pooled = acc[N_META:, :] * (1.0 / w) - e[N_META:, :]
        mixed = jnp.dot(pooled.astype(BF16), poolw_ref[g], preferred_element_type=F32) + poolb_ref[:, lanes]
        mixin_ref[:, RET_WIDTH + g * POOL_CH:RET_WIDTH + (g + 1) * POOL_CH] = (mixed * pools_ref[:, lanes]).astype(BF16)

    proj_ref[:, 0:D_MODEL] = jnp.dot(mixin_ref[...], wout_ref[...], preferred_element_type=F32)

    @pl.when(step >= 1)
    def _():
        scatter_wait(slot)

    keep = hkeep_ref.at[slot]

    def ln1_body(c, _):
        sl = pl.ds(pl.multiple_of(c * CHUNK, CHUNK), CHUNK)
        h1 = _layer_norm(ALPHA * h0_ref[sl, :] + proj_ref[sl, 0:D_MODEL], ln1g_ref[...], ln1b_ref[...], LN_EPS)
        h1_ref[sl, :] = h1
        _store_packed_tokens(keep, c * CHUNK, h1)
        h0b_ref[sl, :] = h1.astype(BF16)
        return 0
    lax.fori_loop(0, n_chunks, ln1_body, 0, unroll=True)

    logits = lax.dot_general(wr_ref[...], h0b_ref[...], (((1,), (1,)), ((), ())), preferred_element_type=F32)
    logits = logits + br_ref[...]
    gl = logits[0:N_GROUPS, :]
    gmax = jnp.max(gl, axis=0, keepdims=True)
    g_p = 1.0 / jnp.sum(jnp.exp(gl - gmax), axis=0, keepdims=True)
    grow = lax.broadcasted_iota(jnp.int32, gl.shape, 0)
    g_idx = jnp.min(jnp.where(gl == gmax, grow, N_GROUPS), axis=0, keepdims=True)
    sel = logits[8:8 + EXPERTS_PER_GROUP, :]
    for g in range(1, N_GROUPS):
        sel = jnp.where(g_idx == g, logits[8 + g * EXPERTS_PER_GROUP:8 + (g + 1) * EXPERTS_PER_GROUP, :], sel)
    erow = lax.broadcasted_iota(jnp.int32, sel.shape, 0)
    m1 = jnp.max(sel, axis=0, keepdims=True)
    i1 = jnp.min(jnp.where(sel == m1, erow, EXPERTS_PER_GROUP), axis=0, keepdims=True)
    sel2 = jnp.where(erow == i1, -jnp.inf, sel)
    m2 = jnp.max(sel2, axis=0, keepdims=True)
    i2 = jnp.min(jnp.where(sel2 == m2, erow, EXPERTS_PER_GROUP), axis=0, keepdims=True)
    e2 = jnp.exp(m2 - m1)
    w1 = 1.0 / (1.0 + e2)
    w2 = e2 / (1.0 + e2)
    pe0 = g_idx * EXPERTS_PER_GROUP + i1
    pe1 = g_idx * EXPERTS_PER_GROUP + i2
    pw_ref[...] = jnp.concatenate([g_p * w1, g_p * w2], axis=0)
    xrow = lax.broadcasted_iota(jnp.int32, (N_EXPERTS, rows), 0)
    oh0 = xrow == pe0
    oh1 = xrow == pe1
    oh = jnp.where(jnp.logical_or(oh0, oh1), 1.0, 0.0)
    carry = carry_ref[...]
    count = carry[:, 0:1]
    prefix = jnp.dot(oh.astype(BF16), tri_ref[...], preferred_element_type=F32) + count

    inv_rows = 1.0 / EXPERT_ROWS
    added = jnp.sum(oh, axis=1, keepdims=True)
    blocks_old = jnp.floor((count + (EXPERT_ROWS - 1)) * inv_rows)
    blocks_new = jnp.floor((count + added + (EXPERT_ROWS - 1)) * inv_rows)
    need = blocks_new - blocks_old
    er = lax.broadcasted_iota(jnp.int32, (N_EXPERTS, N_EXPERTS), 0)
    ec = lax.broadcasted_iota(jnp.int32, (N_EXPERTS, N_EXPERTS), 1)
    earlier = jnp.where(ec < er, 1.0, 0.0)
    need_lanes = jnp.broadcast_to(need, (N_EXPERTS, LANES))
    before = jnp.dot(earlier.astype(BF16), need_lanes.astype(BF16), preferred_element_type=F32)[:, 0:1]
    nalloc = nalloc_ref[0:1, 0:1]
    new_blk = nalloc + before
    cur_blk = curblk_ref[:, 0:1]
    ordinal = jnp.floor(prefix * inv_rows)
    blk = jnp.where(ordinal < blocks_old, cur_blk, new_blk)
    row_in_buf = blk * EXPERT_ROWS + (prefix - ordinal * EXPERT_ROWS)
    dest0 = jnp.sum(jnp.where(oh0, row_in_buf, 0.0), axis=0, keepdims=True)
    dest1 = jnp.sum(jnp.where(oh1, row_in_buf, 0.0), axis=0, keepdims=True)
    dest = jnp.concatenate([dest0, dest1], axis=0).astype(jnp.int32)
    dest_ref[...] = dest
    dvm_ref[...] = dest
    pltpu.make_async_copy(dvm_ref, dsm_ref, idx_sem).start()

    got_new = need > 0.0
    blk_lane = lax.broadcasted_iota(jnp.int32, (N_EXPERTS, BLOCK_TABLE_LANES), 1).astype(F32)
    erow_f = lax.broadcasted_iota(jnp.int32, (N_EXPERTS, BLOCK_TABLE_LANES), 0).astype(F32)
    owner = jnp.max(jnp.where(jnp.logical_and(got_new, blk_lane == new_blk), erow_f, -1.0), axis=0, keepdims=True)
    tab = jnp.where(owner >= 0.0, owner, tab_ref[0:1, :])
    tab_ref[...] = jnp.broadcast_to(tab, tab_ref.shape)
    cur_blk = jnp.where(got_new, new_blk, cur_blk)
    curblk_ref[...] = jnp.broadcast_to(cur_blk, curblk_ref.shape)
    nalloc_ref[...] = jnp.broadcast_to(nalloc + jnp.sum(need, axis=0, keepdims=True), nalloc_ref.shape)
    carry = carry + added
    carry_ref[...] = carry
    cnt_ref[...] = carry.astype(jnp.int32)
    cur_ref[...] = jnp.broadcast_to(cur_blk, cur_ref.shape).astype(jnp.int32)
    table_ref[...] = jnp.broadcast_to(tab, table_ref.shape).astype(jnp.int32)

    @pl.when(step == n_steps - 1)
    def _():
        pltpu.make_async_copy(dvm_ref, dsm_ref, idx_sem).wait()

        def last_rows(t, _):
            for s in range(TOP_K):
                pltpu.make_async_copy(_packed_rows(keep, t), _packed_rows(buf_ref, dsm_ref[s, t]), scat_sem.at[slot]).start()
            return 0
        lax.fori_loop(0, rows, last_rows, 0)

        csm_cnt = pltpu.make_async_copy(cnt_ref, csm_ref.at[0], zsem)
        csm_cur = pltpu.make_async_copy(cur_ref, csm_ref.at[1], zsem)
        csm_cnt.start()
        csm_cur.start()
        csm_cnt.wait()
        csm_cur.wait()

        def go(copy, do_start):
            if do_start:
                copy.start()
            else:
                copy.wait()

        def tail_fill(do_start):
            def body(e, carry):
                cnt_e = csm_ref[0, e, 0]
                gap = (EXPERT_ROWS - lax.rem(cnt_e, EXPERT_ROWS)) % EXPERT_ROWS
                first = csm_ref[1, e, 0] * EXPERT_ROWS + (EXPERT_ROWS - gap)
                for bit in range(EXPERT_ROWS.bit_length() - 1):
                    run = 1 << bit

                    @pl.when((gap >> bit) & 1 == 1)
                    def _():
                        go(pltpu.make_async_copy(_packed_rows(zero_ref, 0, run),
                                                 _packed_rows(buf_ref, first + (gap & (run - 1)), run), zsem), do_start)
                return carry
            return body

        def unused_fill(do_start):
            def body(b, carry):
                go(pltpu.make_async_copy(zero_ref, _packed_rows(buf_ref, b * EXPERT_ROWS, EXPERT_ROWS), zsem), do_start)
                return carry
            return body

        handed_out = jnp.int32(0)
        for e in range(N_EXPERTS):
            handed_out = handed_out + (csm_ref[0, e, 0] + (EXPERT_ROWS - 1)) // EXPERT_ROWS
        for do_start in (True, False):
            lax.fori_loop(0, N_EXPERTS, tail_fill(do_start), 0)
            lax.fori_loop(handed_out, n_blocks, unused_fill(do_start), 0)

        scatter_wait(prev)
        scatter_wait(slot)


def _expert_kernel(be_ref, nv_ref, bi_ref, bo_ref, nxt_ref, x_ref, wg_ref, wu_ref, wd_ref, y_ref,
                   wgb_ref, wub_ref, wdb_ref, wgf_ref, wuf_ref, wdf_ref, wsem):
    i = pl.program_id(0)
    nv = nv_ref[i]
    expert = be_ref[i]
    new_expert = jnp.logical_or(i == 0, expert != be_ref[jnp.maximum(i - 1, 0)])

    def weight_copies(e):
        return (pltpu.make_async_copy(wg_ref.at[e], wgf_ref, wsem.at[0]),
                pltpu.make_async_copy(wu_ref.at[e], wuf_ref, wsem.at[1]),
                pltpu.make_async_copy(wd_ref.at[e], wdf_ref, wsem.at[2]))

    @pl.when(new_expert)
    def _():
        @pl.when(i == 0)
        def _():
            for c in weight_copies(expert):
                c.start()
        for c in weight_copies(expert):
            c.wait()
        wgb_ref[...] = wgf_ref[...].astype(BF16)
        wub_ref[...] = wuf_ref[...].astype(BF16)
        wdb_ref[...] = wdf_ref[...].astype(BF16)
        following = nxt_ref[i]

        @pl.when(following >= 0)
        def _():
            for c in weight_copies(following):
                c.start()

    @pl.when(nv == 0)
    def _():
        y_ref[...] = jnp.zeros_like(y_ref)

    @pl.when(nv > 0)
    def _():
        x = _load_packed_tokens(x_ref, 0, EXPERT_ROWS)
        gate = jnp.dot(x, wgb_ref[...], preferred_element_type=F32)
        up = jnp.dot(x, wub_ref[...], preferred_element_type=F32)
        act = (gate * jax.nn.sigmoid(gate) * up).astype(BF16)
        _store_packed_tokens(y_ref, 0, jnp.dot(act, wdb_ref[...], preferred_element_type=F32))


def _combine_kernel(dcur_ref, dnext_ref, h_ref, pw_ref, g_ref, b_ref, y_ref, o_ref, yb0, yb1, yb2, yb3, sems):
    part = COMBINE_PART_ROWS
    bufs = (yb0, yb1, yb2, yb3)
    step = pl.program_id(0)
    last = pl.num_programs(0) - 1

    def copy(src_tok, ybuf, t, slot, sem):
        return pltpu.make_async_copy(_packed_rows(y_ref, src_tok), _packed_rows(ybuf.at[slot], t), sem)

    def issue(dref, col0, ybuf, sem):
        for t0 in range(0, part, INDEX_GROUP):
            src = [[dref[slot, col0 + t0 + k] for slot in range(TOP_K)] for k in range(INDEX_GROUP)]
            for k in range(INDEX_GROUP):
                for slot in range(TOP_K):
                    copy(src[k][slot], ybuf, t0 + k, slot, sem).start(priority=slot % 2)

    def wait(ybuf, sem):
        for slot in range(TOP_K):
            pltpu.make_async_copy(_packed_rows(y_ref, 0, part), ybuf.at[slot], sem).wait()

    def finish(ybuf, r0):
        rows = pl.ds(r0, part)
        pw = pw_ref[rows, :]
        y = (pw[:, 0:1] * _load_packed_tokens(ybuf.at[0], 0, part, F32)
             + pw[:, 1:2] * _load_packed_tokens(ybuf.at[1], 0, part, F32))
        o_ref[rows, :] = _layer_norm(ALPHA * h_ref[rows, :] + y, g_ref[...], b_ref[...], LN_EPS)

    @pl.when(step == 0)
    def _():
        for p in range(2):
            def first(t, _, p=p):
                for slot in range(TOP_K):
                    copy(dcur_ref[slot, p * part + t], bufs[p], t, slot, sems.at[p]).start()
                return 0
            lax.fori_loop(0, part, first, 0)

    for p in range(COMBINE_PHASES):
        ahead = (p + 2) % COMBINE_PHASES
        wait(bufs[p], sems.at[p])
        if p + 2 < COMBINE_PHASES:
            issue(dcur_ref, (p + 2) * part, bufs[ahead], sems.at[ahead])
        else:
            issue(dnext_ref, ahead * part, bufs[ahead], sems.at[ahead])
        finish(bufs[p], p * part)

    @pl.when(step == last)
    def _():
        wait(bufs[0], sems.at[0])
        wait(bufs[1], sems.at[1])


def _tables(seq):
    log_g = jnp.log1p(-jnp.power(2.0, -5.0 - jnp.arange(RET_HEADS, dtype=F32)))
    i = jnp.arange(CHUNK, dtype=F32)
    rel = i[:, None] - i[None, :]
    dmask = jnp.where(rel[None] >= 0, jnp.exp(jnp.maximum(rel, 0.0)[None] * log_g[:, None, None]), 0.0)
    dmask = dmask.reshape(RET_HEADS * CHUNK, CHUNK)
    lg_lane = jnp.repeat(log_g, RET_DK)
    xi = jnp.exp((i + 1)[:, None] * lg_lane[None, :])
    zeta = jnp.exp((CHUNK - 1 - i)[:, None] * lg_lane[None, :])
    decay = jnp.broadcast_to(jnp.exp(CHUNK * lg_lane)[:, None], (RET_QK_WIDTH, RET_WIDTH))
    bd = (jnp.arange(RET_QK_WIDTH)[:, None] // RET_DK == jnp.arange(RET_WIDTH)[None, :] // RET_DV).astype(F32)
    zeta_meta = zeta[CHUNK - N_META:, :]
    half = RET_DK // 2
    inv = ROPE_BASE ** (-jnp.arange(half, dtype=F32) / half)
    pos = jnp.arange(N_META + seq, dtype=F32)
    ang = pos[:, None] * inv[None, :]
    cos = jnp.tile(jnp.cos(ang), (1, 2 * RET_HEADS))
    sin = jnp.tile(jnp.concatenate([-jnp.sin(ang), jnp.sin(ang)], axis=1), (1, RET_HEADS))
    return dict(dmask=dmask, xi=xi, zeta=zeta, decay=decay, bd=bd, zeta_meta=zeta_meta,
                cos_meta=cos[:N_META], sin_meta=sin[:N_META], cos=cos[N_META:], sin=sin[N_META:])


def _full(shape):
    return pl.BlockSpec(shape, lambda *_: (0,) * len(shape))


def kernel(x, meta_tokens, ln_emb_g, ln_emb_b, w_in, pool_w, pool_b, pool_scale, w_out, ln1_g, ln1_b, router_group_w, router_group_b, router_expert_w, router_expert_b, expert_w_gate, expert_w_up, expert_w_down, ln2_g, ln2_b):
    batch, seq, d = x.shape
    assert d == D_MODEL and seq % MIX_ROWS == 0 and MIX_ROWS <= EXPERT_ROWS
    n_tok = batch * seq
    t_blocks = seq // MIX_ROWS
    tb = _tables(seq)

    row = lambda a: a.reshape(1, -1).astype(F32)
    win_b = w_in[0].astype(BF16)
    wout_b = w_out[0].astype(BF16)
    poolw_b = pool_w[0].astype(BF16)
    wr = jnp.zeros((ROUTER_ROWS, D_MODEL), F32)
    wr = wr.at[0:N_GROUPS].set(router_group_w[0].T).at[8:8 + N_EXPERTS].set(router_expert_w[0].T).astype(BF16)
    br = jnp.zeros((ROUTER_ROWS, 1), F32)
    br = br.at[0:N_GROUPS, 0].set(router_group_b[0]).at[8:8 + N_EXPERTS, 0].set(router_expert_b[0])

    s0, tail0 = pl.pallas_call(
        _meta_kernel,
        out_shape=(jax.ShapeDtypeStruct((RET_QK_WIDTH, RET_WIDTH), F32), jax.ShapeDtypeStruct((N_META, POOL_WIDTH), F32)),
        name="meta_prep",
    )(meta_tokens.astype(F32), row(ln_emb_g), row(ln_emb_b), win_b, tb["cos_meta"], tb["sin_meta"], tb["zeta_meta"], tb["bd"])

    tok_spec = pl.BlockSpec((None, MIX_ROWS, D_MODEL), lambda b, j: (b, j, 0))
    pair_spec = pl.BlockSpec((TOP_K, MIX_ROWS), lambda b, j: (0, b * t_blocks + j))
    rope_spec = pl.BlockSpec((MIX_ROWS, RET_QK_WIDTH), lambda b, j: (j, 0))
    n_blocks = (n_tok * TOP_K) // EXPERT_ROWS + N_EXPERTS
    assert n_blocks + 2 <= BLOCK_TABLE_LANES
    packed_block = (EXPERT_ROWS * PACKED_SUBLANES, LANES)
    sorted_shape = (n_blocks * EXPERT_ROWS * PACKED_SUBLANES, LANES)
    buf_shape = ((n_blocks + 2) * EXPERT_ROWS * PACKED_SUBLANES, LANES)
    stage_block = (MIX_ROWS * PACKED_SUBLANES, LANES)
    h1, dest, pair_w, blk_table, buf = pl.pallas_call(
        _mixer_kernel,
        grid=(batch, t_blocks),
        in_specs=[tok_spec, _full((1, D_MODEL)), _full((1, D_MODEL)), _full((D_MODEL, IN_COLS)), _full((D_MODEL, D_MODEL)),
                  _full((POOL_GROUPS, POOL_CH, POOL_CH)), _full((1, POOL_WIDTH)), _full((1, POOL_WIDTH)),
                  _full((1, D_MODEL)), _full((1, D_MODEL)), rope_spec, rope_spec,
                  _full((RET_HEADS * CHUNK, CHUNK)), _full((CHUNK, RET_QK_WIDTH)), _full((CHUNK, RET_QK_WIDTH)),
                  _full((RET_QK_WIDTH, RET_WIDTH)), _full((RET_QK_WIDTH, RET_WIDTH)),
                  _full((RET_QK_WIDTH, RET_WIDTH)), _full((N_META, POOL_WIDTH)),
                  _full((ROUTER_ROWS, D_MODEL)), _full((ROUTER_ROWS, 1))],
        out_specs=[tok_spec, pair_spec, pair_spec,
                   _full((8, BLOCK_TABLE_LANES)),
                   pl.BlockSpec(memory_space=pl.ANY)],
        out_shape=[jax.ShapeDtypeStruct((batch, seq, D_MODEL), F32),
                   jax.ShapeDtypeStruct((TOP_K, n_tok), jnp.int32),
                   jax.ShapeDtypeStruct((TOP_K, n_tok), F32),
                   jax.ShapeDtypeStruct((8, BLOCK_TABLE_LANES), jnp.int32),
                   jax.ShapeDtypeStruct(buf_shape, jnp.uint32)],
        scratch_shapes=[pltpu.VMEM((RET_QK_WIDTH, RET_WIDTH), F32),
                        pltpu.VMEM((N_META, POOL_WIDTH), F32),
                        pltpu.VMEM((MIX_ROWS + N_META, POOL_WIDTH), F32),
                        pltpu.VMEM((MIX_ROWS, IN_COLS), F32),
                        pltpu.VMEM((MIX_ROWS, D_MODEL), F32),
                        pltpu.VMEM((MIX_ROWS, D_MODEL), BF16),
                        pltpu.VMEM((MIX_ROWS, D_MODEL), BF16),
                        pltpu.VMEM((MIX_ROWS, MIX_ROWS), BF16),
                        pltpu.VMEM((N_EXPERTS, LANES), F32),
                        pltpu.VMEM((N_EXPERTS, LANES), F32),
                        pltpu.VMEM((8, LANES), F32),
                        pltpu.VMEM((8, BLOCK_TABLE_LANES), F32),
                        pltpu.VMEM((2,) + stage_block, jnp.uint32),
                        pltpu.VMEM(packed_block, jnp.uint32),
                        pltpu.VMEM((TOP_K, MIX_ROWS), jnp.int32),
                        pltpu.SMEM((TOP_K, MIX_ROWS), jnp.int32),
                        pltpu.VMEM((N_EXPERTS, LANES), jnp.int32),
                        pltpu.VMEM((N_EXPERTS, LANES), jnp.int32),
                        pltpu.SMEM((2, N_EXPERTS, LANES), jnp.int32),
                        pltpu.SemaphoreType.DMA((2,)),
                        pltpu.SemaphoreType.DMA, pltpu.SemaphoreType.DMA],
        compiler_params=pltpu.CompilerParams(dimension_semantics=("arbitrary", "arbitrary"),
                                             vmem_limit_bytes=VMEM_LIMIT),
        name="mixer",
    )(x, row(ln_emb_g), row(ln_emb_b), win_b, wout_b, poolw_b, row(pool_b[0]), row(pool_scale[0]),
      row(ln1_g[0]), row(ln1_b[0]), tb["cos"], tb["sin"], tb["dmask"], tb["xi"], tb["zeta"], tb["decay"], tb["bd"],
      s0, tail0, wr, br)

    bidx = jnp.arange(n_blocks, dtype=jnp.int32)
    owner = blk_table[0, :n_blocks]
    key = owner * BLOCK_TABLE_LANES + bidx
    place = jnp.sum(key[None, :] < key[:, None], axis=1)
    order = jnp.sum(jnp.where(place[None, :] == bidx[:, None], bidx[None, :], 0), axis=1).astype(jnp.int32)
    owner_in_order = jnp.sum(jnp.where(place[None, :] == bidx[:, None], owner[None, :], 0), axis=1)
    n_used = jnp.sum(owner < N_EXPERTS)
    used = bidx < n_used
    last_used = jnp.maximum(n_used - 1, 0)
    blk_in = jnp.where(used, order, order[last_used]).astype(jnp.int32)
    blk_e = jnp.where(used, owner_in_order, owner_in_order[last_used]).astype(jnp.int32)
    blk_nv = used.astype(jnp.int32)
    later = jnp.logical_and(owner_in_order[None, :] > blk_e[:, None], used[None, :])
    blk_next = jnp.min(jnp.where(later, owner_in_order[None, :], N_EXPERTS), axis=1)
    blk_next = jnp.where(blk_next < N_EXPERTS, blk_next, -1).astype(jnp.int32)

    y_sorted = pl.pallas_call(
        _expert_kernel,
        grid_spec=pltpu.PrefetchScalarGridSpec(
            num_scalar_prefetch=5,
            grid=(n_blocks,),
            in_specs=[pl.BlockSpec(packed_block, lambda i, be, nv, bi, bo, nx: (bi[i], 0)),
                      pl.BlockSpec(memory_space=pl.ANY), pl.BlockSpec(memory_space=pl.ANY),
                      pl.BlockSpec(memory_space=pl.ANY)],
            out_specs=pl.BlockSpec(packed_block, lambda i, be, nv, bi, bo, nx: (bo[i], 0)),
            scratch_shapes=[pltpu.VMEM((D_MODEL, D_EXPERT), BF16), pltpu.VMEM((D_MODEL, D_EXPERT), BF16),
                            pltpu.VMEM((D_EXPERT, D_MODEL), BF16),
                            pltpu.VMEM((D_MODEL, D_EXPERT), F32), pltpu.VMEM((D_MODEL, D_EXPERT), F32),
                            pltpu.VMEM((D_EXPERT, D_MODEL), F32), pltpu.SemaphoreType.DMA((3,))],
        ),
        out_shape=jax.ShapeDtypeStruct(sorted_shape, jnp.uint32),
        compiler_params=pltpu.CompilerParams(dimension_semantics=("arbitrary",), vmem_limit_bytes=VMEM_LIMIT),
        name="experts",
    )(blk_e, blk_nv, blk_in, order, blk_next, buf, expert_w_gate[0], expert_w_up[0], expert_w_down[0])

    comb_rows = COMBINE_PHASES * COMBINE_PART_ROWS
    comb_steps = n_tok // comb_rows
    assert n_tok % comb_rows == 0
    rows_spec = pl.BlockSpec((comb_rows, D_MODEL), lambda i: (i, 0))
    out = pl.pallas_call(
        _combine_kernel,
        grid=(comb_steps,),
        in_specs=[pl.BlockSpec((TOP_K, comb_rows), lambda i: (0, i), memory_space=pltpu.SMEM),
                  pl.BlockSpec((TOP_K, comb_rows), lambda i: (0, jnp.minimum(i + 1, comb_steps - 1)),
                               memory_space=pltpu.SMEM),
                  rows_spec,
                  pl.BlockSpec((comb_rows, TOP_K), lambda i: (i, 0)),
                  pl.BlockSpec((1, D_MODEL), lambda i: (0, 0)), pl.BlockSpec((1, D_MODEL), lambda i: (0, 0)),
                  pl.BlockSpec(memory_space=pl.ANY)],
        out_specs=rows_spec,
        out_shape=jax.ShapeDtypeStruct((n_tok, D_MODEL), F32),
        scratch_shapes=[pltpu.VMEM((TOP_K, COMBINE_PART_ROWS * PACKED_SUBLANES, LANES), jnp.uint32)] * COMBINE_PHASES
                       + [pltpu.SemaphoreType.DMA((COMBINE_PHASES,))],
        compiler_params=pltpu.CompilerParams(dimension_semantics=("arbitrary",), vmem_limit_bytes=VMEM_LIMIT),
        name="combine",
    )(dest, dest, h1.reshape(n_tok, D_MODEL), pair_w.T, row(ln2_g[0]), row(ln2_b[0]), y_sorted)
    return out.reshape(batch, seq, D_MODEL)
```

```python
import jax
import jax.numpy as jnp
from jax import lax
from jax.experimental import pallas as pl
from jax.experimental.pallas import tpu as pltpu

D_MODEL = 1024
DEPTH = 1
N_META = 16
RET_HEADS = 4
RET_WIDTH = D_MODEL // 2
RET_DV = RET_WIDTH // RET_HEADS
RET_DK = RET_DV // 2
RET_QK_WIDTH = RET_HEADS * RET_DK
CHUNK = 128
ROPE_BASE = 10000.0
POOL_WINDOWS = (2, 4, 8, 16)
POOL_GROUPS = len(POOL_WINDOWS)
POOL_WIDTH = D_MODEL // 2
POOL_CH = POOL_WIDTH // POOL_GROUPS
IN_COLS = 2 * RET_QK_WIDTH + 2 * RET_WIDTH + POOL_WIDTH
N_GROUPS = 4
EXPERTS_PER_GROUP = 8
N_EXPERTS = N_GROUPS * EXPERTS_PER_GROUP
D_EXPERT = D_MODEL // 2
TOP_K = 2
LN_EPS = 1e-5
GN_EPS = 1e-6
ALPHA = (2 * DEPTH) ** 0.25

Q0, K0, V0, G0, U0 = 0, RET_QK_WIDTH, 2 * RET_QK_WIDTH, 2 * RET_QK_WIDTH + RET_WIDTH, 2 * RET_QK_WIDTH + 2 * RET_WIDTH

MIX_ROWS = 512
EXPERT_ROWS = 512
ROUTER_ROWS = 40
BLOCK_TABLE_LANES = 384
COMBINE_PHASES = 4
COMBINE_PART_ROWS = 128
INDEX_GROUP = 8
VMEM_LIMIT = 56 * 1024 * 1024
LANES = 128
PACKED_SUBLANES = D_MODEL // (2 * LANES)

F32 = jnp.float32
BF16 = jnp.bfloat16


def _layer_norm(x, g, b, eps):
    mu = jnp.mean(x, axis=-1, keepdims=True)
    xc = x - mu
    var = jnp.mean(xc * xc, axis=-1, keepdims=True)
    return xc * lax.rsqrt(var + eps) * g + b


def _rotary(z, cos, sin_signed, first_half):
    partner = jnp.where(first_half, pltpu.roll(z, RET_QK_WIDTH - RET_DK // 2, 1), pltpu.roll(z, RET_DK // 2, 1))
    return z * cos + partner * sin_signed


def _store_packed_tokens(ref, tok0, x):
    n, half = x.shape[0], D_MODEL // 2
    lo = lax.bitcast_convert_type(x[:, :half].astype(BF16).astype(F32), jnp.uint32) >> 16
    hi = lax.bitcast_convert_type(x[:, half:].astype(BF16).astype(F32), jnp.uint32) & jnp.uint32(0xFFFF0000)
    words = lo | hi
    for s in range(PACKED_SUBLANES):
        ref[pl.ds(tok0 * PACKED_SUBLANES + s, n, stride=PACKED_SUBLANES), :] = words[:, s * LANES:(s + 1) * LANES]


def _load_packed_tokens(ref, tok0, n, dtype=BF16):
    words = [ref[pl.ds(tok0 * PACKED_SUBLANES + s, n, stride=PACKED_SUBLANES), :] for s in range(PACKED_SUBLANES)]
    lo = [lax.bitcast_convert_type(w << 16, F32).astype(dtype) for w in words]
    hi = [lax.bitcast_convert_type(w & jnp.uint32(0xFFFF0000), F32).astype(dtype) for w in words]
    return jnp.concatenate(lo + hi, axis=1)


def _packed_rows(ref, tok, n=1):
    return ref.at[pl.ds(pl.multiple_of(tok * PACKED_SUBLANES, PACKED_SUBLANES), n * PACKED_SUBLANES), :]


def _first_half_mask(rows):
    lane = lax.broadcasted_iota(jnp.int32, (rows, RET_QK_WIDTH), 1)
    return (lane % RET_DK) < (RET_DK // 2)


def _meta_kernel(meta_ref, g_ref, b_ref, win_ref, cos_ref, sin_ref, zeta_ref, bd_ref, s0_ref, tail_ref):
    h = _layer_norm(meta_ref[...], g_ref[...], b_ref[...], LN_EPS)
    proj = jnp.dot(h.astype(BF16), win_ref[...], preferred_element_type=F32)
    k = _rotary(proj[:, K0:V0], cos_ref[...], sin_ref[...], _first_half_mask(N_META)) * (RET_DK ** -0.5)
    kz = (k * zeta_ref[...]).astype(BF16)
    v = proj[:, V0:G0].astype(BF16)
    kv = lax.dot_general(kz, v, (((0,), (0,)), ((), ())), preferred_element_type=F32)
    s0_ref[...] = kv * bd_ref[...]
    tail_ref[...] = proj[:, U0:]


def _mixer_kernel(x_ref, lng_ref, lnb_ref, win_ref, wout_ref, poolw_ref, poolb_ref, pools_ref, ln1g_ref, ln1b_ref,
                  cos_ref, sin_ref, dmask_ref, xi_ref, zeta_ref, decay_ref, bd_ref, s0_ref, tail0_ref,
                  wr_ref, br_ref,
                  h1_ref, dest_ref, pw_ref, table_ref, buf_ref,
                  state_ref, tail_ref, uext_ref, proj_ref, h0_ref, h0b_ref, mixin_ref, tri_ref, carry_ref,
                  curblk_ref, nalloc_ref, tab_ref, hkeep_ref, zero_ref, dvm_ref, dsm_ref, cnt_ref, cur_ref, csm_ref,
                  scat_sem, idx_sem, zsem):
    rows = x_ref.shape[0]
    n_chunks = rows // CHUNK
    step = pl.program_id(0) * pl.num_programs(1) + pl.program_id(1)
    n_steps = pl.num_programs(0) * pl.num_programs(1)
    first_step = step == 0
    slot = lax.rem(step, 2)
    prev = 1 - slot
    n_blocks = buf_ref.shape[0] // (EXPERT_ROWS * PACKED_SUBLANES) - 2

    def scatter_wait(b):
        for _ in range(TOP_K):
            pltpu.make_async_copy(hkeep_ref.at[b], _packed_rows(buf_ref, 0, rows), scat_sem.at[b]).wait()

    @pl.when(first_step)
    def _():
        r = lax.broadcasted_iota(jnp.int32, (rows, rows), 0)
        c = lax.broadcasted_iota(jnp.int32, (rows, rows), 1)
        tri_ref[...] = jnp.where(r < c, 1.0, 0.0).astype(BF16)
        carry_ref[...] = jnp.zeros_like(carry_ref)
        curblk_ref[...] = jnp.full(curblk_ref.shape, -1.0, F32)
        nalloc_ref[...] = jnp.zeros_like(nalloc_ref)
        tab_ref[...] = jnp.full(tab_ref.shape, float(N_EXPERTS), F32)
        zero_ref[...] = jnp.zeros_like(zero_ref)
        hkeep_ref[1] = jnp.zeros(hkeep_ref.shape[1:], hkeep_ref.dtype)

        def spare(t, _):
            for s in range(TOP_K):
                dsm_ref[s, t] = (n_blocks + s) * EXPERT_ROWS + t
            return 0
        lax.fori_loop(0, rows, spare, 0)

    @pl.when(pl.program_id(1) == 0)
    def _():
        state_ref[...] = s0_ref[...]
        tail_ref[...] = tail0_ref[...]

    @pl.when(step > 0)
    def _():
        pltpu.make_async_copy(dvm_ref, dsm_ref, idx_sem).wait()

    def scatter_previous(t_lo, t_hi):
        for t0 in range(t_lo, t_hi, INDEX_GROUP):
            dst = [[dsm_ref[s, t0 + k] for s in range(TOP_K)] for k in range(INDEX_GROUP)]
            for k in range(INDEX_GROUP):
                for s in range(TOP_K):
                    pltpu.make_async_copy(_packed_rows(hkeep_ref.at[prev], t0 + k), _packed_rows(buf_ref, dst[k][s]),
                                          scat_sem.at[prev]).start(priority=s % 2)

    def ln_body(c, _):
        sl = pl.ds(pl.multiple_of(c * CHUNK, CHUNK), CHUNK)
        h0 = _layer_norm(x_ref[sl, :], lng_ref[...], lnb_ref[...], LN_EPS)
        h0_ref[sl, :] = h0
        h0b_ref[sl, :] = h0.astype(BF16)
        return 0
    lax.fori_loop(0, n_chunks, ln_body, 0, unroll=True)

    proj_ref[...] = jnp.dot(h0b_ref[...], win_ref[...], preferred_element_type=F32)
    scatter_previous(0, rows)

    first_half = _first_half_mask(CHUNK)
    head_of_lane = lax.broadcasted_iota(jnp.int32, (CHUNK, RET_QK_WIDTH), 1) // RET_DK

    def ret_body(c, _):
        sl = pl.ds(pl.multiple_of(c * CHUNK, CHUNK), CHUNK)
        cos = cos_ref[sl, :]
        sin = sin_ref[sl, :]
        q = _rotary(proj_ref[sl, Q0:K0], cos, sin, first_half)
        k = _rotary(proj_ref[sl, K0:V0], cos, sin, first_half) * (RET_DK ** -0.5)
        qb = q.astype(BF16)
        kb = k.astype(BF16)
        vb = proj_ref[sl, V0:G0].astype(BF16)
        q_heads = jnp.concatenate(
            [jnp.where(head_of_lane == h, qb, jnp.zeros_like(qb)) for h in range(RET_HEADS)], axis=0)
        scores = lax.dot_general(q_heads, kb, (((1,), (1,)), ((), ())), preferred_element_type=F32)
        p = (scores * dmask_ref[...]).astype(BF16)
        inner = jnp.concatenate(
            [jnp.dot(p[h * CHUNK:(h + 1) * CHUNK, :], vb[:, h * RET_DV:(h + 1) * RET_DV],
                     preferred_element_type=F32) for h in range(RET_HEADS)], axis=1)
        state = state_ref[...]
        cross = jnp.dot((q * xi_ref[...]).astype(BF16), state.astype(BF16), preferred_element_type=F32)
        kz = (k * zeta_ref[...]).astype(BF16)
        kv = lax.dot_general(kz, vb, (((0,), (0,)), ((), ())), preferred_element_type=F32)
        state_ref[...] = state * decay_ref[...] + kv * bd_ref[...]
        ret = inner + cross
        gate = proj_ref[sl, G0:U0]
        outs = []
        for h in range(RET_HEADS):
            o = ret[:, h * RET_DV:(h + 1) * RET_DV]
            mu = jnp.mean(o, axis=-1, keepdims=True)
            oc = o - mu
            var = jnp.mean(oc * oc, axis=-1, keepdims=True)
            outs.append(oc * lax.rsqrt(var + GN_EPS))
        gn = jnp.concatenate(outs, axis=1)
        mixin_ref[sl, 0:RET_WIDTH] = (gate * jax.nn.sigmoid(gate) * gn).astype(BF16)
        return 0
    lax.fori_loop(0, n_chunks, ret_body, 0, unroll=True)

    uext_ref[0:N_META, :] = tail_ref[...]
    uext_ref[N_META:, :] = proj_ref[:, U0:]
    tail_ref[...] = uext_ref[rows:, :]
    for g, w in enumerate(POOL_WINDOWS):
        lanes = slice(g * POOL_CH, (g + 1) * POOL_CH)
        e = uext_ref[:, lanes]
        acc = e
        shift = 1
        while shift < w:
            acc = acc + pltpu.roll(acc, shift, 0)
            shift *= 2
        pooled = acc[N_META:, :] * (1.0 / w) - e[N_META:, :]
        mixed = jnp.dot(pooled.astype(BF16), poolw_ref[g], preferred_element_type=F32) + poolb_ref[:, lanes]
        mixin_ref[:, RET_WIDTH + g * POOL_CH:RET_WIDTH + (g + 1) * POOL_CH] = (mixed * pools_ref[:, lanes]).astype(BF16)

    proj_ref[:, 0:D_MODEL] = jnp.dot(mixin_ref[...], wout_ref[...], preferred_element_type=F32)

    @pl.when(step >= 1)
    def _():
        scatter_wait(slot)

    keep = hkeep_ref.at[slot]

    def ln1_body(c, _):
        sl = pl.ds(pl.multiple_of(c * CHUNK, CHUNK), CHUNK)
        h1 = _layer_norm(ALPHA * h0_ref[sl, :] + proj_ref[sl, 0:D_MODEL], ln1g_ref[...], ln1b_ref[...], LN_EPS)
        h1_ref[sl, :] = h1
        _store_packed_tokens(keep, c * CHUNK, h1)
        h0b_ref[sl, :] = h1.astype(BF16)
        return 0
    lax.fori_loop(0, n_chunks, ln1_body, 0, unroll=True)

    logits = lax.dot_general(wr_ref[...], h0b_ref[...], (((1,), (1,)), ((), ())), preferred_element_type=F32)
    logits = logits + br_ref[...]
    gl = logits[0:N_GROUPS, :]
    gmax = jnp.max(gl, axis=0, keepdims=True)
    g_p = 1.0 / jnp.sum(jnp.exp(gl - gmax), axis=0, keepdims=True)
    grow = lax.broadcasted_iota(jnp.int32, gl.shape, 0)
    g_idx = jnp.min(jnp.where(gl == gmax, grow, N_GROUPS), axis=0, keepdims=True)
    sel = logits[8:8 + EXPERTS_PER_GROUP, :]
    for g in range(1, N_GROUPS):
        sel = jnp.where(g_idx == g, logits[8 + g * EXPERTS_PER_GROUP:8 + (g + 1) * EXPERTS_PER_GROUP, :], sel)
    erow = lax.broadcasted_iota(jnp.int32, sel.shape, 0)
    m1 = jnp.max(sel, axis=0, keepdims=True)
    i1 = jnp.min(jnp.where(sel == m1, erow, EXPERTS_PER_GROUP), axis=0, keepdims=True)
    sel2 = jnp.where(erow == i1, -jnp.inf, sel)
    m2 = jnp.max(sel2, axis=0, keepdims=True)
    i2 = jnp.min(jnp.where(sel2 == m2, erow, EXPERTS_PER_GROUP), axis=0, keepdims=True)
    e2 = jnp.exp(m2 - m1)
    w1 = 1.0 / (1.0 + e2)
    w2 = e2 / (1.0 + e2)
    pe0 = g_idx * EXPERTS_PER_GROUP + i1
    pe1 = g_idx * EXPERTS_PER_GROUP + i2
    pw_ref[...] = jnp.concatenate([g_p * w1, g_p * w2], axis=0)
    xrow = lax.broadcasted_iota(jnp.int32, (N_EXPERTS, rows), 0)
    oh0 = xrow == pe0
    oh1 = xrow == pe1
    oh = jnp.where(jnp.logical_or(oh0, oh1), 1.0, 0.0)
    carry = carry_ref[...]
    count = carry[:, 0:1]
    prefix = jnp.dot(oh.astype(BF16), tri_ref[...], preferred_element_type=F32) + count

    inv_rows = 1.0 / EXPERT_ROWS
    added = jnp.sum(oh, axis=1, keepdims=True)
    blocks_old = jnp.floor((count + (EXPERT_ROWS - 1)) * inv_rows)
    blocks_new = jnp.floor((count + added + (EXPERT_ROWS - 1)) * inv_rows)
    need = blocks_new - blocks_old
    er = lax.broadcasted_iota(jnp.int32, (N_EXPERTS, N_EXPERTS), 0)
    ec = lax.broadcasted_iota(jnp.int32, (N_EXPERTS, N_EXPERTS), 1)
    earlier = jnp.where(ec < er, 1.0, 0.0)
    need_lanes = jnp.broadcast_to(need, (N_EXPERTS, LANES))
    before = jnp.dot(earlier.astype(BF16), need_lanes.astype(BF16), preferred_element_type=F32)[:, 0:1]
    nalloc = nalloc_ref[0:1, 0:1]
    new_blk = nalloc + before
    cur_blk = curblk_ref[:, 0:1]
    ordinal = jnp.floor(prefix * inv_rows)
    blk = jnp.where(ordinal < blocks_old, cur_blk, new_blk)
    row_in_buf = blk * EXPERT_ROWS + (prefix - ordinal * EXPERT_ROWS)
    dest0 = jnp.sum(jnp.where(oh0, row_in_buf, 0.0), axis=0, keepdims=True)
    dest1 = jnp.sum(jnp.where(oh1, row_in_buf, 0.0), axis=0, keepdims=True)
    dest = jnp.concatenate([dest0, dest1], axis=0).astype(jnp.int32)
    dest_ref[...] = dest
    dvm_ref[...] = dest
    pltpu.make_async_copy(dvm_ref, dsm_ref, idx_sem).start()

    got_new = need > 0.0
    blk_lane = lax.broadcasted_iota(jnp.int32, (N_EXPERTS, BLOCK_TABLE_LANES), 1).astype(F32)
    erow_f = lax.broadcasted_iota(jnp.int32, (N_EXPERTS, BLOCK_TABLE_LANES), 0).astype(F32)
    owner = jnp.max(jnp.where(jnp.logical_and(got_new, blk_lane == new_blk), erow_f, -1.0), axis=0, keepdims=True)
    tab = jnp.where(owner >= 0.0, owner, tab_ref[0:1, :])
    tab_ref[...] = jnp.broadcast_to(tab, tab_ref.shape)
    cur_blk = jnp.where(got_new, new_blk, cur_blk)
    curblk_ref[...] = jnp.broadcast_to(cur_blk, curblk_ref.shape)
    nalloc_ref[...] = jnp.broadcast_to(nalloc + jnp.sum(need, axis=0, keepdims=True), nalloc_ref.shape)
    carry = carry + added
    carry_ref[...] = carry
    cnt_ref[...] = carry.astype(jnp.int32)
    cur_ref[...] = jnp.broadcast_to(cur_blk, cur_ref.shape).astype(jnp.int32)
    table_ref[...] = jnp.broadcast_to(tab, table_ref.shape).astype(jnp.int32)

    @pl.when(step == n_steps - 1)
    def _():
        pltpu.make_async_copy(dvm_ref, dsm_ref, idx_sem).wait()

        def last_rows(t, _):
            for s in range(TOP_K):
                pltpu.make_async_copy(_packed_rows(keep, t), _packed_rows(buf_ref, dsm_ref[s, t]), scat_sem.at[slot]).start()
            return 0
        lax.fori_loop(0, rows, last_rows, 0)

        csm_cnt = pltpu.make_async_copy(cnt_ref, csm_ref.at[0], zsem)
        csm_cur = pltpu.make_async_copy(cur_ref, csm_ref.at[1], zsem)
        csm_cnt.start()
        csm_cur.start()
        csm_cnt.wait()
        csm_cur.wait()

        def go(copy, do_start):
            if do_start:
                copy.start()
            else:
                copy.wait()

        def tail_fill(do_start):
            def body(e, carry):
                cnt_e = csm_ref[0, e, 0]
                gap = (EXPERT_ROWS - lax.rem(cnt_e, EXPERT_ROWS)) % EXPERT_ROWS
                first = csm_ref[1, e, 0] * EXPERT_ROWS + (EXPERT_ROWS - gap)
                for bit in range(EXPERT_ROWS.bit_length() - 1):
                    run = 1 << bit

                    @pl.when((gap >> bit) & 1 == 1)
                    def _():
                        go(pltpu.make_async_copy(_packed_rows(zero_ref, 0, run),
                                                 _packed_rows(buf_ref, first + (gap & (run - 1)), run), zsem), do_start)
                return carry
            return body

        def unused_fill(do_start):
            def body(b, carry):
                go(pltpu.make_async_copy(zero_ref, _packed_rows(buf_ref, b * EXPERT_ROWS, EXPERT_ROWS), zsem), do_start)
                return carry
            return body

        handed_out = jnp.int32(0)
        for e in range(N_EXPERTS):
            handed_out = handed_out + (csm_ref[0, e, 0] + (EXPERT_ROWS - 1)) // EXPERT_ROWS
        for do_start in (True, False):
            lax.fori_loop(0, N_EXPERTS, tail_fill(do_start), 0)
            lax.fori_loop(handed_out, n_blocks, unused_fill(do_start), 0)

        scatter_wait(prev)
        scatter_wait(slot)


def _expert_kernel(be_ref, nv_ref, bi_ref, bo_ref, nxt_ref, x_ref, wg_ref, wu_ref, wd_ref, y_ref,
                   wgb_ref, wub_ref, wdb_ref, wgf_ref, wuf_ref, wdf_ref, wsem):
    i = pl.program_id(0)
    nv = nv_ref[i]
    expert = be_ref[i]
    new_expert = jnp.logical_or(i == 0, expert != be_ref[jnp.maximum(i - 1, 0)])

    def weight_copies(e):
        return (pltpu.make_async_copy(wg_ref.at[e], wgf_ref, wsem.at[0]),
                pltpu.make_async_copy(wu_ref.at[e], wuf_ref, wsem.at[1]),
                pltpu.make_async_copy(wd_ref.at[e], wdf_ref, wsem.at[2]))

    @pl.when(new_expert)
    def _():
        @pl.when(i == 0)
        def _():
            for c in weight_copies(expert):
                c.start()
        for c in weight_copies(expert):
            c.wait()
        wgb_ref[...] = wgf_ref[...].astype(BF16)
        wub_ref[...] = wuf_ref[...].astype(BF16)
        wdb_ref[...] = wdf_ref[...].astype(BF16)
        following = nxt_ref[i]

        @pl.when(following >= 0)
        def _():
            for c in weight_copies(following):
                c.start()

    @pl.when(nv == 0)
    def _():
        y_ref[...] = jnp.zeros_like(y_ref)

    @pl.when(nv > 0)
    def _():
        x = _load_packed_tokens(x_ref, 0, EXPERT_ROWS)
        gate = jnp.dot(x, wgb_ref[...], preferred_element_type=F32)
        up = jnp.dot(x, wub_ref[...], preferred_element_type=F32)
        act = (gate * jax.nn.sigmoid(gate) * up).astype(BF16)
        _store_packed_tokens(y_ref, 0, jnp.dot(act, wdb_ref[...], preferred_element_type=F32))


def _combine_kernel(dcur_ref, dnext_ref, h_ref, pw_ref, g_ref, b_ref, y_ref, o_ref, yb0, yb1, yb2, yb3, sems):
    part = COMBINE_PART_ROWS
    bufs = (yb0, yb1, yb2, yb3)
    step = pl.program_id(0)
    last = pl.num_programs(0) - 1

    def copy(src_tok, ybuf, t, slot, sem):
        return pltpu.make_async_copy(_packed_rows(y_ref, src_tok), _packed_rows(ybuf.at[slot], t), sem)

    def issue(dref, col0, ybuf, sem):
        for t0 in range(0, part, INDEX_GROUP):
            src = [[dref[slot, col0 + t0 + k] for slot in range(TOP_K)] for k in range(INDEX_GROUP)]
            for k in range(INDEX_GROUP):
                for slot in range(TOP_K):
                    copy(src[k][slot], ybuf, t0 + k, slot, sem).start(priority=slot % 2)

    def wait(ybuf, sem):
        for slot in range(TOP_K):
            pltpu.make_async_copy(_packed_rows(y_ref, 0, part), ybuf.at[slot], sem).wait()

    def finish(ybuf, r0):
        rows = pl.ds(r0, part)
        pw_lanes = jnp.concatenate([pw_ref[:, rows], jnp.zeros((8 - TOP_K, part), F32)], axis=0)
        pw = pw_lanes.T
        y = (pw[:, 0:1] * _load_packed_tokens(ybuf.at[0], 0, part, F32)
             + pw[:, 1:2] * _load_packed_tokens(ybuf.at[1], 0, part, F32))
        o_ref[rows, :] = _layer_norm(ALPHA * h_ref[rows, :] + y, g_ref[...], b_ref[...], LN_EPS)

    @pl.when(step == 0)
    def _():
        for p in range(2):
            def first(t, _, p=p):
                for slot in range(TOP_K):
                    copy(dcur_ref[slot, p * part + t], bufs[p], t, slot, sems.at[p]).start()
                return 0
            lax.fori_loop(0, part, first, 0)

    for p in range(COMBINE_PHASES):
        ahead = (p + 2) % COMBINE_PHASES
        wait(bufs[p], sems.at[p])
        if p + 2 < COMBINE_PHASES:
            issue(dcur_ref, (p + 2) * part, bufs[ahead], sems.at[ahead])
        else:
            issue(dnext_ref, ahead * part, bufs[ahead], sems.at[ahead])
        finish(bufs[p], p * part)

    @pl.when(step == last)
    def _():
        wait(bufs[0], sems.at[0])
        wait(bufs[1], sems.at[1])


def _tables(seq):
    log_g = jnp.log1p(-jnp.power(2.0, -5.0 - jnp.arange(RET_HEADS, dtype=F32)))
    i = jnp.arange(CHUNK, dtype=F32)
    rel = i[:, None] - i[None, :]
    dmask = jnp.where(rel[None] >= 0, jnp.exp(jnp.maximum(rel, 0.0)[None] * log_g[:, None, None]), 0.0)
    dmask = dmask.reshape(RET_HEADS * CHUNK, CHUNK)
    lg_lane = jnp.repeat(log_g, RET_DK)
    xi = jnp.exp((i + 1)[:, None] * lg_lane[None, :])
    zeta = jnp.exp((CHUNK - 1 - i)[:, None] * lg_lane[None, :])
    decay = jnp.broadcast_to(jnp.exp(CHUNK * lg_lane)[:, None], (RET_QK_WIDTH, RET_WIDTH))
    bd = (jnp.arange(RET_QK_WIDTH)[:, None] // RET_DK == jnp.arange(RET_WIDTH)[None, :] // RET_DV).astype(F32)
    zeta_meta = zeta[CHUNK - N_META:, :]
    half = RET_DK // 2
    inv = ROPE_BASE ** (-jnp.arange(half, dtype=F32) / half)
    pos = jnp.arange(N_META + seq, dtype=F32)
    ang = pos[:, None] * inv[None, :]
    cos = jnp.tile(jnp.cos(ang), (1, 2 * RET_HEADS))
    sin = jnp.tile(jnp.concatenate([-jnp.sin(ang), jnp.sin(ang)], axis=1), (1, RET_HEADS))
    return dict(dmask=dmask, xi=xi, zeta=zeta, decay=decay, bd=bd, zeta_meta=zeta_meta,
                cos_meta=cos[:N_META], sin_meta=sin[:N_META], cos=cos[N_META:], sin=sin[N_META:])


def _full(shape):
    return pl.BlockSpec(shape, lambda *_: (0,) * len(shape))


def kernel(x, meta_tokens, ln_emb_g, ln_emb_b, w_in, pool_w, pool_b, pool_scale, w_out, ln1_g, ln1_b, router_group_w, router_group_b, router_expert_w, router_expert_b, expert_w_gate, expert_w_up, expert_w_down, ln2_g, ln2_b):
    batch, seq, d = x.shape
    assert d == D_MODEL and seq % MIX_ROWS == 0 and MIX_ROWS <= EXPERT_ROWS
    n_tok = batch * seq
    t_blocks = seq // MIX_ROWS
    tb = _tables(seq)

    row = lambda a: a.reshape(1, -1).astype(F32)
    win_b = w_in[0].astype(BF16)
    wout_b = w_out[0].astype(BF16)
    poolw_b = pool_w[0].astype(BF16)
    wr = jnp.zeros((ROUTER_ROWS, D_MODEL), F32)
    wr = wr.at[0:N_GROUPS].set(router_group_w[0].T).at[8:8 + N_EXPERTS].set(router_expert_w[0].T).astype(BF16)
    br = jnp.zeros((ROUTER_ROWS, 1), F32)
    br = br.at[0:N_GROUPS, 0].set(router_group_b[0]).at[8:8 + N_EXPERTS, 0].set(router_expert_b[0])

    s0, tail0 = pl.pallas_call(
        _meta_kernel,
        out_shape=(jax.ShapeDtypeStruct((RET_QK_WIDTH, RET_WIDTH), F32), jax.ShapeDtypeStruct((N_META, POOL_WIDTH), F32)),
        name="meta_prep",
    )(meta_tokens.astype(F32), row(ln_emb_g), row(ln_emb_b), win_b, tb["cos_meta"], tb["sin_meta"], tb["zeta_meta"], tb["bd"])

    tok_spec = pl.BlockSpec((None, MIX_ROWS, D_MODEL), lambda b, j: (b, j, 0))
    pair_spec = pl.BlockSpec((TOP_K, MIX_ROWS), lambda b, j: (0, b * t_blocks + j))
    rope_spec = pl.BlockSpec((MIX_ROWS, RET_QK_WIDTH), lambda b, j: (j, 0))
    n_blocks = (n_tok * TOP_K) // EXPERT_ROWS + N_EXPERTS
    assert n_blocks + 2 <= BLOCK_TABLE_LANES
    packed_block = (EXPERT_ROWS * PACKED_SUBLANES, LANES)
    sorted_shape = (n_blocks * EXPERT_ROWS * PACKED_SUBLANES, LANES)
    buf_shape = ((n_blocks + 2) * EXPERT_ROWS * PACKED_SUBLANES, LANES)
    stage_block = (MIX_ROWS * PACKED_SUBLANES, LANES)
    h1, dest, pair_w, blk_table, buf = pl.pallas_call(
        _mixer_kernel,
        grid=(batch, t_blocks),
        in_specs=[tok_spec, _full((1, D_MODEL)), _full((1, D_MODEL)), _full((D_MODEL, IN_COLS)), _full((D_MODEL, D_MODEL)),
                  _full((POOL_GROUPS, POOL_CH, POOL_CH)), _full((1, POOL_WIDTH)), _full((1, POOL_WIDTH)),
                  _full((1, D_MODEL)), _full((1, D_MODEL)), rope_spec, rope_spec,
                  _full((RET_HEADS * CHUNK, CHUNK)), _full((CHUNK, RET_QK_WIDTH)), _full((CHUNK, RET_QK_WIDTH)),
                  _full((RET_QK_WIDTH, RET_WIDTH)), _full((RET_QK_WIDTH, RET_WIDTH)),
                  _full((RET_QK_WIDTH, RET_WIDTH)), _full((N_META, POOL_WIDTH)),
                  _full((ROUTER_ROWS, D_MODEL)), _full((ROUTER_ROWS, 1))],
        out_specs=[tok_spec, pair_spec, pair_spec,
                   _full((8, BLOCK_TABLE_LANES)),
                   pl.BlockSpec(memory_space=pl.ANY)],
        out_shape=[jax.ShapeDtypeStruct((batch, seq, D_MODEL), F32),
                   jax.ShapeDtypeStruct((TOP_K, n_tok), jnp.int32),
                   jax.ShapeDtypeStruct((TOP_K, n_tok), F32),
                   jax.ShapeDtypeStruct((8, BLOCK_TABLE_LANES), jnp.int32),
                   jax.ShapeDtypeStruct(buf_shape, jnp.uint32)],
        scratch_shapes=[pltpu.VMEM((RET_QK_WIDTH, RET_WIDTH), F32),
                        pltpu.VMEM((N_META, POOL_WIDTH), F32),
                        pltpu.VMEM((MIX_ROWS + N_META, POOL_WIDTH), F32),
                        pltpu.VMEM((MIX_ROWS, IN_COLS), F32),
                        pltpu.VMEM((MIX_ROWS, D_MODEL), F32),
                        pltpu.VMEM((MIX_ROWS, D_MODEL), BF16),
                        pltpu.VMEM((MIX_ROWS, D_MODEL), BF16),
                        pltpu.VMEM((MIX_ROWS, MIX_ROWS), BF16),
                        pltpu.VMEM((N_EXPERTS, LANES), F32),
                        pltpu.VMEM((N_EXPERTS, LANES), F32),
                        pltpu.VMEM((8, LANES), F32),
                        pltpu.VMEM((8, BLOCK_TABLE_LANES), F32),
                        pltpu.VMEM((2,) + stage_block, jnp.uint32),
                        pltpu.VMEM(packed_block, jnp.uint32),
                        pltpu.VMEM((TOP_K, MIX_ROWS), jnp.int32),
                        pltpu.SMEM((TOP_K, MIX_ROWS), jnp.int32),
                        pltpu.VMEM((N_EXPERTS, LANES), jnp.int32),
                        pltpu.VMEM((N_EXPERTS, LANES), jnp.int32),
                        pltpu.SMEM((2, N_EXPERTS, LANES), jnp.int32),
                        pltpu.SemaphoreType.DMA((2,)),
                        pltpu.SemaphoreType.DMA, pltpu.SemaphoreType.DMA],
        compiler_params=pltpu.CompilerParams(dimension_semantics=("arbitrary", "arbitrary"),
                                             vmem_limit_bytes=VMEM_LIMIT),
        name="mixer",
    )(x, row(ln_emb_g), row(ln_emb_b), win_b, wout_b, poolw_b, row(pool_b[0]), row(pool_scale[0]),
      row(ln1_g[0]), row(ln1_b[0]), tb["cos"], tb["sin"], tb["dmask"], tb["xi"], tb["zeta"], tb["decay"], tb["bd"],
      s0, tail0, wr, br)

    bidx = jnp.arange(n_blocks, dtype=jnp.int32)
    owner = blk_table[0, :n_blocks]
    key = owner * BLOCK_TABLE_LANES + bidx
    place = jnp.sum(key[None, :] < key[:, None], axis=1)
    order = jnp.sum(jnp.where(place[None, :] == bidx[:, None], bidx[None, :], 0), axis=1).astype(jnp.int32)
    owner_in_order = jnp.sum(jnp.where(place[None, :] == bidx[:, None], owner[None, :], 0), axis=1)
    n_used = jnp.sum(owner < N_EXPERTS)
    used = bidx < n_used
    last_used = jnp.maximum(n_used - 1, 0)
    blk_in = jnp.where(used, order, order[last_used]).astype(jnp.int32)
    blk_e = jnp.where(used, owner_in_order, owner_in_order[last_used]).astype(jnp.int32)
    blk_nv = used.astype(jnp.int32)
    later = jnp.logical_and(owner_in_order[None, :] > blk_e[:, None], used[None, :])
    blk_next = jnp.min(jnp.where(later, owner_in_order[None, :], N_EXPERTS), axis=1)
    blk_next = jnp.where(blk_next < N_EXPERTS, blk_next, -1).astype(jnp.int32)

    y_sorted = pl.pallas_call(
        _expert_kernel,
        grid_spec=pltpu.PrefetchScalarGridSpec(
            num_scalar_prefetch=5,
            grid=(n_blocks,),
            in_specs=[pl.BlockSpec(packed_block, lambda i, be, nv, bi, bo, nx: (bi[i], 0)),
                      pl.BlockSpec(memory_space=pl.ANY), pl.BlockSpec(memory_space=pl.ANY),
                      pl.BlockSpec(memory_space=pl.ANY)],
            out_specs=pl.BlockSpec(packed_block, lambda i, be, nv, bi, bo, nx: (bo[i], 0)),
            scratch_shapes=[pltpu.VMEM((D_MODEL, D_EXPERT), BF16), pltpu.VMEM((D_MODEL, D_EXPERT), BF16),
                            pltpu.VMEM((D_EXPERT, D_MODEL), BF16),
                            pltpu.VMEM((D_MODEL, D_EXPERT), F32), pltpu.VMEM((D_MODEL, D_EXPERT), F32),
                            pltpu.VMEM((D_EXPERT, D_MODEL), F32), pltpu.SemaphoreType.DMA((3,))],
        ),
        out_shape=jax.ShapeDtypeStruct(sorted_shape, jnp.uint32),
        compiler_params=pltpu.CompilerParams(dimension_semantics=("arbitrary",), vmem_limit_bytes=VMEM_LIMIT),
        name="experts",
    )(blk_e, blk_nv, blk_in, order, blk_next, buf, expert_w_gate[0], expert_w_up[0], expert_w_down[0])

    comb_rows = COMBINE_PHASES * COMBINE_PART_ROWS
    comb_steps = n_tok // comb_rows
    assert n_tok % comb_rows == 0
    rows_spec = pl.BlockSpec((comb_rows, D_MODEL), lambda i: (i, 0))
    out = pl.pallas_call(
        _combine_kernel,
        grid=(comb_steps,),
        in_specs=[pl.BlockSpec((TOP_K, comb_rows), lambda i: (0, i), memory_space=pltpu.SMEM),
                  pl.BlockSpec((TOP_K, comb_rows), lambda i: (0, jnp.minimum(i + 1, comb_steps - 1)),
                               memory_space=pltpu.SMEM),
                  rows_spec,
                  pl.BlockSpec((TOP_K, comb_rows), lambda i: (0, i)),
                  pl.BlockSpec((1, D_MODEL), lambda i: (0, 0)), pl.BlockSpec((1, D_MODEL), lambda i: (0, 0)),
                  pl.BlockSpec(memory_space=pl.ANY)],
        out_specs=rows_spec,
        out_shape=jax.ShapeDtypeStruct((n_tok, D_MODEL), F32),
        scratch_shapes=[pltpu.VMEM((TOP_K, COMBINE_PART_ROWS * PACKED_SUBLANES, LANES), jnp.uint32)] * COMBINE_PHASES
                       + [pltpu.SemaphoreType.DMA((COMBINE_PHASES,))],
        compiler_params=pltpu.CompilerParams(dimension_semantics=("arbitrary",), vmem_limit_bytes=VMEM_LIMIT),
        name="combine",
    )(dest, dest, h1.reshape(n_tok, D_MODEL), pair_w, row(ln2_g[0]), row(ln2_b[0]), y_sorted)
    return out.reshape(batch, seq, D_MODEL)
```
